```python
import math
import jax, jax.numpy as jnp
from jax import lax
import numpy as np

D_MODEL = 1024
BATCH = 8
SEQ = 4096
DEPTH = 1

HEAD_DIM = 64
RWKV_HEADS = 12
RWKV_WIDTH = RWKV_HEADS * HEAD_DIM
DECAY_LORA = 64
ICLR_LORA = 64
GATE_LORA = 128
ATTN_GROUPS = ((128, 1), (512, 4), (2048, 16))
HEADS_PER_GROUP = 4
ATTN_HEADS = HEADS_PER_GROUP * len(ATTN_GROUPS)
ATTN_WIDTH = ATTN_HEADS * HEAD_DIM
ATTN_OUT_WIDTH = HEADS_PER_GROUP * HEAD_DIM
ATTN_BLOCK = 128
ROPE_THETA = 10000.0
D_FF = -(-8 * D_MODEL // (3 * 256)) * 256
SHIFT_WIDTH = 3 * RWKV_WIDTH + DECAY_LORA + ICLR_LORA + GATE_LORA
IN_WIDTH = SHIFT_WIDTH + 3 * ATTN_WIDTH + 2 * D_MODEL
NORM_EPS = 1e-6
GN_EPS = 64e-5

kernel_name = "hybrid_rwkv7_dilated_attn_gated_block"


def _split(t, sizes):
    idx = np.cumsum(sizes)[:-1].tolist()
    return jnp.split(t, idx, axis=-1)


def rms_norm(x, g):
    xf = x.astype(jnp.float32)
    y = xf * lax.rsqrt(jnp.mean(xf * xf, axis=-1, keepdims=True) + NORM_EPS)
    return (y * g.astype(jnp.float32)).astype(x.dtype)


def modulate(h, shift, scale):
    return h * (1.0 + scale[:, None, :]) + shift[:, None, :]


def token_shift(p):
    return jnp.pad(p, ((0, 0), (1, 0), (0, 0)))[:, :-1]


def apply_rope(t, pos):
    half = t.shape[-1] // 2
    inv_freq = ROPE_THETA ** (-jnp.arange(half, dtype=jnp.float32) / half)
    ang = pos.astype(jnp.float32)[:, None] * inv_freq[None, :]
    cos, sin = jnp.cos(ang), jnp.sin(ang)
    tf = t.astype(jnp.float32)
    t1, t2 = tf[..., :half], tf[..., half:]
    return jnp.concatenate([t1 * cos - t2 * sin, t2 * cos + t1 * sin], axis=-1).astype(t.dtype)


def rwkv7_recurrence(r, w, k, v, kk, a):
    B, S, H, N = r.shape

    def step(state, inp):
        r_t, w_t, k_t, v_t, kk_t, a_t = inp
        sa = -jnp.einsum('bhvk,bhk->bhv', state, kk_t)
        state = (state * w_t[:, :, None, :]
                 + sa[..., None] * (kk_t * a_t)[:, :, None, :]
                 + v_t[..., None] * k_t[:, :, None, :])
        y = jnp.einsum('bhvk,bhk->bhv', state, r_t)
        return state, y

    xs = tuple(jnp.moveaxis(t.astype(jnp.float32), 1, 0) for t in (r, w, k, v, kk, a))
    init = jnp.zeros((B, H, N, N), jnp.float32)
    _, ys = lax.scan(step, init, xs)
    return jnp.moveaxis(ys, 0, 1)


def rwkv7_mix(pr, pk, pv, pw, pa, pg, w0, w2, a0, a2, g2, k_k, k_a, r_k, lnx_w, lnx_b, w_a):
    B, S, _ = pr.shape
    f32 = jnp.float32
    heads = lambda t: t.reshape(B, S, RWKV_HEADS, HEAD_DIM)
    wf = (w0 + jnp.tanh(pw) @ w2).astype(f32)
    decay = jnp.exp(-jnp.exp(-jax.nn.softplus(-wf) - 0.5))
    a = jax.nn.sigmoid((a0 + pa @ a2).astype(f32))
    gate = (jax.nn.sigmoid(pg) @ g2).astype(f32)
    kk = heads((pk * k_k).astype(f32))
    kk = kk / jnp.maximum(jnp.linalg.norm(kk, axis=-1, keepdims=True), 1e-12)
    k = pk.astype(f32) * (1.0 + (a - 1.0) * k_a.astype(f32))
    rf, vf = heads(pr.astype(f32)), heads(pv.astype(f32))
    y = rwkv7_recurrence(rf, heads(decay), heads(k), vf, kk, heads(a))
    mean = jnp.mean(y, axis=-1, keepdims=True)
    var = jnp.mean(jnp.square(y - mean), axis=-1, keepdims=True)
    yn = ((y - mean) * lax.rsqrt(var + GN_EPS)).reshape(B, S, RWKV_WIDTH)
    yn = yn * lnx_w.astype(f32) + lnx_b.astype(f32)
    bonus = (jnp.sum(rf * heads(k) * r_k.astype(f32), axis=-1, keepdims=True) * vf).reshape(B, S, RWKV_WIDTH)
    return ((yn + bonus) * gate).astype(pr.dtype) @ w_a


def dilated_window_attn(q, k, v, window, dilation):
    B, H, S, hd = q.shape
    span = window // dilation
    n_prev = -(-span // ATTN_BLOCK)
    chunk = dilation * ATTN_BLOCK
    s_pad = -(-S // chunk) * chunk
    L = s_pad // dilation
    nb = L // ATTN_BLOCK

    def to_blocks(t):
        t = jnp.pad(t, ((0, 0), (0, 0), (0, s_pad - S), (0, 0)))
        t = t.reshape(B, H, L, dilation, hd).transpose(0, 1, 3, 2, 4)
        return t.reshape(B, H, dilation, nb, ATTN_BLOCK, hd)

    def with_prev(t):
        padded = jnp.pad(t, ((0, 0), (0, 0), (0, 0), (n_prev, 0), (0, 0), (0, 0)))
        return jnp.concatenate([padded[:, :, :, i:i + nb] for i in range(n_prev + 1)], axis=4)

    qb, kb, vb = to_blocks(q), to_blocks(k), to_blocks(v)
    kc, vc = with_prev(kb), with_prev(vb)
    s = jnp.einsum('bhrnqd,bhrnkd->bhrnqk', qb, kc, preferred_element_type=jnp.float32)
    s = s * (HEAD_DIM ** -0.5)
    n_keys = (n_prev + 1) * ATTN_BLOCK
    qi = jnp.arange(ATTN_BLOCK)[:, None]
    kj = jnp.arange(n_keys)[None, :]
    dist = qi + n_prev * ATTN_BLOCK - kj
    key_pos = jnp.arange(nb)[:, None, None] * ATTN_BLOCK + kj[None] - n_prev * ATTN_BLOCK
    valid = (dist >= 0) & (dist <= span) & (key_pos >= 0)
    s = jnp.where(valid, s, -jnp.inf)
    lse = jax.nn.logsumexp(s, axis=-1)
    p = jnp.exp(s - lse[..., None])
    o = jnp.einsum('bhrnqk,bhrnkd->bhrnqd', p, vc.astype(jnp.float32))
    o = o.reshape(B, H, dilation, L, hd).transpose(0, 1, 3, 2, 4).reshape(B, H, s_pad, hd)[:, :, :S]
    lse = lse.reshape(B, H, dilation, L).transpose(0, 1, 3, 2).reshape(B, H, s_pad)[:, :, :S]
    return o, lse


def dilated_attention_mix(pq, pk, pv, w_b):
    B, S, _ = pq.shape
    heads = lambda t: t.reshape(B, S, ATTN_HEADS, HEAD_DIM).transpose(0, 2, 1, 3)
    pos = jnp.arange(S)
    q, k, v = apply_rope(heads(pq), pos), apply_rope(heads(pk), pos), heads(pv)
    outs, lses = [], []
    for g, (window, dilation) in enumerate(ATTN_GROUPS):
        sl = slice(g * HEADS_PER_GROUP, (g + 1) * HEADS_PER_GROUP)
        o, l = dilated_window_attn(q[:, sl], k[:, sl], v[:, sl], window, dilation)
        outs.append(o)
        lses.append(l)
    outs = jnp.stack(outs)
    wts = jax.nn.softmax(jnp.stack(lses), axis=0)
    o = jnp.sum(wts[..., None] * outs, axis=0)
    o = o.transpose(0, 2, 1, 3).reshape(B, S, ATTN_OUT_WIDTH).astype(pq.dtype)
    return o @ w_b


def setup_inputs(seed: int = 0) -> dict:
    key = jax.random.key(seed)
    ks = iter(jax.random.split(key, 32))
    f32 = jnp.float32
    nrm = lambda shape, sc: jax.random.normal(next(ks), shape, f32) * sc
    L, D, RW = DEPTH, D_MODEL, RWKV_WIDTH
    return {
        "x": nrm((BATCH, SEQ, D), 1.0),
        "c": nrm((BATCH, D), 1.0),
        "w_ada": nrm((L, D, 6 * D), 0.5 * D ** -0.5),
        "b_ada": nrm((L, 6 * D), 0.02),
        "g_pre_mix": 1.0 + nrm((L, D), 0.05),
        "g_post_mix": 1.0 + nrm((L, D), 0.05),
        "w_in": nrm((L, D, IN_WIDTH), D ** -0.5),
        "mu_shift": jax.random.uniform(next(ks), (L, SHIFT_WIDTH), f32, 0.0, 1.0),
        "w0": jax.random.uniform(next(ks), (L, RW), f32, -5.0, -0.5),
        "w2": nrm((L, DECAY_LORA, RW), 0.1 * DECAY_LORA ** -0.5),
        "a0": nrm((L, RW), 0.5),
        "a2": nrm((L, ICLR_LORA, RW), 0.5 * ICLR_LORA ** -0.5),
        "g2": nrm((L, GATE_LORA, RW), GATE_LORA ** -0.5),
        "k_k": 0.85 + nrm((L, RW), 0.05),
        "k_a": 1.0 + nrm((L, RW), 0.05),
        "r_k": nrm((L, RWKV_HEADS, HEAD_DIM), 0.1),
        "lnx_w": 1.0 + nrm((L, RW), 0.05),
        "lnx_b": nrm((L, RW), 0.02),
        "w_a": nrm((L, RW, D), RW ** -0.5),
        "w_b": nrm((L, ATTN_OUT_WIDTH, D), ATTN_OUT_WIDTH ** -0.5),
        "w_out": nrm((L, D, D), D ** -0.5),
        "g_pre_ffn": 1.0 + nrm((L, D), 0.05),
        "g_post_ffn": 1.0 + nrm((L, D), 0.05),
        "w_ffn_in": nrm((L, D, 2 * D_FF), D ** -0.5),
        "w_ffn_out": nrm((L, D_FF, D), D_FF ** -0.5),
    }


def reference(x, c, w_ada, b_ada, g_pre_mix, g_post_mix, w_in, mu_shift, w0, w2, a0, a2, g2,
              k_k, k_a, r_k, lnx_w, lnx_b, w_a, w_b, w_out, g_pre_ffn, g_post_ffn,
              w_ffn_in, w_ffn_out):
    for i in range(DEPTH):
        mod = jax.nn.silu(c) @ w_ada[i] + b_ada[i]
        shift_m, scale_m, gate_m, shift_f, scale_f, gate_f = jnp.split(mod, 6, axis=-1)

        h = modulate(rms_norm(x, g_pre_mix[i]), shift_m, scale_m)
        p = h @ w_in[i]
        p_shift, pq, pk_att, pv_att, p_ga, p_gb = _split(
            p, [SHIFT_WIDTH, ATTN_WIDTH, ATTN_WIDTH, ATTN_WIDTH, D_MODEL, D_MODEL])
        p_shift = p_shift + mu_shift[i] * (token_shift(p_shift) - p_shift)
        pr, pk, pv, pw, pa, pg = _split(
            p_shift, [RWKV_WIDTH, RWKV_WIDTH, RWKV_WIDTH, DECAY_LORA, ICLR_LORA, GATE_LORA])
        y_a = rwkv7_mix(pr, pk, pv, pw, pa, pg, w0[i], w2[i], a0[i], a2[i], g2[i],
                        k_k[i], k_a[i], r_k[i], lnx_w[i], lnx_b[i], w_a[i])
        y_b = dilated_attention_mix(pq, pk_att, pv_att, w_b[i])
        merged = jax.nn.sigmoid(p_ga) * y_a + jax.nn.sigmoid(p_gb) * y_b
        x = x + gate_m[:, None, :] * rms_norm(merged @ w_out[i], g_post_mix[i])

        h = modulate(rms_norm(x, g_pre_ffn[i]), shift_f, scale_f)
        u, gt = jnp.split(h @ w_ffn_in[i], 2, axis=-1)
        f = (jax.nn.silu(u) * gt) @ w_ffn_out[i]
        x = x + gate_f[:, None, :] * rms_norm(f, g_post_ffn[i])
    return x
```

```python
import functools
import math

import jax
import jax.numpy as jnp
from jax import lax
from jax.experimental import pallas as pl
from jax.experimental.pallas import tpu as pltpu

D_MODEL = 1024
HEAD_DIM = 64
RWKV_HEADS = 12
RWKV_WIDTH = RWKV_HEADS * HEAD_DIM
DECAY_LORA = 64
ICLR_LORA = 64
GATE_LORA = 128
LORA_WIDTH = DECAY_LORA + ICLR_LORA + GATE_LORA
ATTN_GROUPS = ((128, 1), (512, 4), (2048, 16))
HEADS_PER_GROUP = 4
GROUP_WIDTH = HEADS_PER_GROUP * HEAD_DIM
ATTN_WIDTH = len(ATTN_GROUPS) * GROUP_WIDTH
ATTN_BLOCK = 128
ROPE_THETA = 10000.0
SHIFT_WIDTH = 3 * RWKV_WIDTH + LORA_WIDTH
NORM_EPS = 1e-6
GN_EPS = 64e-5

CHUNK = 64
INV_BASE = 8
LANES = 128
NEG_BIG = -1e30
VMEM_LIMIT = 56 * 1024 * 1024

F32 = jnp.float32
BF16 = jnp.bfloat16


def _bf(x):
    return x.astype(BF16)


def _div(x, n):
    assert n & (n - 1) == 0
    return jnp.right_shift(x, n.bit_length() - 1)


def _rem(x, n):
    assert n & (n - 1) == 0
    return jnp.bitwise_and(x, n - 1)


def _sigmoid(x):
    return 1.0 / (1.0 + jnp.exp(-x))


def _mm(a, b):
    return jnp.dot(_bf(a), _bf(b), preferred_element_type=F32)


def _mm_nt(a, b):
    return lax.dot_general(_bf(a), _bf(b), (((1,), (1,)), ((), ())), preferred_element_type=F32)


def _mm_tn(a, b):
    return lax.dot_general(_bf(a), _bf(b), (((0,), (0,)), ((), ())), preferred_element_type=F32)


def _split_hi_lo(x):
    hi = _bf(x)
    lo = _bf(x - hi.astype(F32))
    return hi, lo


def _mm_exact_lhs(m_bf, x):
    hi, lo = _split_hi_lo(x)
    return (jnp.dot(m_bf, hi, preferred_element_type=F32)
            + jnp.dot(m_bf, lo, preferred_element_type=F32))


def _head_sum(x):
    tile = 2 * LANES
    ri = _div(lax.broadcasted_iota(jnp.int32, (tile, tile), 0), HEAD_DIM)
    ci = _div(lax.broadcasted_iota(jnp.int32, (tile, tile), 1), HEAD_DIM)
    ones_bd = jnp.where(ri == ci, 1.0, 0.0).astype(BF16)
    hi, lo = _split_hi_lo(x)
    parts = []
    for j in range(x.shape[-1] // tile):
        sl = slice(j * tile, (j + 1) * tile)
        parts.append(jnp.dot(hi[:, sl], ones_bd, preferred_element_type=F32)
                     + jnp.dot(lo[:, sl], ones_bd, preferred_element_type=F32))
    return jnp.concatenate(parts, axis=-1)


def _norm_modulate(x, g, shift, scale):
    y = x * lax.rsqrt(jnp.mean(x * x, axis=-1, keepdims=True) + NORM_EPS)
    return (y * g) * (1.0 + scale) + shift


def _rms(x, g):
    return x * lax.rsqrt(jnp.mean(x * x, axis=-1, keepdims=True) + NORM_EPS) * g


def _mod_kernel(c_ref, w_ref, b_ref, o_ref):
    c = c_ref[...]
    o_ref[...] = jnp.dot(c * _sigmoid(c), w_ref[...], preferred_element_type=F32,
                         precision=lax.Precision.HIGHEST) + b_ref[...]


def _mod(c, w_ada, b_ada):
    bsz, d = c.shape
    n = w_ada.shape[1]
    tn = d
    return pl.pallas_call(
        _mod_kernel,
        grid=(n // tn,),
        in_specs=[pl.BlockSpec((bsz, d), lambda j: (0, 0)),
                  pl.BlockSpec((d, tn), lambda j: (0, j)),
                  pl.BlockSpec((1, tn), lambda j: (0, j))],
        out_specs=pl.BlockSpec((bsz, tn), lambda j: (0, j)),
        out_shape=jax.ShapeDtypeStruct((bsz, n), F32),
        name="mod",
    )(c, w_ada, b_ada.reshape(1, n))


def _rwkv_prep_kernel(x_ref, shift_ref, scale_ref, g_ref, w_ref, mu_ref, wl_ref, w0_ref, a0_ref,
                      kk_ref, ka_ref, rk_ref,
                      aq_o, rq_o, bk_o, kd_o, bg_o, kg_o, v_o, gc_o, bonus_o, gate_o,
                      carry_ref):
    i = pl.program_id(1)
    tm = x_ref.shape[1]
    rw = RWKV_WIDTH

    @pl.when(i == 0)
    def _():
        carry_ref[...] = jnp.zeros_like(carry_ref)

    h = _norm_modulate(x_ref[0], g_ref[...], shift_ref[0], scale_ref[0])
    p = jnp.dot(_bf(h), w_ref[...], preferred_element_type=F32)

    prev_row = carry_ref[7:8, :]
    carry_ref[...] = p[tm - 8:, :]
    row = lax.broadcasted_iota(jnp.int32, p.shape, 0)
    shifted = jnp.where(row == 0, prev_row, pltpu.roll(p, 1, axis=0))
    ps = p + mu_ref[...] * (shifted - p)

    pr, pk, pv = ps[:, :rw], ps[:, rw:2 * rw], ps[:, 2 * rw:3 * rw]
    lora_in = ps[:, 3 * rw:]
    lane = lax.broadcasted_iota(jnp.int32, lora_in.shape, 1)
    z = jnp.where(lane < DECAY_LORA, jnp.tanh(lora_in),
                  jnp.where(lane < DECAY_LORA + ICLR_LORA, lora_in, _sigmoid(lora_in)))
    lo = jnp.dot(_bf(z), wl_ref[...], preferred_element_type=F32)
    wf = w0_ref[...] + lo[:, :rw]
    a = _sigmoid(a0_ref[...] + lo[:, rw:2 * rw])
    gate_o[0] = lo[:, 2 * rw:]

    lw = -math.exp(-0.5) * _sigmoid(wf)

    kkr = pk * kk_ref[...]
    kkn = kkr * (1.0 / jnp.maximum(jnp.sqrt(_head_sum(kkr * kkr)), 1e-12))
    k = pk * (1.0 + (a - 1.0) * ka_ref[...])
    b = kkn * a
    bonus_o[0] = _head_sum(pr * k * rk_ref[...]) * pv
    v_o[0] = pv

    ri = lax.broadcasted_iota(jnp.int32, (tm, tm), 0)
    ci = lax.broadcasted_iota(jnp.int32, (tm, tm), 1)
    same = _div(ri, CHUNK) == _div(ci, CHUNK)
    lower = jnp.where(same & (ci <= ri), 1.0, 0.0).astype(BF16)
    upper = jnp.where(same & (ci > ri), 1.0, 0.0).astype(BF16)
    cum = _mm_exact_lhs(lower, lw)
    rest = _mm_exact_lhs(upper, lw)
    nck = tm // CHUNK
    si = lax.broadcasted_iota(jnp.int32, (nck, tm), 0)
    sj = _div(lax.broadcasted_iota(jnp.int32, (nck, tm), 1), CHUNK)
    total = _mm_exact_lhs(jnp.where(si == sj, 1.0, 0.0).astype(BF16), lw)
    gc_o[0, 0] = jnp.exp(total)

    e_cum = jnp.exp(cum)
    e_inv = jnp.exp(-cum)
    e_rest = jnp.exp(rest)
    aq_o[0] = -kkn * jnp.exp(cum - lw)
    rq_o[0] = pr * e_cum
    bk_o[0] = b * e_inv
    kd_o[0] = k * e_inv
    bg_o[0] = b * e_rest
    kg_o[0] = k * e_rest


def _rwkv_prep(x, mod3, g_pre, w_rwkv, mu, w_lora, w0, a0, k_k, k_a, r_k, tm):
    bsz, seq, d = x.shape
    rw = RWKV_WIDTH
    nck = tm // CHUNK
    row_spec = pl.BlockSpec((1, tm, rw), lambda b, i: (b, i, 0))
    vec = lambda n: pl.BlockSpec((1, n), lambda b, i: (0, 0))
    big = jax.ShapeDtypeStruct((bsz, seq, rw), F32)
    return pl.pallas_call(
        _rwkv_prep_kernel,
        grid=(bsz, seq // tm),
        in_specs=[pl.BlockSpec((1, tm, d), lambda b, i: (b, i, 0)),
                  pl.BlockSpec((1, 1, d), lambda b, i: (b, 0, 0)),
                  pl.BlockSpec((1, 1, d), lambda b, i: (b, 0, 1)),
                  vec(d),
                  pl.BlockSpec((d, SHIFT_WIDTH), lambda b, i: (0, 0)),
                  vec(SHIFT_WIDTH),
                  pl.BlockSpec((LORA_WIDTH, 3 * rw), lambda b, i: (0, 0)),
                  vec(rw), vec(rw), vec(rw), vec(rw), vec(rw)],
        out_specs=[row_spec] * 7
                  + [pl.BlockSpec((1, 1, nck, rw), lambda b, i: (b, i, 0, 0)), row_spec, row_spec],
        out_shape=[big] * 7
                  + [jax.ShapeDtypeStruct((bsz, seq // tm, nck, rw), F32), big, big],
        scratch_shapes=[pltpu.VMEM((8, SHIFT_WIDTH), F32)],
        compiler_params=pltpu.CompilerParams(
            dimension_semantics=("arbitrary", "arbitrary"), vmem_limit_bytes=VMEM_LIMIT),
        name="rwkv_prep",
    )(x, mod3, mod3, g_pre, w_rwkv, mu, w_lora, w0, a0, k_k, k_a, r_k)


def _unit_lower_inverse(a, masks):
    eye, diag_mask, merge_masks = masks
    ad = jnp.where(diag_mask, a, 0.0)
    t = eye + ad
    p = ad
    m = 1
    while 2 * m < INV_BASE:
        p = _mm(p, p)
        t = t + _mm(t, p)
        m *= 2
    for mk in merge_masks:
        t = t + _mm(t, _mm(jnp.where(mk, a, 0.0), t))
    return t


def _rwkv_scan_kernel(aq_ref, rq_ref, bk_ref, kd_ref, bg_ref, kg_ref, v_ref, gc_ref, y_ref, st_ref):
    c = pl.program_id(1)

    @pl.when(c == 0)
    def _():
        st_ref[...] = jnp.zeros_like(st_ref)

    ri = lax.broadcasted_iota(jnp.int32, (CHUNK, CHUNK), 0)
    ci = lax.broadcasted_iota(jnp.int32, (CHUNK, CHUNK), 1)
    strict = ri > ci
    incl = ri >= ci
    eye = jnp.where(ri == ci, 1.0, 0.0)
    diag_mask = _div(ri, INV_BASE) == _div(ci, INV_BASE)
    merge_masks = []
    bsize = INV_BASE
    while bsize < CHUNK:
        merge_masks.append((_div(ri, 2 * bsize) == _div(ci, 2 * bsize))
                           & (_div(ri, bsize) != _div(ci, bsize)))
        bsize *= 2
    masks = (eye, diag_mask, merge_masks)

    for j in range(RWKV_WIDTH // LANES):
        ls = slice(j * LANES, (j + 1) * LANES)
        aq2, rq2, bk2, kd2 = aq_ref[0, :, ls], rq_ref[0, :, ls], bk_ref[0, :, ls], kd_ref[0, :, ls]
        bg2, kg2, v2, gc2 = bg_ref[0, :, ls], kg_ref[0, :, ls], v_ref[0, :, ls], gc_ref[0, 0, :, ls]
        ys = []
        for e in range(LANES // HEAD_DIM):
            hs = slice(e * HEAD_DIM, (e + 1) * HEAD_DIM)
            hd = j * (LANES // HEAD_DIM) + e
            aq, rq, bk, kd = aq2[:, hs], rq2[:, hs], bk2[:, hs], kd2[:, hs]
            bg, kg, v, gc = bg2[:, hs], kg2[:, hs], v2[:, hs], gc2[:, hs]
            qs = jnp.concatenate([aq, rq], axis=0)
            ab = _mm_nt(qs, bk)
            ak = _mm_nt(qs, kd)
            a_ab = jnp.where(strict, ab[:CHUNK], 0.0)
            a_ak = jnp.where(strict, ak[:CHUNK], 0.0)
            a_rb = jnp.where(incl, ab[CHUNK:], 0.0)
            a_rk = jnp.where(incl, ak[CHUNK:], 0.0)
            t = _unit_lower_inverse(a_ab, masks)
            s0 = st_ref[hd]
            xs = _mm_nt(qs, s0)
            u = _mm(t, xs[:CHUNK] + _mm(a_ak, v))
            ys.append(xs[CHUNK:] + _mm(a_rb, u) + _mm(a_rk, v))
            st_ref[hd] = s0 * gc + _mm_tn(u, bg) + _mm_tn(v, kg)
        y_ref[0, :, ls] = jnp.concatenate(ys, axis=-1)


def _rwkv_scan(aq, rq, bk, kd, bg, kg, v, gc):
    bsz, seq, rw = aq.shape
    nchunk = seq // CHUNK
    row_spec = pl.BlockSpec((1, CHUNK, rw), lambda b, c: (b, c, 0))
    gc4 = gc.reshape(bsz, nchunk, 1, rw)
    return pl.pallas_call(
        _rwkv_scan_kernel,
        grid=(bsz, nchunk),
        in_specs=[row_spec] * 7 + [pl.BlockSpec((1, 1, 1, rw), lambda b, c: (b, c, 0, 0))],
        out_specs=row_spec,
        out_shape=jax.ShapeDtypeStruct((bsz, seq, rw), F32),
        scratch_shapes=[pltpu.VMEM((RWKV_HEADS, HEAD_DIM, HEAD_DIM), F32)],
        compiler_params=pltpu.CompilerParams(dimension_semantics=("arbitrary", "arbitrary")),
        name="rwkv_scan",
    )(aq, rq, bk, kd, bg, kg, v, gc4)


def _attn_proj_kernel(x_ref, shift_ref, scale_ref, g_ref, w_ref, cos_ref, sin_ref, *rest):
    outs, buf = rest[:-1], rest[-1]
    tm = x_ref.shape[1]
    aw = ATTN_WIDTH
    h = _norm_modulate(x_ref[0], g_ref[...], shift_ref[0], scale_ref[0])
    p = jnp.dot(_bf(h), w_ref[...], preferred_element_type=F32)
    reps = aw // LANES
    cos = jnp.concatenate([cos_ref[...]] * reps, axis=-1)
    sin = jnp.concatenate([sin_ref[...]] * reps, axis=-1)
    lane = lax.broadcasted_iota(jnp.int32, (tm, aw), 1)
    first_half = _rem(lane, HEAD_DIM) < (HEAD_DIM // 2)

    def rope(t):
        partner = jnp.where(first_half, pltpu.roll(t, aw - HEAD_DIM // 2, axis=1),
                            pltpu.roll(t, HEAD_DIM // 2, axis=1))
        return t * cos + partner * sin

    roped = (rope(p[:, :aw]) * (HEAD_DIM ** -0.5), rope(p[:, aw:2 * aw]), p[:, 2 * aw:])
    tiles_per_group = GROUP_WIDTH // LANES
    n = 0
    for which in range(3):
        for g, (_, dil) in enumerate(ATTN_GROUPS):
            o_ref = outs[n]
            n += 1
            for ct in range(tiles_per_group):
                c0 = g * GROUP_WIDTH + ct * LANES
                val = roped[which][:, c0:c0 + LANES]
                if dil == 1:
                    o_ref[0, 0, :, ct * LANES:(ct + 1) * LANES] = _bf(val)
                    continue
                slot = (which * len(ATTN_GROUPS) + g) * tiles_per_group + ct
                buf[slot] = val
                for r in range(dil):
                    o_ref[0, r, :, ct * LANES:(ct + 1) * LANES] = _bf(
                        buf[slot, pl.ds(r, tm // dil, stride=dil), :])


def _attn_proj(x, mod3, g_pre, w_qkv, cos_t, sin_t, tm):
    bsz, seq, d = x.shape
    out_specs, out_shape = [], []
    for _ in range(3):
        for _, dil in ATTN_GROUPS:
            out_specs.append(pl.BlockSpec((1, dil, tm // dil, GROUP_WIDTH), lambda b, i: (b, 0, i, 0)))
            out_shape.append(jax.ShapeDtypeStruct((bsz, dil, seq // dil, GROUP_WIDTH), BF16))
    return pl.pallas_call(
        _attn_proj_kernel,
        grid=(bsz, seq // tm),
        in_specs=[pl.BlockSpec((1, tm, d), lambda b, i: (b, i, 0)),
                  pl.BlockSpec((1, 1, d), lambda b, i: (b, 0, 0)),
                  pl.BlockSpec((1, 1, d), lambda b, i: (b, 0, 1)),
                  pl.BlockSpec((1, d), lambda b, i: (0, 0)),
                  pl.BlockSpec((d, 3 * ATTN_WIDTH), lambda b, i: (0, 0)),
                  pl.BlockSpec((tm, LANES), lambda b, i: (i, 0)),
                  pl.BlockSpec((tm, LANES), lambda b, i: (i, 0))],
        out_specs=out_specs,
        out_shape=out_shape,
        scratch_shapes=[pltpu.VMEM((3 * ATTN_WIDTH // LANES, tm, LANES), F32)],
        compiler_params=pltpu.CompilerParams(
            dimension_semantics=("arbitrary", "arbitrary"), vmem_limit_bytes=VMEM_LIMIT),
        name="attn_proj",
    )(x, mod3, mod3, g_pre, w_qkv, cos_t, sin_t)


def _attn_kernel(*refs):
    qkv = refs[:9]
    o_ref = refs[9]
    num_ref, m_ref, den_ref = refs[10:]
    blk = ATTN_BLOCK
    qi = lax.broadcasted_iota(jnp.int32, (blk, blk), 0)
    kj = lax.broadcasted_iota(jnp.int32, (blk, blk), 1)
    bias_cur = jnp.where(kj <= qi, 0.0, NEG_BIG)
    bias_prev = jnp.where(kj >= qi, 0.0, NEG_BIG)
    lane = lax.broadcasted_iota(jnp.int32, (blk, LANES), 1)
    head0 = lane < HEAD_DIM

    for g, (_, dil) in enumerate(ATTN_GROUPS):
        q_ref, k_ref, v_ref = qkv[g], qkv[3 + g], qkv[6 + g]
        nb = q_ref.shape[2] // blk

        def body(idx, carry, q_ref=q_ref, k_ref=k_ref, v_ref=v_ref, dil=dil, nb=nb, g=g):
            r = _div(idx, nb)
            n = _rem(idx, nb)
            cur = pl.multiple_of(n * blk, blk)
            prv = pl.multiple_of(jnp.maximum(n - 1, 0) * blk, blk)
            qb = q_ref[0, r, pl.ds(cur, blk), :]
            kc, vc = k_ref[0, r, pl.ds(cur, blk), :], v_ref[0, r, pl.ds(cur, blk), :]
            kp, vp = k_ref[0, r, pl.ds(prv, blk), :], v_ref[0, r, pl.ds(prv, blk), :]
            bp = jnp.where(n > 0, bias_prev, NEG_BIG)
            ms, ls, pvs = [], [], []
            for e in range(2):
                qe = jnp.where(head0 if e == 0 else jnp.logical_not(head0), qb, jnp.zeros_like(qb))
                s_c = lax.dot_general(qe, kc, (((1,), (1,)), ((), ())), preferred_element_type=F32) + bias_cur
                s_p = lax.dot_general(qe, kp, (((1,), (1,)), ((), ())), preferred_element_type=F32) + bp
                m = jnp.maximum(jnp.max(s_c, axis=-1, keepdims=True), jnp.max(s_p, axis=-1, keepdims=True))
                p_c = jnp.exp(s_c - m)
                p_p = jnp.exp(s_p - m)
                ls.append(jnp.sum(p_c, axis=-1, keepdims=True) + jnp.sum(p_p, axis=-1, keepdims=True))
                ms.append(m)
                pvs.append(jnp.dot(_bf(p_c), vc, preferred_element_type=F32)
                           + jnp.dot(_bf(p_p), vp, preferred_element_type=F32))
            m_b = jnp.where(head0, ms[0], ms[1])
            l_b = jnp.where(head0, ls[0], ls[1])
            num_b = jnp.where(head0, pvs[0], pvs[1])
            if g == 0:
                rows = pl.ds(cur, blk)
                num_ref[rows, :] = num_b
                m_ref[rows, :] = m_b
                den_ref[rows, :] = l_b
            else:
                rows = pl.ds(n * (blk * dil) + r, blk, stride=dil)
                m_old = m_ref[rows, :]
                m_new = jnp.maximum(m_old, m_b)
                w_old = jnp.exp(m_old - m_new)
                w_b = jnp.exp(m_b - m_new)
                num_ref[rows, :] = num_ref[rows, :] * w_old + num_b * w_b
                den_ref[rows, :] = den_ref[rows, :] * w_old + l_b * w_b
                m_ref[rows, :] = m_new
            return carry

        lax.fori_loop(0, dil * nb, body, 0)

    o_ref[0] = _bf(num_ref[...] / den_ref[...])


def _attn(qkv, seq):
    bsz = qkv[0].shape[0]
    in_specs = []
    for _ in range(3):
        for _, dil in ATTN_GROUPS:
            in_specs.append(pl.BlockSpec((1, dil, seq // dil, LANES), lambda b, sp: (b, 0, 0, sp)))
    return pl.pallas_call(
        _attn_kernel,
        grid=(bsz, GROUP_WIDTH // LANES),
        in_specs=in_specs,
        out_specs=pl.BlockSpec((1, seq, LANES), lambda b, sp: (b, 0, sp)),
        out_shape=jax.ShapeDtypeStruct((bsz, seq, GROUP_WIDTH), BF16),
        scratch_shapes=[pltpu.VMEM((seq, LANES), F32)] * 3,
        compiler_params=pltpu.CompilerParams(
            dimension_semantics=("arbitrary", "arbitrary"), vmem_limit_bytes=VMEM_LIMIT),
        name="attn",
    )(*qkv)


def _merge_kernel(x_ref, shift_ref, scale_ref, gatem_ref, g_ref, wg_ref, y_ref, bonus_ref, gate_ref,
                  lnw_ref, lnb_ref, wa_ref, o_ref, wb_ref, wout_ref, gpost_ref, out_ref):
    x = x_ref[0]
    h = _norm_modulate(x, g_ref[...], shift_ref[0], scale_ref[0])
    pg = jnp.dot(_bf(h), wg_ref[...], preferred_element_type=F32)
    y = y_ref[0]
    inv_n = 1.0 / HEAD_DIM
    yc = y - _head_sum(y) * inv_n
    var = _head_sum(yc * yc) * inv_n
    yn = yc * lax.rsqrt(var + GN_EPS) * lnw_ref[...] + lnb_ref[...]
    ya = jnp.dot(_bf((yn + bonus_ref[0]) * gate_ref[0]), wa_ref[...], preferred_element_type=F32)
    yb = jnp.dot(o_ref[0], wb_ref[...], preferred_element_type=F32)
    merged = _sigmoid(pg[:, :D_MODEL]) * ya + _sigmoid(pg[:, D_MODEL:]) * yb
    zz = jnp.dot(_bf(merged), wout_ref[...], preferred_element_type=F32)
    out_ref[0] = x + gatem_ref[0] * _rms(zz, gpost_ref[...])


def _merge(x, mod3, g_pre, w_g, y, bonus, gate, lnx_w, lnx_b, w_a, o_att, w_b, w_out, g_post, tm):
    bsz, seq, d = x.shape
    rw = RWKV_WIDTH
    const = lambda shape: pl.BlockSpec(shape, lambda b, i: (0, 0))
    row = lambda n: pl.BlockSpec((1, tm, n), lambda b, i: (b, i, 0))
    modv = lambda j: pl.BlockSpec((1, 1, d), lambda b, i: (b, 0, j))
    return pl.pallas_call(
        _merge_kernel,
        grid=(bsz, seq // tm),
        in_specs=[row(d), modv(0), modv(1), modv(2), const((1, d)), const((d, 2 * d)),
                  row(rw), row(rw), row(rw), const((1, rw)), const((1, rw)), const((rw, d)),
                  row(GROUP_WIDTH), const((GROUP_WIDTH, d)), const((d, d)), const((1, d))],
        out_specs=row(d),
        out_shape=jax.ShapeDtypeStruct((bsz, seq, d), F32),
        compiler_params=pltpu.CompilerParams(
            dimension_semantics=("arbitrary", "arbitrary"), vmem_limit_bytes=VMEM_LIMIT),
        name="merge",
    )(x, mod3, mod3, mod3, g_pre, w_g, y, bonus, gate, lnx_w, lnx_b, w_a, o_att, w_b, w_out, g_post)


def _ffn_kernel(x_ref, shift_ref, scale_ref, gatef_ref, g_ref, win_ref, wout_ref, gpost_ref, out_ref):
    x = x_ref[0]
    dff = wout_ref.shape[0]
    h = _bf(_norm_modulate(x, g_ref[...], shift_ref[0], scale_ref[0]))
    u = jnp.dot(h, win_ref[:, :dff], preferred_element_type=F32)
    gt = jnp.dot(h, win_ref[:, dff:], preferred_element_type=F32)
    act = _bf(u * _sigmoid(u) * gt)
    f = jnp.dot(act, wout_ref[...], preferred_element_type=F32)
    out_ref[0] = x + gatef_ref[0] * _rms(f, gpost_ref[...])


def _ffn(x, mod3, g_pre, w_in, w_out, g_post, tm):
    bsz, seq, d = x.shape
    dff = w_out.shape[0]
    const = lambda shape: pl.BlockSpec(shape, lambda b, i: (0, 0))
    row = pl.BlockSpec((1, tm, d), lambda b, i: (b, i, 0))
    modv = lambda j: pl.BlockSpec((1, 1, d), lambda b, i: (b, 0, j))
    return pl.pallas_call(
        _ffn_kernel,
        grid=(bsz, seq // tm),
        in_specs=[row, modv(3), modv(4), modv(5), const((1, d)), const((d, 2 * dff)),
                  const((dff, d)), const((1, d))],
        out_specs=row,
        out_shape=jax.ShapeDtypeStruct((bsz, seq, d), F32),
        compiler_params=pltpu.CompilerParams(
            dimension_semantics=("arbitrary", "arbitrary"), vmem_limit_bytes=VMEM_LIMIT),
        name="ffn",
    )(x, mod3, mod3, mod3, g_pre, w_in, w_out, g_post)


def _rope_tables(seq):
    half = HEAD_DIM // 2
    inv_freq = ROPE_THETA ** (-jnp.arange(half, dtype=F32) / half)
    ang = jnp.arange(seq, dtype=F32)[:, None] * inv_freq[None, :]
    cos, sin = jnp.cos(ang), jnp.sin(ang)
    reps = LANES // HEAD_DIM
    cos_t = jnp.tile(jnp.concatenate([cos, cos], axis=-1), (1, reps))
    sin_t = jnp.tile(jnp.concatenate([-sin, sin], axis=-1), (1, reps))
    return cos_t, sin_t


def _layer(x, c, w_ada, b_ada, g_pre_mix, g_post_mix, w_in, mu_shift, w0, w2, a0, a2, g2,
           k_k, k_a, r_k, lnx_w, lnx_b, w_a, w_b, w_out, g_pre_ffn, g_post_ffn, w_ffn_in, w_ffn_out):
    bsz, seq, d = x.shape
    rw = RWKV_WIDTH
    row1 = lambda t: t.reshape(1, -1)
    mod3 = _mod(c, w_ada, b_ada).reshape(bsz, 1, 6 * d)

    w_rwkv = _bf(w_in[:, :SHIFT_WIDTH])
    w_qkv = _bf(w_in[:, SHIFT_WIDTH:SHIFT_WIDTH + 3 * ATTN_WIDTH])
    w_gates = _bf(w_in[:, SHIFT_WIDTH + 3 * ATTN_WIDTH:])
    w_lora = jnp.zeros((LORA_WIDTH, 3 * rw), F32)
    w_lora = w_lora.at[:DECAY_LORA, :rw].set(w2)
    w_lora = w_lora.at[DECAY_LORA:DECAY_LORA + ICLR_LORA, rw:2 * rw].set(a2)
    w_lora = _bf(w_lora.at[DECAY_LORA + ICLR_LORA:, 2 * rw:].set(g2))

    prep = _rwkv_prep(x, mod3, row1(g_pre_mix), w_rwkv, row1(mu_shift), w_lora, row1(w0), row1(a0),
                      row1(k_k), row1(k_a), row1(r_k), tm=256)
    aq, rq, bk, kd, bg, kg, v, gc, bonus, gate = prep
    y = _rwkv_scan(aq, rq, bk, kd, bg, kg, v, gc.reshape(bsz, seq // CHUNK, rw))

    cos_t, sin_t = _rope_tables(seq)
    qkv = _attn_proj(x, mod3, row1(g_pre_mix), w_qkv, cos_t, sin_t, tm=256)
    o_att = _attn(qkv, seq)

    x1 = _merge(x, mod3, row1(g_pre_mix), w_gates, y, bonus, gate, row1(lnx_w), row1(lnx_b),
                _bf(w_a), o_att, _bf(w_b), _bf(w_out), row1(g_post_mix), tm=256)
    return _ffn(x1, mod3, row1(g_pre_ffn), _bf(w_ffn_in), _bf(w_ffn_out), row1(g_post_ffn), tm=256)


def kernel(x, c, w_ada, b_ada, g_pre_mix, g_post_mix, w_in, mu_shift, w0, w2, a0, a2, g2, k_k, k_a, r_k,
           lnx_w, lnx_b, w_a, w_b, w_out, g_pre_ffn, g_post_ffn, w_ffn_in, w_ffn_out):
    for i in range(w_in.shape[0]):
        x = _layer(x, c, w_ada[i], b_ada[i], g_pre_mix[i], g_post_mix[i], w_in[i], mu_shift[i], w0[i],
                   w2[i], a0[i], a2[i], g2[i], k_k[i], k_a[i], r_k[i], lnx_w[i], lnx_b[i], w_a[i], w_b[i],
                   w_out[i], g_pre_ffn[i], g_post_ffn[i], w_ffn_in[i], w_ffn_out[i])
    return x
```

```python
import functools
import math

import jax
import jax.numpy as jnp
from jax import lax
from jax.experimental import pallas as pl
from jax.experimental.pallas import tpu as pltpu

D_MODEL = 1024
HEAD_DIM = 64
RWKV_HEADS = 12
RWKV_WIDTH = RWKV_HEADS * HEAD_DIM
DECAY_LORA = 64
ICLR_LORA = 64
GATE_LORA = 128
LORA_WIDTH = DECAY_LORA + ICLR_LORA + GATE_LORA
ATTN_GROUPS = ((128, 1), (512, 4), (2048, 16))
HEADS_PER_GROUP = 4
GROUP_WIDTH = HEADS_PER_GROUP * HEAD_DIM
ATTN_WIDTH = len(ATTN_GROUPS) * GROUP_WIDTH
ATTN_BLOCK = 128
ROPE_THETA = 10000.0
SHIFT_WIDTH = 3 * RWKV_WIDTH + LORA_WIDTH
NORM_EPS = 1e-6
GN_EPS = 64e-5

CHUNK = 64
INV_BASE = 8
LANES = 128
NEG_BIG = -1e30
VMEM_LIMIT = 56 * 1024 * 1024

F32 = jnp.float32
BF16 = jnp.bfloat16


def _bf(x):
    return x.astype(BF16)


def _div(x, n):
    assert n & (n - 1) == 0
    return jnp.right_shift(x, n.bit_length() - 1)


def _rem(x, n):
    assert n & (n - 1) == 0
    return jnp.bitwise_and(x, n - 1)


def _sigmoid(x):
    return 1.0 / (1.0 + jnp.exp(-x))


def _mm(a, b):
    return jnp.dot(_bf(a), _bf(b), preferred_element_type=F32)


def _mm_nt(a, b):
    return lax.dot_general(_bf(a), _bf(b), (((1,), (1,)), ((), ())), preferred_element_type=F32)


def _mm_tn(a, b):
    return lax.dot_general(_bf(a), _bf(b), (((0,), (0,)), ((), ())), preferred_element_type=F32)


def _split_hi_lo(x):
    hi = _bf(x)
    lo = _bf(x - hi.astype(F32))
    return hi, lo


def _mm_exact_lhs(m_bf, x):
    hi, lo = _split_hi_lo(x)
    return (jnp.dot(m_bf, hi, preferred_element_type=F32)
            + jnp.dot(m_bf, lo, preferred_element_type=F32))


def _head_sum(x):
    tile = 2 * LANES
    ri = _div(lax.broadcasted_iota(jnp.int32, (tile, tile), 0), HEAD_DIM)
    ci = _div(lax.broadcasted_iota(jnp.int32, (tile, tile), 1), HEAD_DIM)
    ones_bd = jnp.where(ri == ci, 1.0, 0.0).astype(BF16)
    hi, lo = _split_hi_lo(x)
    parts = []
    for j in range(x.shape[-1] // tile):
        sl = slice(j * tile, (j + 1) * tile)
        parts.append(jnp.dot(hi[:, sl], ones_bd, preferred_element_type=F32)
                     + jnp.dot(lo[:, sl], ones_bd, preferred_element_type=F32))
    return jnp.concatenate(parts, axis=-1)


def _norm_modulate(x, g, shift, scale):
    y = x * lax.rsqrt(jnp.mean(x * x, axis=-1, keepdims=True) + NORM_EPS)
    return (y * g) * (1.0 + scale) + shift


def _rms(x, g):
    return x * lax.rsqrt(jnp.mean(x * x, axis=-1, keepdims=True) + NORM_EPS) * g


def _mod_kernel(c_ref, w_ref, b_ref, o_ref):
    c = c_ref[...]
    o_ref[...] = jnp.dot(c * _sigmoid(c), w_ref[...], preferred_element_type=F32,
                         precision=lax.Precision.HIGHEST) + b_ref[...]


def _mod(c, w_ada, b_ada):
    bsz, d = c.shape
    n = w_ada.shape[1]
    tn = d
    return pl.pallas_call(
        _mod_kernel,
        grid=(n // tn,),
        in_specs=[pl.BlockSpec((bsz, d), lambda j: (0, 0)),
                  pl.BlockSpec((d, tn), lambda j: (0, j)),
                  pl.BlockSpec((1, tn), lambda j: (0, j))],
        out_specs=pl.BlockSpec((bsz, tn), lambda j: (0, j)),
        out_shape=jax.ShapeDtypeStruct((bsz, n), F32),
        name="mod",
    )(c, w_ada, b_ada.reshape(1, n))


def _rwkv_prep_kernel(x_ref, shift_ref, scale_ref, g_ref, w_ref, mu_ref, wl_ref, w0_ref, a0_ref,
                      kk_ref, ka_ref, rk_ref,
                      aq_o, rq_o, bk_o, kd_o, bg_o, kg_o, v_o, gc_o, bonus_o, gate_o,
                      carry_ref):
    i = pl.program_id(1)
    tm = x_ref.shape[1]
    rw = RWKV_WIDTH

    @pl.when(i == 0)
    def _():
        carry_ref[...] = jnp.zeros_like(carry_ref)

    h = _norm_modulate(x_ref[0], g_ref[...], shift_ref[0], scale_ref[0])
    p = jnp.dot(_bf(h), w_ref[...], preferred_element_type=F32)

    prev_row = carry_ref[7:8, :]
    carry_ref[...] = p[tm - 8:, :]
    row = lax.broadcasted_iota(jnp.int32, p.shape, 0)
    shifted = jnp.where(row == 0, prev_row, pltpu.roll(p, 1, axis=0))
    ps = p + mu_ref[...] * (shifted - p)

    pr, pk, pv = ps[:, :rw], ps[:, rw:2 * rw], ps[:, 2 * rw:3 * rw]
    lora_in = ps[:, 3 * rw:]
    lane = lax.broadcasted_iota(jnp.int32, lora_in.shape, 1)
    z = jnp.where(lane < DECAY_LORA, jnp.tanh(lora_in),
                  jnp.where(lane < DECAY_LORA + ICLR_LORA, lora_in, _sigmoid(lora_in)))
    lo = jnp.dot(_bf(z), wl_ref[...], preferred_element_type=F32)
    wf = w0_ref[...] + lo[:, :rw]
    a = _sigmoid(a0_ref[...] + lo[:, rw:2 * rw])
    gate_o[0] = lo[:, 2 * rw:]

    lw = -math.exp(-0.5) * _sigmoid(wf)

    kkr = pk * kk_ref[...]
    kkn = kkr * (1.0 / jnp.maximum(jnp.sqrt(_head_sum(kkr * kkr)), 1e-12))
    k = pk * (1.0 + (a - 1.0) * ka_ref[...])
    b = kkn * a
    bonus_o[0] = _head_sum(pr * k * rk_ref[...]) * pv
    v_o[0] = pv

    ri = lax.broadcasted_iota(jnp.int32, (tm, tm), 0)
    ci = lax.broadcasted_iota(jnp.int32, (tm, tm), 1)
    same = _div(ri, CHUNK) == _div(ci, CHUNK)
    lower = jnp.where(same & (ci <= ri), 1.0, 0.0).astype(BF16)
    upper = jnp.where(same & (ci > ri), 1.0, 0.0).astype(BF16)
    cum = _mm_exact_lhs(lower, lw)
    rest = _mm_exact_lhs(upper, lw)
    nck = tm // CHUNK
    si = lax.broadcasted_iota(jnp.int32, (nck, tm), 0)
    sj = _div(lax.broadcasted_iota(jnp.int32, (nck, tm), 1), CHUNK)
    total = _mm_exact_lhs(jnp.where(si == sj, 1.0, 0.0).astype(BF16), lw)
    gc_o[0, 0] = jnp.exp(total)

    e_cum = jnp.exp(cum)
    e_inv = jnp.exp(-cum)
    e_rest = jnp.exp(rest)
    aq_o[0] = -kkn * jnp.exp(cum - lw)
    rq_o[0] = pr * e_cum
    bk_o[0] = b * e_inv
    kd_o[0] = k * e_inv
    bg_o[0] = b * e_rest
    kg_o[0] = k * e_rest


def _rwkv_prep(x, mod3, g_pre, w_rwkv, mu, w_lora, w0, a0, k_k, k_a, r_k, tm):
    bsz, seq, d = x.shape
    rw = RWKV_WIDTH
    nck = tm // CHUNK
    row_spec = pl.BlockSpec((1, tm, rw), lambda b, i: (b, i, 0))
    vec = lambda n: pl.BlockSpec((1, n), lambda b, i: (0, 0))
    big = jax.ShapeDtypeStruct((bsz, seq, rw), F32)
    return pl.pallas_call(
        _rwkv_prep_kernel,
        grid=(bsz, seq // tm),
        in_specs=[pl.BlockSpec((1, tm, d), lambda b, i: (b, i, 0)),
                  pl.BlockSpec((1, 1, d), lambda b, i: (b, 0, 0)),
                  pl.BlockSpec((1, 1, d), lambda b, i: (b, 0, 1)),
                  vec(d),
                  pl.BlockSpec((d, SHIFT_WIDTH), lambda b, i: (0, 0)),
                  vec(SHIFT_WIDTH),
                  pl.BlockSpec((LORA_WIDTH, 3 * rw), lambda b, i: (0, 0)),
                  vec(rw), vec(rw), vec(rw), vec(rw), vec(rw)],
        out_specs=[row_spec] * 7
                  + [pl.BlockSpec((1, 1, nck, rw), lambda b, i: (b, i, 0, 0)), row_spec, row_spec],
        out_shape=[big] * 7
                  + [jax.ShapeDtypeStruct((bsz, seq // tm, nck, rw), F32), big, big],
        scratch_shapes=[pltpu.VMEM((8, SHIFT_WIDTH), F32)],
        compiler_params=pltpu.CompilerParams(
            dimension_semantics=("arbitrary", "arbitrary"), vmem_limit_bytes=VMEM_LIMIT),
        name="rwkv_prep",
    )(x, mod3, mod3, g_pre, w_rwkv, mu, w_lora, w0, a0, k_k, k_a, r_k)


def _unit_lower_inverse(a_list, masks):
    eye, diag_mask, merge_masks = masks
    ps = [jnp.where(diag_mask, a, 0.0) for a in a_list]
    ts = [eye + p for p in ps]
    m = 1
    while 2 * m < INV_BASE:
        ps = [_mm(p, p) for p in ps]
        ts = [t + _mm(t, p) for t, p in zip(ts, ps)]
        m *= 2
    for mk in merge_masks:
        ws = [_mm(jnp.where(mk, a, 0.0), t) for a, t in zip(a_list, ts)]
        ts = [t + _mm(t, w) for t, w in zip(ts, ws)]
    return ts


def _rwkv_scan_kernel(aq_ref, rq_ref, bk_ref, kd_ref, bg_ref, kg_ref, v_ref, gc_ref, y_ref, st_ref):
    c = pl.program_id(1)

    @pl.when(c == 0)
    def _():
        st_ref[...] = jnp.zeros_like(st_ref)

    ri = lax.broadcasted_iota(jnp.int32, (CHUNK, CHUNK), 0)
    ci = lax.broadcasted_iota(jnp.int32, (CHUNK, CHUNK), 1)
    strict = ri > ci
    incl = ri >= ci
    ri2 = lax.broadcasted_iota(jnp.int32, (2 * CHUNK, CHUNK), 0)
    ci2 = lax.broadcasted_iota(jnp.int32, (2 * CHUNK, CHUNK), 1)
    rloc = _rem(ri2, CHUNK)
    stacked = (rloc > ci2) | ((rloc == ci2) & (ri2 >= CHUNK))
    eye = jnp.where(ri == ci, 1.0, 0.0)
    diag_mask = _div(ri, INV_BASE) == _div(ci, INV_BASE)
    merge_masks = []
    bsize = INV_BASE
    while bsize < CHUNK:
        merge_masks.append((_div(ri, 2 * bsize) == _div(ci, 2 * bsize))
                           & (_div(ri, bsize) != _div(ci, bsize)))
        bsize *= 2
    masks = (eye, diag_mask, merge_masks)

    heads = range(RWKV_HEADS)
    per_pair = LANES // HEAD_DIM

    def per_head(ref, *lead):
        out = []
        for hd in heads:
            j, e = divmod(hd, per_pair)
            pair = ref[(0,) + lead + (slice(None), slice(j * LANES, (j + 1) * LANES))]
            out.append(pair[:, e * HEAD_DIM:(e + 1) * HEAD_DIM])
        return out

    aq, rq, bk, kd = per_head(aq_ref), per_head(rq_ref), per_head(bk_ref), per_head(kd_ref)
    bg, kg, v, gc = per_head(bg_ref), per_head(kg_ref), per_head(v_ref), per_head(gc_ref, 0)
    s0 = [st_ref[h] for h in heads]

    qs = [jnp.concatenate([aq[h], rq[h]], axis=0) for h in heads]
    ab = [_mm_nt(qs[h], bk[h]) for h in heads]
    ak = [jnp.where(stacked, _mm_nt(qs[h], kd[h]), 0.0) for h in heads]
    xs = [_mm_nt(qs[h], s0[h]) for h in heads]
    akv = [_mm(ak[h], v[h]) for h in heads]
    t = _unit_lower_inverse([jnp.where(strict, ab[h][:CHUNK], 0.0) for h in heads], masks)
    u = [_mm(t[h], xs[h][:CHUNK] + akv[h][:CHUNK]) for h in heads]
    y = [xs[h][CHUNK:] + akv[h][CHUNK:] + _mm(jnp.where(incl, ab[h][CHUNK:], 0.0), u[h])
         for h in heads]
    for h in heads:
        uv = jnp.concatenate([u[h], v[h]], axis=0)
        bkg = jnp.concatenate([bg[h], kg[h]], axis=0)
        st_ref[h] = s0[h] * gc[h] + _mm_tn(uv, bkg)
    for j in range(RWKV_WIDTH // LANES):
        y_ref[0, :, j * LANES:(j + 1) * LANES] = jnp.concatenate(
            y[j * per_pair:(j + 1) * per_pair], axis=-1)


def _rwkv_scan(aq, rq, bk, kd, bg, kg, v, gc):
    bsz, seq, rw = aq.shape
    nchunk = seq // CHUNK
    row_spec = pl.BlockSpec((1, CHUNK, rw), lambda b, c: (b, c, 0))
    gc4 = gc.reshape(bsz, nchunk, 1, rw)
    return pl.pallas_call(
        _rwkv_scan_kernel,
        grid=(bsz, nchunk),
        in_specs=[row_spec] * 7 + [pl.BlockSpec((1, 1, 1, rw), lambda b, c: (b, c, 0, 0))],
        out_specs=row_spec,
        out_shape=jax.ShapeDtypeStruct((bsz, seq, rw), F32),
        scratch_shapes=[pltpu.VMEM((RWKV_HEADS, HEAD_DIM, HEAD_DIM), F32)],
        compiler_params=pltpu.CompilerParams(dimension_semantics=("arbitrary", "arbitrary")),
        name="rwkv_scan",
    )(aq, rq, bk, kd, bg, kg, v, gc4)


def _attn_proj_kernel(x_ref, shift_ref, scale_ref, g_ref, w_ref, cos_ref, sin_ref, *rest):
    outs, buf = rest[:-1], rest[-1]
    tm = x_ref.shape[1]
    aw = ATTN_WIDTH
    h = _norm_modulate(x_ref[0], g_ref[...], shift_ref[0], scale_ref[0])
    p = jnp.dot(_bf(h), w_ref[...], preferred_element_type=F32)
    reps = aw // LANES
    cos = jnp.concatenate([cos_ref[...]] * reps, axis=-1)
    sin = jnp.concatenate([sin_ref[...]] * reps, axis=-1)
    lane = lax.broadcasted_iota(jnp.int32, (tm, aw), 1)
    first_half = _rem(lane, HEAD_DIM) < (HEAD_DIM // 2)

    def rope(t):
        partner = jnp.where(first_half, pltpu.roll(t, aw - HEAD_DIM // 2, axis=1),
                            pltpu.roll(t, HEAD_DIM // 2, axis=1))
        return t * cos + partner * sin

    roped = (rope(p[:, :aw]) * (HEAD_DIM ** -0.5), rope(p[:, aw:2 * aw]), p[:, 2 * aw:])
    tiles_per_group = GROUP_WIDTH // LANES
    n = 0
    for which in range(3):
        for g, (_, dil) in enumerate(ATTN_GROUPS):
            o_ref = outs[n]
            n += 1
            for ct in range(tiles_per_group):
                c0 = g * GROUP_WIDTH + ct * LANES
                val = roped[which][:, c0:c0 + LANES]
                if dil == 1:
                    o_ref[0, 0, :, ct * LANES:(ct + 1) * LANES] = _bf(val)
                    continue
                slot = (which * len(ATTN_GROUPS) + g) * tiles_per_group + ct
                buf[slot] = val
                for r in range(dil):
                    o_ref[0, r, :, ct * LANES:(ct + 1) * LANES] = _bf(
                        buf[slot, pl.ds(r, tm // dil, stride=dil), :])


def _attn_proj(x, mod3, g_pre, w_qkv, cos_t, sin_t, tm):
    bsz, seq, d = x.shape
    out_specs, out_shape = [], []
    for _ in range(3):
        for _, dil in ATTN_GROUPS:
            out_specs.append(pl.BlockSpec((1, dil, tm // dil, GROUP_WIDTH), lambda b, i: (b, 0, i, 0)))
            out_shape.append(jax.ShapeDtypeStruct((bsz, dil, seq // dil, GROUP_WIDTH), BF16))
    return pl.pallas_call(
        _attn_proj_kernel,
        grid=(bsz, seq // tm),
        in_specs=[pl.BlockSpec((1, tm, d), lambda b, i: (b, i, 0)),
                  pl.BlockSpec((1, 1, d), lambda b, i: (b, 0, 0)),
                  pl.BlockSpec((1, 1, d), lambda b, i: (b, 0, 1)),
                  pl.BlockSpec((1, d), lambda b, i: (0, 0)),
                  pl.BlockSpec((d, 3 * ATTN_WIDTH), lambda b, i: (0, 0)),
                  pl.BlockSpec((tm, LANES), lambda b, i: (i, 0)),
                  pl.BlockSpec((tm, LANES), lambda b, i: (i, 0))],
        out_specs=out_specs,
        out_shape=out_shape,
        scratch_shapes=[pltpu.VMEM((3 * ATTN_WIDTH // LANES, tm, LANES), F32)],
        compiler_params=pltpu.CompilerParams(
            dimension_semantics=("arbitrary", "arbitrary"), vmem_limit_bytes=VMEM_LIMIT),
        name="attn_proj",
    )(x, mod3, mod3, g_pre, w_qkv, cos_t, sin_t)


def _attn_kernel(*refs):
    qkv = refs[:9]
    o_ref = refs[9]
    num_ref, m_ref, den_ref = refs[10:]
    blk = ATTN_BLOCK
    qi = lax.broadcasted_iota(jnp.int32, (blk, blk), 0)
    kj = lax.broadcasted_iota(jnp.int32, (blk, blk), 1)
    bias_cur = jnp.where(kj <= qi, 0.0, NEG_BIG)
    bias_prev = jnp.where(kj >= qi, 0.0, NEG_BIG)
    lane = lax.broadcasted_iota(jnp.int32, (blk, LANES), 1)
    head0 = lane < HEAD_DIM

    for g, (_, dil) in enumerate(ATTN_GROUPS):
        q_ref, k_ref, v_ref = qkv[g], qkv[3 + g], qkv[6 + g]
        nb = q_ref.shape[2] // blk

        def body(idx, carry, q_ref=q_ref, k_ref=k_ref, v_ref=v_ref, dil=dil, nb=nb, g=g):
            r = _div(idx, nb)
            n = _rem(idx, nb)
            cur = pl.multiple_of(n * blk, blk)
            prv = pl.multiple_of(jnp.maximum(n - 1, 0) * blk, blk)
            qb = q_ref[0, r, pl.ds(cur, blk), :]
            kc, vc = k_ref[0, r, pl.ds(cur, blk), :], v_ref[0, r, pl.ds(cur, blk), :]
            kp, vp = k_ref[0, r, pl.ds(prv, blk), :], v_ref[0, r, pl.ds(prv, blk), :]
            bp = jnp.where(n > 0, bias_prev, NEG_BIG)
            ms, ls, pvs = [], [], []
            for e in range(2):
                qe = jnp.where(head0 if e == 0 else jnp.logical_not(head0), qb, jnp.zeros_like(qb))
                s_c = lax.dot_general(qe, kc, (((1,), (1,)), ((), ())), preferred_element_type=F32) + bias_cur
                s_p = lax.dot_general(qe, kp, (((1,), (1,)), ((), ())), preferred_element_type=F32) + bp
                m = jnp.maximum(jnp.max(s_c, axis=-1, keepdims=True), jnp.max(s_p, axis=-1, keepdims=True))
                p_c = jnp.exp(s_c - m)
                p_p = jnp.exp(s_p - m)
                ls.append(jnp.sum(p_c, axis=-1, keepdims=True) + jnp.sum(p_p, axis=-1, keepdims=True))
                ms.append(m)
                pvs.append(jnp.dot(_bf(p_c), vc, preferred_element_type=F32)
                           + jnp.dot(_bf(p_p), vp, preferred_element_type=F32))
            m_b = jnp.where(head0, ms[0], ms[1])
            l_b = jnp.where(head0, ls[0], ls[1])
            num_b = jnp.where(head0, pvs[0], pvs[1])
            if g == 0:
                rows = pl.ds(cur, blk)
                num_ref[rows, :] = num_b
                m_ref[rows, :] = m_b
                den_ref[rows, :] = l_b
            else:
                rows = pl.ds(n * (blk * dil) + r, blk, stride=dil)
                m_old = m_ref[rows, :]
                m_new = jnp.maximum(m_old, m_b)
                w_old = jnp.exp(m_old - m_new)
                w_b = jnp.exp(m_b - m_new)
                num_ref[rows, :] = num_ref[rows, :] * w_old + num_b * w_b
                den_ref[rows, :] = den_ref[rows, :] * w_old + l_b * w_b
                m_ref[rows, :] = m_new
            return carry

        lax.fori_loop(0, dil * nb, body, 0)

    o_ref[0] = _bf(num_ref[...] / den_ref[...])


def _attn(qkv, seq):
    bsz = qkv[0].shape[0]
    in_specs = []
    for _ in range(3):
        for _, dil in ATTN_GROUPS:
            in_specs.append(pl.BlockSpec((1, dil, seq // dil, LANES), lambda b, sp: (b, 0, 0, sp)))
    return pl.pallas_call(
        _attn_kernel,
        grid=(bsz, GROUP_WIDTH // LANES),
        in_specs=in_specs,
        out_specs=pl.BlockSpec((1, seq, LANES), lambda b, sp: (b, 0, sp)),
        out_shape=jax.ShapeDtypeStruct((bsz, seq, GROUP_WIDTH), BF16),
        scratch_shapes=[pltpu.VMEM((seq, LANES), F32)] * 3,
        compiler_params=pltpu.CompilerParams(
            dimension_semantics=("arbitrary", "arbitrary"), vmem_limit_bytes=VMEM_LIMIT),
        name="attn",
    )(*qkv)


def _merge_kernel(x_ref, shift_ref, scale_ref, gatem_ref, g_ref, wg_ref, y_ref, bonus_ref, gate_ref,
                  lnw_ref, lnb_ref, wa_ref, o_ref, wb_ref, wout_ref, gpost_ref, out_ref):
    x = x_ref[0]
    h = _norm_modulate(x, g_ref[...], shift_ref[0], scale_ref[0])
    pg = jnp.dot(_bf(h), wg_ref[...], preferred_element_type=F32)
    y = y_ref[0]
    inv_n = 1.0 / HEAD_DIM
    yc = y - _head_sum(y) * inv_n
    var = _head_sum(yc * yc) * inv_n
    yn = yc * lax.rsqrt(var + GN_EPS) * lnw_ref[...] + lnb_ref[...]
    ya = jnp.dot(_bf((yn + bonus_ref[0]) * gate_ref[0]), wa_ref[...], preferred_element_type=F32)
    yb = jnp.dot(o_ref[0], wb_ref[...], preferred_element_type=F32)
    merged = _sigmoid(pg[:, :D_MODEL]) * ya + _sigmoid(pg[:, D_MODEL:]) * yb
    zz = jnp.dot(_bf(merged), wout_ref[...], preferred_element_type=F32)
    out_ref[0] = x + gatem_ref[0] * _rms(zz, gpost_ref[...])


def _merge(x, mod3, g_pre, w_g, y, bonus, gate, lnx_w, lnx_b, w_a, o_att, w_b, w_out, g_post, tm):
    bsz, seq, d = x.shape
    rw = RWKV_WIDTH
    const = lambda shape: pl.BlockSpec(shape, lambda b, i: (0, 0))
    row = lambda n: pl.BlockSpec((1, tm, n), lambda b, i: (b, i, 0))
    modv = lambda j: pl.BlockSpec((1, 1, d), lambda b, i: (b, 0, j))
    return pl.pallas_call(
        _merge_kernel,
        grid=(bsz, seq // tm),
        in_specs=[row(d), modv(0), modv(1), modv(2), const((1, d)), const((d, 2 * d)),
                  row(rw), row(rw), row(rw), const((1, rw)), const((1, rw)), const((rw, d)),
                  row(GROUP_WIDTH), const((GROUP_WIDTH, d)), const((d, d)), const((1, d))],
        out_specs=row(d),
        out_shape=jax.ShapeDtypeStruct((bsz, seq, d), F32),
        compiler_params=pltpu.CompilerParams(
            dimension_semantics=("arbitrary", "arbitrary"), vmem_limit_bytes=VMEM_LIMIT),
        name="merge",
    )(x, mod3, mod3, mod3, g_pre, w_g, y, bonus, gate, lnx_w, lnx_b, w_a, o_att, w_b, w_out, g_post)


def _ffn_kernel(x_ref, shift_ref, scale_ref, gatef_ref, g_ref, win_ref, wout_ref, gpost_ref, out_ref):
    x = x_ref[0]
    dff = wout_ref.shape[0]
    h = _bf(_norm_modulate(x, g_ref[...], shift_ref[0], scale_ref[0]))
    u = jnp.dot(h, win_ref[:, :dff], preferred_element_type=F32)
    gt = jnp.dot(h, win_ref[:, dff:], preferred_element_type=F32)
    act = _bf(u * _sigmoid(u) * gt)
    f = jnp.dot(act, wout_ref[...], preferred_element_type=F32)
    out_ref[0] = x + gatef_ref[0] * _rms(f, gpost_ref[...])


def _ffn(x, mod3, g_pre, w_in, w_out, g_post, tm):
    bsz, seq, d = x.shape
    dff = w_out.shape[0]
    const = lambda shape: pl.BlockSpec(shape, lambda b, i: (0, 0))
    row = pl.BlockSpec((1, tm, d), lambda b, i: (b, i, 0))
    modv = lambda j: pl.BlockSpec((1, 1, d), lambda b, i: (b, 0, j))
    return pl.pallas_call(
        _ffn_kernel,
        grid=(bsz, seq // tm),
        in_specs=[row, modv(3), modv(4), modv(5), const((1, d)), const((d, 2 * dff)),
                  const((dff, d)), const((1, d))],
        out_specs=row,
        out_shape=jax.ShapeDtypeStruct((bsz, seq, d), F32),
        compiler_params=pltpu.CompilerParams(
            dimension_semantics=("arbitrary", "arbitrary"), vmem_limit_bytes=VMEM_LIMIT),
        name="ffn",
    )(x, mod3, mod3, mod3, g_pre, w_in, w_out, g_post)


def _rope_tables(seq):
    half = HEAD_DIM // 2
    inv_freq = ROPE_THETA ** (-jnp.arange(half, dtype=F32) / half)
    ang = jnp.arange(seq, dtype=F32)[:, None] * inv_freq[None, :]
    cos, sin = jnp.cos(ang), jnp.sin(ang)
    reps = LANES // HEAD_DIM
    cos_t = jnp.tile(jnp.concatenate([cos, cos], axis=-1), (1, reps))
    sin_t = jnp.tile(jnp.concatenate([-sin, sin], axis=-1), (1, reps))
    return cos_t, sin_t


def _layer(x, c, w_ada, b_ada, g_pre_mix, g_post_mix, w_in, mu_shift, w0, w2, a0, a2, g2,
           k_k, k_a, r_k, lnx_w, lnx_b, w_a, w_b, w_out, g_pre_ffn, g_post_ffn, w_ffn_in, w_ffn_out):
    bsz, seq, d = x.shape
    rw = RWKV_WIDTH
    row1 = lambda t: t.reshape(1, -1)
    mod3 = _mod(c, w_ada, b_ada).reshape(bsz, 1, 6 * d)

    w_rwkv = _bf(w_in[:, :SHIFT_WIDTH])
    w_qkv = _bf(w_in[:, SHIFT_WIDTH:SHIFT_WIDTH + 3 * ATTN_WIDTH])
    w_gates = _bf(w_in[:, SHIFT_WIDTH + 3 * ATTN_WIDTH:])
    w_lora = jnp.zeros((LORA_WIDTH, 3 * rw), F32)
    w_lora = w_lora.at[:DECAY_LORA, :rw].set(w2)
    w_lora = w_lora.at[DECAY_LORA:DECAY_LORA + ICLR_LORA, rw:2 * rw].set(a2)
    w_lora = _bf(w_lora.at[DECAY_LORA + ICLR_LORA:, 2 * rw:].set(g2))

    prep = _rwkv_prep(x, mod3, row1(g_pre_mix), w_rwkv, row1(mu_shift), w_lora, row1(w0), row1(a0),
                      row1(k_k), row1(k_a), row1(r_k), tm=256)
    aq, rq, bk, kd, bg, kg, v, gc, bonus, gate = prep
    y = _rwkv_scan(aq, rq, bk, kd, bg, kg, v, gc.reshape(bsz, seq // CHUNK, rw))

    cos_t, sin_t = _rope_tables(seq)
    qkv = _attn_proj(x, mod3, row1(g_pre_mix), w_qkv, cos_t, sin_t, tm=256)
    o_att = _attn(qkv, seq)

    x1 = _merge(x, mod3, row1(g_pre_mix), w_gates, y, bonus, gate, row1(lnx_w), row1(lnx_b),
                _bf(w_a), o_att, _bf(w_b), _bf(w_out), row1(g_post_mix), tm=256)
    return _ffn(x1, mod3, row1(g_pre_ffn), _bf(w_ffn_in), _bf(w_ffn_out), row1(g_post_ffn), tm=256)


def kernel(x, c, w_ada, b_ada, g_pre_mix, g_post_mix, w_in, mu_shift, w0, w2, a0, a2, g2, k_k, k_a, r_k,
           lnx_w, lnx_b, w_a, w_b, w_out, g_pre_ffn, g_post_ffn, w_ffn_in, w_ffn_out):
    for i in range(w_in.shape[0]):
        x = _layer(x, c, w_ada[i], b_ada[i], g_pre_mix[i], g_post_mix[i], w_in[i], mu_shift[i], w0[i],
                   w2[i], a0[i], a2[i], g2[i], k_k[i], k_a[i], r_k[i], lnx_w[i], lnx_b[i], w_a[i], w_b[i],
                   w_out[i], g_pre_ffn[i], g_post_ffn[i], w_ffn_in[i], w_ffn_out[i])
    return x
```

```python
import functools
import math

import jax
import jax.numpy as jnp
from jax import lax
from jax.experimental import pallas as pl
from jax.experimental.pallas import tpu as pltpu

D_MODEL = 1024
HEAD_DIM = 64
RWKV_HEADS = 12
RWKV_WIDTH = RWKV_HEADS * HEAD_DIM
DECAY_LORA = 64
ICLR_LORA = 64
GATE_LORA = 128
LORA_WIDTH = DECAY_LORA + ICLR_LORA + GATE_LORA
ATTN_GROUPS = ((128, 1), (512, 4), (2048, 16))
HEADS_PER_GROUP = 4
GROUP_WIDTH = HEADS_PER_GROUP * HEAD_DIM
ATTN_WIDTH = len(ATTN_GROUPS) * GROUP_WIDTH
ATTN_BLOCK = 128
ROPE_THETA = 10000.0
SHIFT_WIDTH = 3 * RWKV_WIDTH + LORA_WIDTH
NORM_EPS = 1e-6
GN_EPS = 64e-5

CHUNK = 64
INV_BASE = 8
LANES = 128
NEG_BIG = -1e30
ATTN_BLOCKS_PER_ITER = 4
VMEM_LIMIT = 56 * 1024 * 1024

F32 = jnp.float32
BF16 = jnp.bfloat16


def _bf(x):
    return x.astype(BF16)


def _div(x, n):
    assert n & (n - 1) == 0
    return jnp.right_shift(x, n.bit_length() - 1)


def _rem(x, n):
    assert n & (n - 1) == 0
    return jnp.bitwise_and(x, n - 1)


def _sigmoid(x):
    return 1.0 / (1.0 + jnp.exp(-x))


def _mm(a, b):
    return jnp.dot(_bf(a), _bf(b), preferred_element_type=F32)


def _mm_nt(a, b):
    return lax.dot_general(_bf(a), _bf(b), (((1,), (1,)), ((), ())), preferred_element_type=F32)


def _mm_tn(a, b):
    return lax.dot_general(_bf(a), _bf(b), (((0,), (0,)), ((), ())), preferred_element_type=F32)


def _split_hi_lo(x):
    hi = _bf(x)
    lo = _bf(x - hi.astype(F32))
    return hi, lo


def _mm_exact_lhs(m_bf, x):
    hi, lo = _split_hi_lo(x)
    return (jnp.dot(m_bf, hi, preferred_element_type=F32)
            + jnp.dot(m_bf, lo, preferred_element_type=F32))


def _head_sum(x):
    tile = 2 * LANES
    ri = _div(lax.broadcasted_iota(jnp.int32, (tile, tile), 0), HEAD_DIM)
    ci = _div(lax.broadcasted_iota(jnp.int32, (tile, tile), 1), HEAD_DIM)
    ones_bd = jnp.where(ri == ci, 1.0, 0.0).astype(BF16)
    hi, lo = _split_hi_lo(x)
    parts = []
    for j in range(x.shape[-1] // tile):
        sl = slice(j * tile, (j + 1) * tile)
        parts.append(jnp.dot(hi[:, sl], ones_bd, preferred_element_type=F32)
                     + jnp.dot(lo[:, sl], ones_bd, preferred_element_type=F32))
    return jnp.concatenate(parts, axis=-1)


def _norm_modulate(x, g, shift, scale):
    y = x * lax.rsqrt(jnp.mean(x * x, axis=-1, keepdims=True) + NORM_EPS)
    return (y * g) * (1.0 + scale) + shift


def _rms(x, g):
    return x * lax.rsqrt(jnp.mean(x * x, axis=-1, keepdims=True) + NORM_EPS) * g


def _mod_kernel(c_ref, w_ref, b_ref, o_ref):
    c = c_ref[...]
    o_ref[...] = jnp.dot(c * _sigmoid(c), w_ref[...], preferred_element_type=F32,
                         precision=lax.Precision.HIGHEST) + b_ref[...]


def _mod(c, w_ada, b_ada):
    bsz, d = c.shape
    n = w_ada.shape[1]
    tn = d
    return pl.pallas_call(
        _mod_kernel,
        grid=(n // tn,),
        in_specs=[pl.BlockSpec((bsz, d), lambda j: (0, 0)),
                  pl.BlockSpec((d, tn), lambda j: (0, j)),
                  pl.BlockSpec((1, tn), lambda j: (0, j))],
        out_specs=pl.BlockSpec((bsz, tn), lambda j: (0, j)),
        out_shape=jax.ShapeDtypeStruct((bsz, n), F32),
        name="mod",
    )(c, w_ada, b_ada.reshape(1, n))


def _rwkv_prep_kernel(x_ref, shift_ref, scale_ref, g_ref, w_ref, mu_ref, wl_ref, w0_ref, a0_ref,
                      kk_ref, ka_ref, rk_ref,
                      aq_o, rq_o, bk_o, kd_o, bg_o, kg_o, v_o, gc_o, bonus_o, gate_o,
                      carry_ref):
    i = pl.program_id(1)
    tm = x_ref.shape[1]
    rw = RWKV_WIDTH

    @pl.when(i == 0)
    def _():
        carry_ref[...] = jnp.zeros_like(carry_ref)

    h = _norm_modulate(x_ref[0], g_ref[...], shift_ref[0], scale_ref[0])
    p = jnp.dot(_bf(h), w_ref[...], preferred_element_type=F32)

    prev_row = carry_ref[7:8, :]
    carry_ref[...] = p[tm - 8:, :]
    row = lax.broadcasted_iota(jnp.int32, p.shape, 0)
    shifted = jnp.where(row == 0, prev_row, pltpu.roll(p, 1, axis=0))
    ps = p + mu_ref[...] * (shifted - p)

    pr, pk, pv = ps[:, :rw], ps[:, rw:2 * rw], ps[:, 2 * rw:3 * rw]
    lora_in = ps[:, 3 * rw:]
    lane = lax.broadcasted_iota(jnp.int32, lora_in.shape, 1)
    z = jnp.where(lane < DECAY_LORA, jnp.tanh(lora_in),
                  jnp.where(lane < DECAY_LORA + ICLR_LORA, lora_in, _sigmoid(lora_in)))
    lo = jnp.dot(_bf(z), wl_ref[...], preferred_element_type=F32)
    wf = w0_ref[...] + lo[:, :rw]
    a = _sigmoid(a0_ref[...] + lo[:, rw:2 * rw])
    gate_o[0] = lo[:, 2 * rw:]

    lw = -math.exp(-0.5) * _sigmoid(wf)

    kkr = pk * kk_ref[...]
    kkn = kkr * (1.0 / jnp.maximum(jnp.sqrt(_head_sum(kkr * kkr)), 1e-12))
    k = pk * (1.0 + (a - 1.0) * ka_ref[...])
    b = kkn * a
    bonus_o[0] = _head_sum(pr * k * rk_ref[...]) * pv
    v_o[0] = pv

    ri = lax.broadcasted_iota(jnp.int32, (tm, tm), 0)
    ci = lax.broadcasted_iota(jnp.int32, (tm, tm), 1)
    same = _div(ri, CHUNK) == _div(ci, CHUNK)
    lower = jnp.where(same & (ci <= ri), 1.0, 0.0).astype(BF16)
    upper = jnp.where(same & (ci > ri), 1.0, 0.0).astype(BF16)
    cum = _mm_exact_lhs(lower, lw)
    rest = _mm_exact_lhs(upper, lw)
    nck = tm // CHUNK
    si = lax.broadcasted_iota(jnp.int32, (nck, tm), 0)
    sj = _div(lax.broadcasted_iota(jnp.int32, (nck, tm), 1), CHUNK)
    total = _mm_exact_lhs(jnp.where(si == sj, 1.0, 0.0).astype(BF16), lw)
    gc_o[0, 0] = jnp.exp(total)

    e_cum = jnp.exp(cum)
    e_inv = jnp.exp(-cum)
    e_rest = jnp.exp(rest)
    aq_o[0] = -kkn * jnp.exp(cum - lw)
    rq_o[0] = pr * e_cum
    bk_o[0] = b * e_inv
    kd_o[0] = k * e_inv
    bg_o[0] = b * e_rest
    kg_o[0] = k * e_rest


def _rwkv_prep(x, mod3, g_pre, w_rwkv, mu, w_lora, w0, a0, k_k, k_a, r_k, tm):
    bsz, seq, d = x.shape
    rw = RWKV_WIDTH
    nck = tm // CHUNK
    row_spec = pl.BlockSpec((1, tm, rw), lambda b, i: (b, i, 0))
    vec = lambda n: pl.BlockSpec((1, n), lambda b, i: (0, 0))
    big = jax.ShapeDtypeStruct((bsz, seq, rw), F32)
    return pl.pallas_call(
        _rwkv_prep_kernel,
        grid=(bsz, seq // tm),
        in_specs=[pl.BlockSpec((1, tm, d), lambda b, i: (b, i, 0)),
                  pl.BlockSpec((1, 1, d), lambda b, i: (b, 0, 0)),
                  pl.BlockSpec((1, 1, d), lambda b, i: (b, 0, 1)),
                  vec(d),
                  pl.BlockSpec((d, SHIFT_WIDTH), lambda b, i: (0, 0)),
                  vec(SHIFT_WIDTH),
                  pl.BlockSpec((LORA_WIDTH, 3 * rw), lambda b, i: (0, 0)),
                  vec(rw), vec(rw), vec(rw), vec(rw), vec(rw)],
        out_specs=[row_spec] * 7
                  + [pl.BlockSpec((1, 1, nck, rw), lambda b, i: (b, i, 0, 0)), row_spec, row_spec],
        out_shape=[big] * 7
                  + [jax.ShapeDtypeStruct((bsz, seq // tm, nck, rw), F32), big, big],
        scratch_shapes=[pltpu.VMEM((8, SHIFT_WIDTH), F32)],
        compiler_params=pltpu.CompilerParams(
            dimension_semantics=("arbitrary", "arbitrary"), vmem_limit_bytes=VMEM_LIMIT),
        name="rwkv_prep",
    )(x, mod3, mod3, g_pre, w_rwkv, mu, w_lora, w0, a0, k_k, k_a, r_k)


def _pair_blockdiag(x):
    first = lax.broadcasted_iota(jnp.int32, x.shape, 1) < HEAD_DIM
    zero = jnp.zeros_like(x)
    return jnp.concatenate([jnp.where(first, x, zero), jnp.where(first, zero, x)], axis=0)


def _pmm(a, b):
    return jnp.dot(_bf(a), _pair_blockdiag(_bf(b)), preferred_element_type=F32)


def _pmm_nt(a, b):
    return lax.dot_general(_bf(a), _pair_blockdiag(_bf(b)), (((1,), (1,)), ((), ())),
                           preferred_element_type=F32)


def _pmm_tn(a, b):
    full = lax.dot_general(_bf(a), _bf(b), (((0,), (0,)), ((), ())), preferred_element_type=F32)
    first = lax.broadcasted_iota(jnp.int32, (HEAD_DIM, LANES), 1) < HEAD_DIM
    return jnp.where(first, full[:HEAD_DIM], full[HEAD_DIM:])


def _unit_lower_inverse(a_list, masks):
    eye, diag_mask, merge_masks = masks
    ps = [jnp.where(diag_mask, a, 0.0) for a in a_list]
    ts = [eye + p for p in ps]
    m = 1
    while 2 * m < INV_BASE:
        ps = [_pmm(p, p) for p in ps]
        ts = [t + _pmm(t, p) for t, p in zip(ts, ps)]
        m *= 2
    for mk in merge_masks:
        ws = [_pmm(jnp.where(mk, a, 0.0), t) for a, t in zip(a_list, ts)]
        ts = [t + _pmm(t, w) for t, w in zip(ts, ws)]
    return ts


def _rwkv_scan_kernel(aq_ref, rq_ref, bk_ref, kd_ref, bg_ref, kg_ref, v_ref, gc_ref, y_ref, st_ref):
    @pl.when(pl.program_id(1) == 0)
    def _():
        st_ref[...] = jnp.zeros_like(st_ref)

    ri = lax.broadcasted_iota(jnp.int32, (CHUNK, LANES), 0)
    ci = _rem(lax.broadcasted_iota(jnp.int32, (CHUNK, LANES), 1), HEAD_DIM)
    strict = ri > ci
    incl = ri >= ci
    ri2 = lax.broadcasted_iota(jnp.int32, (2 * CHUNK, LANES), 0)
    ci2 = _rem(lax.broadcasted_iota(jnp.int32, (2 * CHUNK, LANES), 1), HEAD_DIM)
    rloc = _rem(ri2, CHUNK)
    stacked = (rloc > ci2) | ((rloc == ci2) & (ri2 >= CHUNK))
    eye = jnp.where(ri == ci, 1.0, 0.0)
    diag_mask = _div(ri, INV_BASE) == _div(ci, INV_BASE)
    merge_masks = []
    bsize = INV_BASE
    while bsize < CHUNK:
        merge_masks.append((_div(ri, 2 * bsize) == _div(ci, 2 * bsize))
                           & (_div(ri, bsize) != _div(ci, bsize)))
        bsize *= 2
    masks = (eye, diag_mask, merge_masks)

    n_rows = aq_ref.shape[0]
    n_chunks = aq_ref.shape[1] // CHUNK
    n_pairs = RWKV_WIDTH // LANES
    seqs = [(b, j) for b in range(n_rows) for j in range(n_pairs)]
    items = [(c, s) for c in range(n_chunks) for s in seqs]

    def tile(ref, c, s):
        b, j = s
        return ref[b, c * CHUNK:(c + 1) * CHUNK, j * LANES:(j + 1) * LANES]

    qs = {it: jnp.concatenate([tile(aq_ref, *it), tile(rq_ref, *it)], axis=0) for it in items}
    ab = {it: _pmm_nt(qs[it], tile(bk_ref, *it)) for it in items}
    ak = {it: jnp.where(stacked, _pmm_nt(qs[it], tile(kd_ref, *it)), 0.0) for it in items}
    akv = {it: _pmm(ak[it], tile(v_ref, *it)) for it in items}
    t_list = _unit_lower_inverse([jnp.where(strict, ab[it][:CHUNK], 0.0) for it in items], masks)
    t = dict(zip(items, t_list))

    state = {s: st_ref[s[0], s[1]] for s in seqs}
    for c in range(n_chunks):
        xs = {s: _pmm_nt(qs[(c, s)], state[s]) for s in seqs}
        u = {s: _pmm(t[(c, s)], xs[s][:CHUNK] + akv[(c, s)][:CHUNK]) for s in seqs}
        for s in seqs:
            b, j = s
            y = xs[s][CHUNK:] + akv[(c, s)][CHUNK:] + _pmm(
                jnp.where(incl, ab[(c, s)][CHUNK:], 0.0), u[s])
            y_ref[b, c * CHUNK:(c + 1) * CHUNK, j * LANES:(j + 1) * LANES] = y
        new_state = {}
        for s in seqs:
            b, j = s
            uv = jnp.concatenate([u[s], tile(v_ref, c, s)], axis=0)
            bkg = jnp.concatenate([tile(bg_ref, c, s), tile(kg_ref, c, s)], axis=0)
            gc = gc_ref[b, c, :, j * LANES:(j + 1) * LANES]
            new_state[s] = state[s] * gc + _pmm_tn(uv, bkg)
        state = new_state
    for s in seqs:
        st_ref[s[0], s[1]] = state[s]


def _rwkv_scan(aq, rq, bk, kd, bg, kg, v, gc, rows_per_step, chunks_per_step):
    bsz, seq, rw = aq.shape
    nchunk = seq // CHUNK
    ts = chunks_per_step * CHUNK
    row_spec = pl.BlockSpec((rows_per_step, ts, rw), lambda b, c: (b, c, 0))
    gc4 = gc.reshape(bsz, nchunk, 1, rw)
    return pl.pallas_call(
        _rwkv_scan_kernel,
        grid=(bsz // rows_per_step, seq // ts),
        in_specs=[row_spec] * 7
                 + [pl.BlockSpec((rows_per_step, chunks_per_step, 1, rw), lambda b, c: (b, c, 0, 0))],
        out_specs=row_spec,
        out_shape=jax.ShapeDtypeStruct((bsz, seq, rw), F32),
        scratch_shapes=[pltpu.VMEM((rows_per_step, rw // LANES, HEAD_DIM, LANES), F32)],
        compiler_params=pltpu.CompilerParams(dimension_semantics=("arbitrary", "arbitrary")),
        name="rwkv_scan",
    )(aq, rq, bk, kd, bg, kg, v, gc4)


def _attn_proj_kernel(x_ref, shift_ref, scale_ref, g_ref, w_ref, cos_ref, sin_ref, *rest):
    outs, buf = rest[:-1], rest[-1]
    tm = x_ref.shape[1]
    aw = ATTN_WIDTH
    h = _norm_modulate(x_ref[0], g_ref[...], shift_ref[0], scale_ref[0])
    p = jnp.dot(_bf(h), w_ref[...], preferred_element_type=F32)
    reps = aw // LANES
    cos = jnp.concatenate([cos_ref[...]] * reps, axis=-1)
    sin = jnp.concatenate([sin_ref[...]] * reps, axis=-1)
    lane = lax.broadcasted_iota(jnp.int32, (tm, aw), 1)
    first_half = _rem(lane, HEAD_DIM) < (HEAD_DIM // 2)

    def rope(t):
        partner = jnp.where(first_half, pltpu.roll(t, aw - HEAD_DIM // 2, axis=1),
                            pltpu.roll(t, HEAD_DIM // 2, axis=1))
        return t * cos + partner * sin

    roped = (rope(p[:, :aw]) * (HEAD_DIM ** -0.5), rope(p[:, aw:2 * aw]), p[:, 2 * aw:])
    tiles_per_group = GROUP_WIDTH // LANES
    n = 0
    for which in range(3):
        for g, (_, dil) in enumerate(ATTN_GROUPS):
            o_ref = outs[n]
            n += 1
            for ct in range(tiles_per_group):
                c0 = g * GROUP_WIDTH + ct * LANES
                val = roped[which][:, c0:c0 + LANES]
                if dil == 1:
                    o_ref[0, 0, :, ct * LANES:(ct + 1) * LANES] = _bf(val)
                    continue
                slot = (which * len(ATTN_GROUPS) + g) * tiles_per_group + ct
                buf[slot] = val
                for r in range(dil):
                    o_ref[0, r, :, ct * LANES:(ct + 1) * LANES] = _bf(
                        buf[slot, pl.ds(r, tm // dil, stride=dil), :])


def _attn_proj(x, mod3, g_pre, w_qkv, cos_t, sin_t, tm):
    bsz, seq, d = x.shape
    out_specs, out_shape = [], []
    for _ in range(3):
        for _, dil in ATTN_GROUPS:
            out_specs.append(pl.BlockSpec((1, dil, tm // dil, GROUP_WIDTH), lambda b, i: (b, 0, i, 0)))
            out_shape.append(jax.ShapeDtypeStruct((bsz, dil, seq // dil, GROUP_WIDTH), BF16))
    return pl.pallas_call(
        _attn_proj_kernel,
        grid=(bsz, seq // tm),
        in_specs=[pl.BlockSpec((1, tm, d), lambda b, i: (b, i, 0)),
                  pl.BlockSpec((1, 1, d), lambda b, i: (b, 0, 0)),
                  pl.BlockSpec((1, 1, d), lambda b, i: (b, 0, 1)),
                  pl.BlockSpec((1, d), lambda b, i: (0, 0)),
                  pl.BlockSpec((d, 3 * ATTN_WIDTH), lambda b, i: (0, 0)),
                  pl.BlockSpec((tm, LANES), lambda b, i: (i, 0)),
                  pl.BlockSpec((tm, LANES), lambda b, i: (i, 0))],
        out_specs=out_specs,
        out_shape=out_shape,
        scratch_shapes=[pltpu.VMEM((3 * ATTN_WIDTH // LANES, tm, LANES), F32)],
        compiler_params=pltpu.CompilerParams(
            dimension_semantics=("arbitrary", "arbitrary"), vmem_limit_bytes=VMEM_LIMIT),
        name="attn_proj",
    )(x, mod3, mod3, g_pre, w_qkv, cos_t, sin_t)


def _attn_kernel(*refs):
    qkv = refs[:9]
    o_ref = refs[9]
    num_ref, m_ref, den_ref = refs[10:]
    blk = ATTN_BLOCK
    qi = lax.broadcasted_iota(jnp.int32, (blk, blk), 0)
    kj = lax.broadcasted_iota(jnp.int32, (blk, blk), 1)
    bias_cur = jnp.where(kj <= qi, 0.0, NEG_BIG)
    bias_prev = jnp.where(kj >= qi, 0.0, NEG_BIG)
    lane = lax.broadcasted_iota(jnp.int32, (blk, LANES), 1)
    head0 = lane < HEAD_DIM

    for g, (_, dil) in enumerate(ATTN_GROUPS):
        q_ref, k_ref, v_ref = qkv[g], qkv[3 + g], qkv[6 + g]
        nb = q_ref.shape[2] // blk

        nper = min(nb, ATTN_BLOCKS_PER_ITER)
        rper = ATTN_BLOCKS_PER_ITER // nper
        whole = nper == nb

        def body(idx, carry, q_ref=q_ref, k_ref=k_ref, v_ref=v_ref, dil=dil, nb=nb, g=g,
                 nper=nper, rper=rper, whole=whole):
            if whole:
                r0, n0 = idx * rper, 0
                base = 0
            else:
                r0 = _div(idx, nb // nper)
                n0 = _rem(idx, nb // nper) * nper
                base = pl.multiple_of(n0 * blk, blk)
            nt = lambda a, b: lax.dot_general(a, b, (((1,), (1,)), ((), ())), preferred_element_type=F32)
            one = jnp.ones((blk, LANES), BF16)
            zero = jnp.zeros((blk, LANES), BF16)
            q_msk, ks, v_aug = {}, {}, {}
            for rr in range(rper):
                r = r0 + rr
                first = 0 if whole else -1
                for i in range(first, nper):
                    if i < 0:
                        st = pl.multiple_of(jnp.maximum(n0 - 1, 0) * blk, blk)
                    else:
                        st = base + i * blk
                    ks[(rr, i)] = k_ref[0, r, pl.ds(st, blk), :]
                    v = v_ref[0, r, pl.ds(st, blk), :]
                    v_aug[(rr, i)] = (jnp.where(head0, v, one), jnp.where(head0, one, v))
                    if i >= 0:
                        q = q_ref[0, r, pl.ds(st, blk), :]
                        q_msk[(rr, i)] = (jnp.where(head0, q, zero), jnp.where(head0, zero, q))
            items = [(rr, i, e) for rr in range(rper) for i in range(nper) for e in range(2)]
            has_prev = lambda it: (it[0], it[1] - 1) in ks
            s_c = {it: nt(q_msk[it[:2]][it[2]], ks[it[:2]]) for it in items}
            s_p = {it: nt(q_msk[it[:2]][it[2]], ks[(it[0], it[1] - 1)]) for it in items if has_prev(it)}
            bias_first = bias_prev if whole else jnp.where(n0 > 0, bias_prev, NEG_BIG)
            ms, pcs, pps = {}, {}, {}
            for it in items:
                sc = s_c[it] + bias_cur
                if has_prev(it):
                    sp = s_p[it] + (bias_first if it[1] == 0 else bias_prev)
                    m = jnp.max(jnp.maximum(sc, sp), axis=-1, keepdims=True)
                    pps[it] = _bf(jnp.exp(sp - m))
                else:
                    m = jnp.max(sc, axis=-1, keepdims=True)
                ms[it] = m
                pcs[it] = _bf(jnp.exp(sc - m))
            pv = {}
            for it in items:
                acc = jnp.dot(pcs[it], v_aug[it[:2]][it[2]], preferred_element_type=F32)
                if has_prev(it):
                    acc = acc + jnp.dot(pps[it], v_aug[(it[0], it[1] - 1)][it[2]],
                                        preferred_element_type=F32)
                pv[it] = acc
            for rr in range(rper):
                for i in range(nper):
                    m_b = jnp.where(head0, ms[(rr, i, 0)], ms[(rr, i, 1)])
                    num_b = jnp.where(head0, pv[(rr, i, 0)], pv[(rr, i, 1)])
                    l_b = pltpu.roll(jnp.where(head0, pv[(rr, i, 1)], pv[(rr, i, 0)]), HEAD_DIM, axis=1)
                    if g == 0:
                        rows = pl.ds(base + i * blk, blk)
                        num_ref[rows, :] = num_b
                        m_ref[rows, :] = m_b
                        den_ref[rows, :] = l_b
                    else:
                        rows = pl.ds((n0 + i) * (blk * dil) + r0 + rr, blk, stride=dil)
                        m_old = m_ref[rows, :]
                        m_new = jnp.maximum(m_old, m_b)
                        w_old = jnp.exp(m_old - m_new)
                        w_b = jnp.exp(m_b - m_new)
                        num_ref[rows, :] = num_ref[rows, :] * w_old + num_b * w_b
                        den_ref[rows, :] = den_ref[rows, :] * w_old + l_b * w_b
                        m_ref[rows, :] = m_new
            return carry

        lax.fori_loop(0, dil * nb // (nper * rper), body, 0)

    o_ref[0] = _bf(num_ref[...] / den_ref[...])


def _attn(qkv, seq):
    bsz = qkv[0].shape[0]
    in_specs = []
    for _ in range(3):
        for _, dil in ATTN_GROUPS:
            in_specs.append(pl.BlockSpec((1, dil, seq // dil, LANES), lambda b, sp: (b, 0, 0, sp)))
    return pl.pallas_call(
        _attn_kernel,
        grid=(bsz, GROUP_WIDTH // LANES),
        in_specs=in_specs,
        out_specs=pl.BlockSpec((1, seq, LANES), lambda b, sp: (b, 0, sp)),
        out_shape=jax.ShapeDtypeStruct((bsz, seq, GROUP_WIDTH), BF16),
        scratch_shapes=[pltpu.VMEM((seq, LANES), F32)] * 3,
        compiler_params=pltpu.CompilerParams(
            dimension_semantics=("arbitrary", "arbitrary"), vmem_limit_bytes=VMEM_LIMIT),
        name="attn",
    )(*qkv)


def _merge_kernel(x_ref, shift_ref, scale_ref, gatem_ref, g_ref, wg_ref, y_ref, bonus_ref, gate_ref,
                  lnw_ref, lnb_ref, wa_ref, o_ref, wb_ref, wout_ref, gpost_ref, out_ref):
    x = x_ref[0]
    h = _norm_modulate(x, g_ref[...], shift_ref[0], scale_ref[0])
    pg = jnp.dot(_bf(h), wg_ref[...], preferred_element_type=F32)
    y = y_ref[0]
    inv_n = 1.0 / HEAD_DIM
    yc = y - _head_sum(y) * inv_n
    var = _head_sum(yc * yc) * inv_n
    yn = yc * lax.rsqrt(var + GN_EPS) * lnw_ref[...] + lnb_ref[...]
    ya = jnp.dot(_bf((yn + bonus_ref[0]) * gate_ref[0]), wa_ref[...], preferred_element_type=F32)
    yb = jnp.dot(o_ref[0], wb_ref[...], preferred_element_type=F32)
    merged = _sigmoid(pg[:, :D_MODEL]) * ya + _sigmoid(pg[:, D_MODEL:]) * yb
    zz = jnp.dot(_bf(merged), wout_ref[...], preferred_element_type=F32)
    out_ref[0] = x + gatem_ref[0] * _rms(zz, gpost_ref[...])


def _merge(x, mod3, g_pre, w_g, y, bonus, gate, lnx_w, lnx_b, w_a, o_att, w_b, w_out, g_post, tm):
    bsz, seq, d = x.shape
    rw = RWKV_WIDTH
    const = lambda shape: pl.BlockSpec(shape, lambda b, i: (0, 0))
    row = lambda n: pl.BlockSpec((1, tm, n), lambda b, i: (b, i, 0))
    modv = lambda j: pl.BlockSpec((1, 1, d), lambda b, i: (b, 0, j))
    return pl.pallas_call(
        _merge_kernel,
        grid=(bsz, seq // tm),
        in_specs=[row(d), modv(0), modv(1), modv(2), const((1, d)), const((d, 2 * d)),
                  row(rw), row(rw), row(rw), const((1, rw)), const((1, rw)), const((rw, d)),
                  row(GROUP_WIDTH), const((GROUP_WIDTH, d)), const((d, d)), const((1, d))],
        out_specs=row(d),
        out_shape=jax.ShapeDtypeStruct((bsz, seq, d), F32),
        compiler_params=pltpu.CompilerParams(
            dimension_semantics=("arbitrary", "arbitrary"), vmem_limit_bytes=VMEM_LIMIT),
        name="merge",
    )(x, mod3, mod3, mod3, g_pre, w_g, y, bonus, gate, lnx_w, lnx_b, w_a, o_att, w_b, w_out, g_post)


def _ffn_kernel(x_ref, shift_ref, scale_ref, gatef_ref, g_ref, win_ref, wout_ref, gpost_ref, out_ref):
    x = x_ref[0]
    dff = wout_ref.shape[0]
    h = _bf(_norm_modulate(x, g_ref[...], shift_ref[0], scale_ref[0]))
    u = jnp.dot(h, win_ref[:, :dff], preferred_element_type=F32)
    gt = jnp.dot(h, win_ref[:, dff:], preferred_element_type=F32)
    act = _bf(u * _sigmoid(u) * gt)
    f = jnp.dot(act, wout_ref[...], preferred_element_type=F32)
    out_ref[0] = x + gatef_ref[0] * _rms(f, gpost_ref[...])


def _ffn(x, mod3, g_pre, w_in, w_out, g_post, tm):
    bsz, seq, d = x.shape
    dff = w_out.shape[0]
    const = lambda shape: pl.BlockSpec(shape, lambda b, i: (0, 0))
    row = pl.BlockSpec((1, tm, d), lambda b, i: (b, i, 0))
    modv = lambda j: pl.BlockSpec((1, 1, d), lambda b, i: (b, 0, j))
    return pl.pallas_call(
        _ffn_kernel,
        grid=(bsz, seq // tm),
        in_specs=[row, modv(3), modv(4), modv(5), const((1, d)), const((d, 2 * dff)),
                  const((dff, d)), const((1, d))],
        out_specs=row,
        out_shape=jax.ShapeDtypeStruct((bsz, seq, d), F32),
        compiler_params=pltpu.CompilerParams(
            dimension_semantics=("arbitrary", "arbitrary"), vmem_limit_bytes=VMEM_LIMIT),
        name="ffn",
    )(x, mod3, mod3, mod3, g_pre, w_in, w_out, g_post)


def _rope_tables(seq):
    half = HEAD_DIM // 2
    inv_freq = ROPE_THETA ** (-jnp.arange(half, dtype=F32) / half)
    ang = jnp.arange(seq, dtype=F32)[:, None] * inv_freq[None, :]
    cos, sin = jnp.cos(ang), jnp.sin(ang)
    reps = LANES // HEAD_DIM
    cos_t = jnp.tile(jnp.concatenate([cos, cos], axis=-1), (1, reps))
    sin_t = jnp.tile(jnp.concatenate([-sin, sin], axis=-1), (1, reps))
    return cos_t, sin_t


def _layer(x, c, w_ada, b_ada, g_pre_mix, g_post_mix, w_in, mu_shift, w0, w2, a0, a2, g2,
           k_k, k_a, r_k, lnx_w, lnx_b, w_a, w_b, w_out, g_pre_ffn, g_post_ffn, w_ffn_in, w_ffn_out):
    bsz, seq, d = x.shape
    rw = RWKV_WIDTH
    row1 = lambda t: t.reshape(1, -1)
    mod3 = _mod(c, w_ada, b_ada).reshape(bsz, 1, 6 * d)

    w_rwkv = _bf(w_in[:, :SHIFT_WIDTH])
    w_qkv = _bf(w_in[:, SHIFT_WIDTH:SHIFT_WIDTH + 3 * ATTN_WIDTH])
    w_gates = _bf(w_in[:, SHIFT_WIDTH + 3 * ATTN_WIDTH:])
    w_lora = jnp.zeros((LORA_WIDTH, 3 * rw), F32)
    w_lora = w_lora.at[:DECAY_LORA, :rw].set(w2)
    w_lora = w_lora.at[DECAY_LORA:DECAY_LORA + ICLR_LORA, rw:2 * rw].set(a2)
    w_lora = _bf(w_lora.at[DECAY_LORA + ICLR_LORA:, 2 * rw:].set(g2))

    prep = _rwkv_prep(x, mod3, row1(g_pre_mix), w_rwkv, row1(mu_shift), w_lora, row1(w0), row1(a0),
                      row1(k_k), row1(k_a), row1(r_k), tm=256)
    aq, rq, bk, kd, bg, kg, v, gc, bonus, gate = prep
    y = _rwkv_scan(aq, rq, bk, kd, bg, kg, v, gc.reshape(bsz, seq // CHUNK, rw),
                   rows_per_step=4, chunks_per_step=1)

    cos_t, sin_t = _rope_tables(seq)
    qkv = _attn_proj(x, mod3, row1(g_pre_mix), w_qkv, cos_t, sin_t, tm=256)
    o_att = _attn(qkv, seq)

    x1 = _merge(x, mod3, row1(g_pre_mix), w_gates, y, bonus, gate, row1(lnx_w), row1(lnx_b),
                _bf(w_a), o_att, _bf(w_b), _bf(w_out), row1(g_post_mix), tm=256)
    return _ffn(x1, mod3, row1(g_pre_ffn), _bf(w_ffn_in), _bf(w_ffn_out), row1(g_post_ffn), tm=256)


def kernel(x, c, w_ada, b_ada, g_pre_mix, g_post_mix, w_in, mu_shift, w0, w2, a0, a2, g2, k_k, k_a, r_k,
           lnx_w, lnx_b, w_a, w_b, w_out, g_pre_ffn, g_post_ffn, w_ffn_in, w_ffn_out):
    for i in range(w_in.shape[0]):
        x = _layer(x, c, w_ada[i], b_ada[i], g_pre_mix[i], g_post_mix[i], w_in[i], mu_shift[i], w0[i],
                   w2[i], a0[i], a2[i], g2[i], k_k[i], k_a[i], r_k[i], lnx_w[i], lnx_b[i], w_a[i], w_b[i],
                   w_out[i], g_pre_ffn[i], g_post_ffn[i], w_ffn_in[i], w_ffn_out[i])
    return x
```

```python
import functools
import math

import jax
import jax.numpy as jnp
from jax import lax
from jax.experimental import pallas as pl
from jax.experimental.pallas import tpu as pltpu

D_MODEL = 1024
HEAD_DIM = 64
RWKV_HEADS = 12
RWKV_WIDTH = RWKV_HEADS * HEAD_DIM
DECAY_LORA = 64
ICLR_LORA = 64
GATE_LORA = 128
LORA_WIDTH = DECAY_LORA + ICLR_LORA + GATE_LORA
ATTN_GROUPS = ((128, 1), (512, 4), (2048, 16))
HEADS_PER_GROUP = 4
GROUP_WIDTH = HEADS_PER_GROUP * HEAD_DIM
ATTN_WIDTH = len(ATTN_GROUPS) * GROUP_WIDTH
ATTN_BLOCK = 128
ROPE_THETA = 10000.0
SHIFT_WIDTH = 3 * RWKV_WIDTH + LORA_WIDTH
NORM_EPS = 1e-6
GN_EPS = 64e-5

CHUNK = 64
INV_BASE = 16
LANES = 128
NEG_BIG = -1e30
LOG2E = math.log2(math.e)
ATTN_BLOCKS_PER_ITER = 8
SUB_ROWS = 256
VMEM_LIMIT = 56 * 1024 * 1024

F32 = jnp.float32
BF16 = jnp.bfloat16


def _bf(x):
    return x.astype(BF16)


def _div(x, n):
    assert n & (n - 1) == 0
    return jnp.right_shift(x, n.bit_length() - 1)


def _rem(x, n):
    assert n & (n - 1) == 0
    return jnp.bitwise_and(x, n - 1)


def _sigmoid(x):
    return 1.0 / (1.0 + jnp.exp2(x * (-LOG2E)))


def _mm(a, b):
    return jnp.dot(_bf(a), _bf(b), preferred_element_type=F32)


def _mm_nt(a, b):
    return lax.dot_general(_bf(a), _bf(b), (((1,), (1,)), ((), ())), preferred_element_type=F32)


def _mm_tn(a, b):
    return lax.dot_general(_bf(a), _bf(b), (((0,), (0,)), ((), ())), preferred_element_type=F32)


def _split_hi_lo(x):
    hi = _bf(x)
    lo = _bf(x - hi.astype(F32))
    return hi, lo


def _mm_exact_lhs(m_bf, x):
    hi, lo = _split_hi_lo(x)
    return (jnp.dot(m_bf, hi, preferred_element_type=F32)
            + jnp.dot(m_bf, lo, preferred_element_type=F32))


def _head_sum(x):
    tile = 2 * LANES
    ri = _div(lax.broadcasted_iota(jnp.int32, (tile, tile), 0), HEAD_DIM)
    ci = _div(lax.broadcasted_iota(jnp.int32, (tile, tile), 1), HEAD_DIM)
    ones_bd = jnp.where(ri == ci, 1.0, 0.0).astype(BF16)
    hi, lo = _split_hi_lo(x)
    parts = []
    for j in range(x.shape[-1] // tile):
        sl = slice(j * tile, (j + 1) * tile)
        parts.append(jnp.dot(hi[:, sl], ones_bd, preferred_element_type=F32)
                     + jnp.dot(lo[:, sl], ones_bd, preferred_element_type=F32))
    return jnp.concatenate(parts, axis=-1)


def _norm_modulate(x, g, shift, scale):
    y = x * lax.rsqrt(jnp.mean(x * x, axis=-1, keepdims=True) + NORM_EPS)
    return (y * g) * (1.0 + scale) + shift


def _rms(x, g):
    return x * lax.rsqrt(jnp.mean(x * x, axis=-1, keepdims=True) + NORM_EPS) * g


def _mod_kernel(c_ref, w_ref, b_ref, o_ref):
    c = c_ref[...]
    o_ref[...] = jnp.dot(c * _sigmoid(c), w_ref[...], preferred_element_type=F32,
                         precision=lax.Precision.HIGHEST) + b_ref[...]


def _mod(c, w_ada, b_ada):
    bsz, d = c.shape
    n = w_ada.shape[1]
    tn = d
    return pl.pallas_call(
        _mod_kernel,
        grid=(n // tn,),
        in_specs=[pl.BlockSpec((bsz, d), lambda j: (0, 0)),
                  pl.BlockSpec((d, tn), lambda j: (0, j)),
                  pl.BlockSpec((1, tn), lambda j: (0, j))],
        out_specs=pl.BlockSpec((bsz, tn), lambda j: (0, j)),
        out_shape=jax.ShapeDtypeStruct((bsz, n), F32),
        name="mod",
    )(c, w_ada, b_ada.reshape(1, n))


def _rwkv_prep_kernel(x_ref, shift_ref, scale_ref, g_ref, w_ref, mu_ref, wl_ref, w0_ref, a0_ref,
                      kk_ref, ka_ref, rk_ref,
                      aq_o, rq_o, bk_o, kd_o, bg_o, kg_o, v_o, gc_o, bonus_o, gate_o,
                      carry_ref):
    i = pl.program_id(1)
    rw = RWKV_WIDTH
    sub = SUB_ROWS
    nck = sub // CHUNK

    @pl.when(i == 0)
    def _():
        carry_ref[...] = jnp.zeros_like(carry_ref)

    ri = lax.broadcasted_iota(jnp.int32, (sub, sub), 0)
    ci = lax.broadcasted_iota(jnp.int32, (sub, sub), 1)
    same = _div(ri, CHUNK) == _div(ci, CHUNK)
    lower = jnp.where(same & (ci <= ri), 1.0, 0.0).astype(BF16)
    upper = jnp.where(same & (ci > ri), 1.0, 0.0).astype(BF16)
    si = lax.broadcasted_iota(jnp.int32, (nck, sub), 0)
    sj = _div(lax.broadcasted_iota(jnp.int32, (nck, sub), 1), CHUNK)
    chunk_sel = jnp.where(si == sj, 1.0, 0.0).astype(BF16)
    row0 = lax.broadcasted_iota(jnp.int32, (sub, SHIFT_WIDTH), 0) == 0
    lane = lax.broadcasted_iota(jnp.int32, (sub, LORA_WIDTH), 1)

    prev_row = carry_ref[7:8, :]
    for sb in range(x_ref.shape[1] // sub):
        rows = slice(sb * sub, (sb + 1) * sub)
        h = _norm_modulate(x_ref[0, rows, :], g_ref[...], shift_ref[0], scale_ref[0])
        p = jnp.dot(_bf(h), w_ref[...], preferred_element_type=F32)
        shifted = jnp.where(row0, prev_row, pltpu.roll(p, 1, axis=0))
        last8 = p[sub - 8:, :]
        prev_row = last8[7:8, :]
        ps = p + mu_ref[...] * (shifted - p)

        pr, pk, pv = ps[:, :rw], ps[:, rw:2 * rw], ps[:, 2 * rw:3 * rw]
        lora_in = ps[:, 3 * rw:]
        z = jnp.where(lane < DECAY_LORA, jnp.tanh(lora_in),
                      jnp.where(lane < DECAY_LORA + ICLR_LORA, lora_in, _sigmoid(lora_in)))
        lo = jnp.dot(_bf(z), wl_ref[...], preferred_element_type=F32)
        wf = w0_ref[...] + lo[:, :rw]
        a = _sigmoid(a0_ref[...] + lo[:, rw:2 * rw])
        gate_o[0, rows, :] = lo[:, 2 * rw:]

        lw = (-math.exp(-0.5) * LOG2E) * _sigmoid(wf)

        kkr = pk * kk_ref[...]
        kkn = kkr * (1.0 / jnp.maximum(jnp.sqrt(_head_sum(kkr * kkr)), 1e-12))
        k = pk * (1.0 + (a - 1.0) * ka_ref[...])
        b = kkn * a
        bonus_o[0, rows, :] = _head_sum(pr * k * rk_ref[...]) * pv
        v_o[0, rows, :] = _bf(pv)

        cum = _mm_exact_lhs(lower, lw)
        rest = _mm_exact_lhs(upper, lw)
        gc_o[0, 0, sb * nck:(sb + 1) * nck, :] = jnp.exp2(_mm_exact_lhs(chunk_sel, lw))

        e_cum = jnp.exp2(cum)
        e_inv = jnp.exp2(-cum)
        e_rest = jnp.exp2(rest)
        aq_o[0, rows, :] = _bf(-kkn * jnp.exp2(cum - lw))
        rq_o[0, rows, :] = _bf(pr * e_cum)
        bk_o[0, rows, :] = _bf(b * e_inv)
        kd_o[0, rows, :] = _bf(k * e_inv)
        bg_o[0, rows, :] = _bf(b * e_rest)
        kg_o[0, rows, :] = _bf(k * e_rest)
    carry_ref[...] = last8


def _rwkv_prep(x, mod3, g_pre, w_rwkv, mu, w_lora, w0, a0, k_k, k_a, r_k, tm):
    bsz, seq, d = x.shape
    rw = RWKV_WIDTH
    nck = tm // CHUNK
    row_spec = pl.BlockSpec((1, tm, rw), lambda b, i: (b, i, 0))
    vec = lambda n: pl.BlockSpec((1, n), lambda b, i: (0, 0))
    big = jax.ShapeDtypeStruct((bsz, seq, rw), F32)
    return pl.pallas_call(
        _rwkv_prep_kernel,
        grid=(bsz, seq // tm),
        in_specs=[pl.BlockSpec((1, tm, d), lambda b, i: (b, i, 0)),
                  pl.BlockSpec((1, 1, d), lambda b, i: (b, 0, 0)),
                  pl.BlockSpec((1, 1, d), lambda b, i: (b, 0, 1)),
                  vec(d),
                  pl.BlockSpec((d, SHIFT_WIDTH), lambda b, i: (0, 0), pipeline_mode=pl.Buffered(1)),
                  vec(SHIFT_WIDTH),
                  pl.BlockSpec((LORA_WIDTH, 3 * rw), lambda b, i: (0, 0), pipeline_mode=pl.Buffered(1)),
                  vec(rw), vec(rw), vec(rw), vec(rw), vec(rw)],
        out_specs=[row_spec] * 7
                  + [pl.BlockSpec((1, 1, nck, rw), lambda b, i: (b, i, 0, 0)), row_spec, row_spec],
        out_shape=[jax.ShapeDtypeStruct((bsz, seq, rw), BF16)] * 7
                  + [jax.ShapeDtypeStruct((bsz, seq // tm, nck, rw), F32), big, big],
        scratch_shapes=[pltpu.VMEM((8, SHIFT_WIDTH), F32)],
        compiler_params=pltpu.CompilerParams(
            dimension_semantics=("arbitrary", "arbitrary"), vmem_limit_bytes=VMEM_LIMIT),
        name="rwkv_prep",
    )(x, mod3, mod3, g_pre, w_rwkv, mu, w_lora, w0, a0, k_k, k_a, r_k)


def _pair_blockdiag(x):
    first = lax.broadcasted_iota(jnp.int32, x.shape, 1) < HEAD_DIM
    zero = jnp.zeros_like(x)
    return jnp.concatenate([jnp.where(first, x, zero), jnp.where(first, zero, x)], axis=0)


def _pmm(a, b):
    return jnp.dot(_bf(a), _pair_blockdiag(_bf(b)), preferred_element_type=F32)


def _pmm_nt(a, b):
    return lax.dot_general(_bf(a), _pair_blockdiag(_bf(b)), (((1,), (1,)), ((), ())),
                           preferred_element_type=F32)


def _pmm_tn(a, b):
    full = lax.dot_general(_bf(a), _bf(b), (((0,), (0,)), ((), ())), preferred_element_type=F32)
    first = lax.broadcasted_iota(jnp.int32, (HEAD_DIM, LANES), 1) < HEAD_DIM
    return jnp.where(first, full[:HEAD_DIM], full[HEAD_DIM:])


def _unit_lower_inverse(a_list, masks):
    eye, diag_mask, merge_masks = masks
    ps = [jnp.where(diag_mask, a, 0.0) for a in a_list]
    ts = [eye + p for p in ps]
    ps = [_pmm(p, p) for p in ps]
    m = 2
    while 2 * m < INV_BASE:
        both = [_pmm(jnp.concatenate([p, t], axis=0), p) for p, t in zip(ps, ts)]
        ps = [x[:CHUNK] for x in both]
        ts = [t + x[CHUNK:] for t, x in zip(ts, both)]
        m *= 2
    ts = [t + _pmm(t, p) for t, p in zip(ts, ps)]
    for mk in merge_masks:
        ws = [_pmm(jnp.where(mk, a, 0.0), t) for a, t in zip(a_list, ts)]
        ts = [t + _pmm(t, w) for t, w in zip(ts, ws)]
    return ts


def _rwkv_scan_kernel(aq_ref, rq_ref, bk_ref, kd_ref, bg_ref, kg_ref, v_ref, gc_ref, y_ref, st_ref):
    @pl.when(pl.program_id(1) == 0)
    def _():
        st_ref[...] = jnp.zeros_like(st_ref)

    ri = lax.broadcasted_iota(jnp.int32, (CHUNK, LANES), 0)
    ci = _rem(lax.broadcasted_iota(jnp.int32, (CHUNK, LANES), 1), HEAD_DIM)
    strict = ri > ci
    incl = ri >= ci
    ri2 = lax.broadcasted_iota(jnp.int32, (2 * CHUNK, LANES), 0)
    ci2 = _rem(lax.broadcasted_iota(jnp.int32, (2 * CHUNK, LANES), 1), HEAD_DIM)
    rloc = _rem(ri2, CHUNK)
    stacked = (rloc > ci2) | ((rloc == ci2) & (ri2 >= CHUNK))
    eye = jnp.where(ri == ci, 1.0, 0.0)
    diag_mask = _div(ri, INV_BASE) == _div(ci, INV_BASE)
    merge_masks = []
    bsize = INV_BASE
    while bsize < CHUNK:
        merge_masks.append((_div(ri, 2 * bsize) == _div(ci, 2 * bsize))
                           & (_div(ri, bsize) != _div(ci, bsize)))
        bsize *= 2
    masks = (eye, diag_mask, merge_masks)

    n_rows = aq_ref.shape[0]
    n_chunks = aq_ref.shape[1] // CHUNK
    n_pairs = RWKV_WIDTH // LANES
    seqs = [(b, j) for b in range(n_rows) for j in range(n_pairs)]
    items = [(c, s) for c in range(n_chunks) for s in seqs]

    def tile(ref, c, s):
        b, j = s
        return ref[b, c * CHUNK:(c + 1) * CHUNK, j * LANES:(j + 1) * LANES]

    qs = {it: jnp.concatenate([tile(aq_ref, *it), tile(rq_ref, *it)], axis=0) for it in items}
    nt = lambda a, b: lax.dot_general(a, b, (((1,), (1,)), ((), ())), preferred_element_type=F32)
    abk = {it: nt(_bf(qs[it]), jnp.concatenate([_pair_blockdiag(_bf(tile(bk_ref, *it))),
                                                 _pair_blockdiag(_bf(tile(kd_ref, *it)))], axis=0))
           for it in items}
    ab = {it: abk[it][:, :LANES] for it in items}
    ak = {it: jnp.where(stacked, abk[it][:, LANES:], 0.0) for it in items}
    akv = {it: _pmm(ak[it], tile(v_ref, *it)) for it in items}
    t_list = _unit_lower_inverse([jnp.where(strict, ab[it][:CHUNK], 0.0) for it in items], masks)
    t = dict(zip(items, t_list))

    state = {s: st_ref[s[0], s[1]] for s in seqs}
    for c in range(n_chunks):
        xs = {s: _pmm_nt(qs[(c, s)], state[s]) for s in seqs}
        u = {s: _pmm(t[(c, s)], xs[s][:CHUNK] + akv[(c, s)][:CHUNK]) for s in seqs}
        for s in seqs:
            b, j = s
            y = xs[s][CHUNK:] + akv[(c, s)][CHUNK:] + _pmm(
                jnp.where(incl, ab[(c, s)][CHUNK:], 0.0), u[s])
            y_ref[b, c * CHUNK:(c + 1) * CHUNK, j * LANES:(j + 1) * LANES] = y
        new_state = {}
        for s in seqs:
            b, j = s
            uv = jnp.concatenate([_bf(u[s]), tile(v_ref, c, s)], axis=0)
            bkg = jnp.concatenate([tile(bg_ref, c, s), tile(kg_ref, c, s)], axis=0)
            gc = gc_ref[b, c, :, j * LANES:(j + 1) * LANES]
            new_state[s] = state[s] * gc + _pmm_tn(uv, bkg)
        state = new_state
    for s in seqs:
        st_ref[s[0], s[1]] = state[s]


def _rwkv_scan(aq, rq, bk, kd, bg, kg, v, gc, rows_per_step, chunks_per_step):
    bsz, seq, rw = aq.shape
    nchunk = seq // CHUNK
    ts = chunks_per_step * CHUNK
    row_spec = pl.BlockSpec((rows_per_step, ts, rw), lambda b, c: (b, c, 0))
    gc4 = gc.reshape(bsz, nchunk, 1, rw)
    return pl.pallas_call(
        _rwkv_scan_kernel,
        grid=(bsz // rows_per_step, seq // ts),
        in_specs=[row_spec] * 7
                 + [pl.BlockSpec((rows_per_step, chunks_per_step, 1, rw), lambda b, c: (b, c, 0, 0))],
        out_specs=row_spec,
        out_shape=jax.ShapeDtypeStruct((bsz, seq, rw), F32),
        scratch_shapes=[pltpu.VMEM((rows_per_step, rw // LANES, HEAD_DIM, LANES), F32)],
        compiler_params=pltpu.CompilerParams(dimension_semantics=("arbitrary", "arbitrary")),
        name="rwkv_scan",
    )(aq, rq, bk, kd, bg, kg, v, gc4)


def _attn_proj_kernel(x_ref, shift_ref, scale_ref, g_ref, w_ref, cos_ref, sin_ref, *rest):
    outs, buf = rest[:-1], rest[-1]
    aw = ATTN_WIDTH
    reps = aw // LANES
    lane = lax.broadcasted_iota(jnp.int32, (SUB_ROWS, aw), 1)
    first_half = _rem(lane, HEAD_DIM) < (HEAD_DIM // 2)
    tiles_per_group = GROUP_WIDTH // LANES

    for sb in range(x_ref.shape[1] // SUB_ROWS):
        rows = slice(sb * SUB_ROWS, (sb + 1) * SUB_ROWS)
        h = _norm_modulate(x_ref[0, rows, :], g_ref[...], shift_ref[0], scale_ref[0])
        p = jnp.dot(_bf(h), w_ref[...], preferred_element_type=F32)
        cos = jnp.concatenate([cos_ref[rows, :]] * reps, axis=-1)
        sin = jnp.concatenate([sin_ref[rows, :]] * reps, axis=-1)

        def rope(t, cos=cos, sin=sin):
            partner = jnp.where(first_half, pltpu.roll(t, aw - HEAD_DIM // 2, axis=1),
                                pltpu.roll(t, HEAD_DIM // 2, axis=1))
            return t * cos + partner * sin

        roped = (rope(p[:, :aw]) * (HEAD_DIM ** -0.5), rope(p[:, aw:2 * aw]), p[:, 2 * aw:])
        n = 0
        for which in range(3):
            for g, (_, dil) in enumerate(ATTN_GROUPS):
                o_ref = outs[n]
                n += 1
                out_rows = slice(sb * SUB_ROWS // dil, (sb + 1) * SUB_ROWS // dil)
                for ct in range(tiles_per_group):
                    c0 = g * GROUP_WIDTH + ct * LANES
                    val = roped[which][:, c0:c0 + LANES]
                    if dil == 1:
                        o_ref[0, 0, out_rows, ct * LANES:(ct + 1) * LANES] = _bf(val)
                        continue
                    slot = (which * len(ATTN_GROUPS) + g) * tiles_per_group + ct
                    buf[sb, slot] = val
                    for r in range(dil):
                        o_ref[0, r, out_rows, ct * LANES:(ct + 1) * LANES] = _bf(
                            buf[sb, slot, pl.ds(r, SUB_ROWS // dil, stride=dil), :])


def _attn_proj(x, mod3, g_pre, w_qkv, cos_t, sin_t, tm):
    bsz, seq, d = x.shape
    out_specs, out_shape = [], []
    for _ in range(3):
        for _, dil in ATTN_GROUPS:
            out_specs.append(pl.BlockSpec((1, dil, tm // dil, GROUP_WIDTH), lambda b, i: (b, 0, i, 0)))
            out_shape.append(jax.ShapeDtypeStruct((bsz, dil, seq // dil, GROUP_WIDTH), BF16))
    return pl.pallas_call(
        _attn_proj_kernel,
        grid=(bsz, seq // tm),
        in_specs=[pl.BlockSpec((1, tm, d), lambda b, i: (b, i, 0)),
                  pl.BlockSpec((1, 1, d), lambda b, i: (b, 0, 0)),
                  pl.BlockSpec((1, 1, d), lambda b, i: (b, 0, 1)),
                  pl.BlockSpec((1, d), lambda b, i: (0, 0)),
                  pl.BlockSpec((d, 3 * ATTN_WIDTH), lambda b, i: (0, 0), pipeline_mode=pl.Buffered(1)),
                  pl.BlockSpec((tm, LANES), lambda b, i: (i, 0)),
                  pl.BlockSpec((tm, LANES), lambda b, i: (i, 0))],
        out_specs=out_specs,
        out_shape=out_shape,
        scratch_shapes=[pltpu.VMEM((tm // SUB_ROWS, 3 * ATTN_WIDTH // LANES, SUB_ROWS, LANES), F32)],
        compiler_params=pltpu.CompilerParams(
            dimension_semantics=("arbitrary", "arbitrary"), vmem_limit_bytes=VMEM_LIMIT),
        name="attn_proj",
    )(x, mod3, mod3, g_pre, w_qkv, cos_t, sin_t)


def _attn_kernel(*refs):
    qkv = refs[:9]
    o_ref = refs[9]
    num_ref, m_ref, den_ref = refs[10:]
    blk = ATTN_BLOCK
    qi = lax.broadcasted_iota(jnp.int32, (blk, blk), 0)
    kj = lax.broadcasted_iota(jnp.int32, (blk, blk), 1)
    bias_cur = jnp.where(kj <= qi, 0.0, NEG_BIG)
    bias_prev = jnp.where(kj >= qi, 0.0, NEG_BIG)
    lane = lax.broadcasted_iota(jnp.int32, (blk, LANES), 1)
    head0 = lane < HEAD_DIM

    for g, (_, dil) in enumerate(ATTN_GROUPS):
        q_ref, k_ref, v_ref = qkv[g], qkv[3 + g], qkv[6 + g]
        nb = q_ref.shape[2] // blk

        nper = min(nb, ATTN_BLOCKS_PER_ITER)
        rper = ATTN_BLOCKS_PER_ITER // nper
        whole = nper == nb

        def body(idx, carry, q_ref=q_ref, k_ref=k_ref, v_ref=v_ref, dil=dil, nb=nb, g=g,
                 nper=nper, rper=rper, whole=whole):
            if whole:
                r0, n0 = idx * rper, 0
                base = 0
            else:
                r0 = _div(idx, nb // nper)
                n0 = _rem(idx, nb // nper) * nper
                base = pl.multiple_of(n0 * blk, blk)
            nt = lambda a, b: lax.dot_general(a, b, (((1,), (1,)), ((), ())), preferred_element_type=F32)
            one = jnp.ones((blk, LANES), BF16)
            zero = jnp.zeros((blk, LANES), BF16)
            q_msk, ks, v_aug = {}, {}, {}
            for rr in range(rper):
                r = r0 + rr
                first = 0 if whole else -1
                for i in range(first, nper):
                    if i < 0:
                        st = pl.multiple_of(jnp.maximum(n0 - 1, 0) * blk, blk)
                    else:
                        st = base + i * blk
                    ks[(rr, i)] = k_ref[0, r, pl.ds(st, blk), :]
                    v = v_ref[0, r, pl.ds(st, blk), :]
                    v_aug[(rr, i)] = (jnp.where(head0, v, one), jnp.where(head0, one, v))
                    if i >= 0:
                        q = q_ref[0, r, pl.ds(st, blk), :]
                        q_msk[(rr, i)] = (jnp.where(head0, q, zero), jnp.where(head0, zero, q))
            items = [(rr, i, e) for rr in range(rper) for i in range(nper) for e in range(2)]
            has_prev = lambda it: (it[0], it[1] - 1) in ks
            s_c = {it: nt(q_msk[it[:2]][it[2]], ks[it[:2]]) for it in items}
            s_p = {it: nt(q_msk[it[:2]][it[2]], ks[(it[0], it[1] - 1)]) for it in items if has_prev(it)}
            bias_first = bias_prev if whole else jnp.where(n0 > 0, bias_prev, NEG_BIG)
            ms, pcs, pps = {}, {}, {}
            for it in items:
                sc = s_c[it] + bias_cur
                if has_prev(it):
                    sp = s_p[it] + (bias_first if it[1] == 0 else bias_prev)
                    m = jnp.max(jnp.maximum(sc, sp), axis=-1, keepdims=True)
                    pps[it] = _bf(jnp.exp(sp - m))
                else:
                    m = jnp.max(sc, axis=-1, keepdims=True)
                ms[it] = m
                pcs[it] = _bf(jnp.exp(sc - m))
            pv = {}
            for it in items:
                acc = jnp.dot(pcs[it], v_aug[it[:2]][it[2]], preferred_element_type=F32)
                if has_prev(it):
                    acc = acc + jnp.dot(pps[it], v_aug[(it[0], it[1] - 1)][it[2]],
                                        preferred_element_type=F32)
                pv[it] = acc
            for rr in range(rper):
                for i in range(nper):
                    m_b = jnp.where(head0, ms[(rr, i, 0)], ms[(rr, i, 1)])
                    num_b = jnp.where(head0, pv[(rr, i, 0)], pv[(rr, i, 1)])
                    l_b = pltpu.roll(jnp.where(head0, pv[(rr, i, 1)], pv[(rr, i, 0)]), HEAD_DIM, axis=1)
                    if g == 0:
                        rows = pl.ds(base + i * blk, blk)
                        num_ref[rows, :] = num_b
                        m_ref[rows, :] = m_b
                        den_ref[rows, :] = l_b
                    else:
                        rows = pl.ds((n0 + i) * (blk * dil) + r0 + rr, blk, stride=dil)
                        m_old = m_ref[rows, :]
                        m_new = jnp.maximum(m_old, m_b)
                        w_old = jnp.exp(m_old - m_new)
                        w_b = jnp.exp(m_b - m_new)
                        num_ref[rows, :] = num_ref[rows, :] * w_old + num_b * w_b
                        den_ref[rows, :] = den_ref[rows, :] * w_old + l_b * w_b
                        m_ref[rows, :] = m_new
            return carry

        lax.fori_loop(0, dil * nb // (nper * rper), body, 0)

    o_ref[0] = _bf(num_ref[...] / den_ref[...])


def _attn(qkv, seq):
    bsz = qkv[0].shape[0]
    in_specs = []
    for _ in range(3):
        for _, dil in ATTN_GROUPS:
            in_specs.append(pl.BlockSpec((1, dil, seq // dil, LANES), lambda b, sp: (b, 0, 0, sp)))
    return pl.pallas_call(
        _attn_kernel,
        grid=(bsz, GROUP_WIDTH // LANES),
        in_specs=in_specs,
        out_specs=pl.BlockSpec((1, seq, LANES), lambda b, sp: (b, 0, sp)),
        out_shape=jax.ShapeDtypeStruct((bsz, seq, GROUP_WIDTH), BF16),
        scratch_shapes=[pltpu.VMEM((seq, LANES), F32)] * 3,
        compiler_params=pltpu.CompilerParams(
            dimension_semantics=("arbitrary", "arbitrary"), vmem_limit_bytes=VMEM_LIMIT),
        name="attn",
    )(*qkv)


def _merge_kernel(x_ref, shift_ref, scale_ref, gatem_ref, g_ref, wg_ref, y_ref, bonus_ref, gate_ref,
                  lnw_ref, lnb_ref, wa_ref, o_ref, wb_ref, wout_ref, gpost_ref, out_ref):
    inv_n = 1.0 / HEAD_DIM
    for r0 in range(0, x_ref.shape[1], SUB_ROWS):
        rows = slice(r0, r0 + SUB_ROWS)
        x = x_ref[0, rows, :]
        h = _norm_modulate(x, g_ref[...], shift_ref[0], scale_ref[0])
        pg = jnp.dot(_bf(h), wg_ref[...], preferred_element_type=F32)
        y = y_ref[0, rows, :]
        yc = y - _head_sum(y) * inv_n
        var = _head_sum(yc * yc) * inv_n
        yn = yc * lax.rsqrt(var + GN_EPS) * lnw_ref[...] + lnb_ref[...]
        ya = jnp.dot(_bf((yn + bonus_ref[0, rows, :]) * gate_ref[0, rows, :]), wa_ref[...],
                     preferred_element_type=F32)
        yb = jnp.dot(o_ref[0, rows, :], wb_ref[...], preferred_element_type=F32)
        merged = _sigmoid(pg[:, :D_MODEL]) * ya + _sigmoid(pg[:, D_MODEL:]) * yb
        zz = jnp.dot(_bf(merged), wout_ref[...], preferred_element_type=F32)
        out_ref[0, rows, :] = x + gatem_ref[0] * _rms(zz, gpost_ref[...])


def _merge(x, mod3, g_pre, w_g, y, bonus, gate, lnx_w, lnx_b, w_a, o_att, w_b, w_out, g_post, tm):
    bsz, seq, d = x.shape
    rw = RWKV_WIDTH
    const = lambda shape: pl.BlockSpec(shape, lambda b, i: (0, 0), pipeline_mode=pl.Buffered(1))
    row = lambda n: pl.BlockSpec((1, tm, n), lambda b, i: (b, i, 0))
    modv = lambda j: pl.BlockSpec((1, 1, d), lambda b, i: (b, 0, j))
    return pl.pallas_call(
        _merge_kernel,
        grid=(bsz, seq // tm),
        in_specs=[row(d), modv(0), modv(1), modv(2), const((1, d)), const((d, 2 * d)),
                  row(rw), row(rw), row(rw), const((1, rw)), const((1, rw)), const((rw, d)),
                  row(GROUP_WIDTH), const((GROUP_WIDTH, d)), const((d, d)), const((1, d))],
        out_specs=row(d),
        out_shape=jax.ShapeDtypeStruct((bsz, seq, d), F32),
        compiler_params=pltpu.CompilerParams(
            dimension_semantics=("arbitrary", "arbitrary"), vmem_limit_bytes=VMEM_LIMIT),
        name="merge",
    )(x, mod3, mod3, mod3, g_pre, w_g, y, bonus, gate, lnx_w, lnx_b, w_a, o_att, w_b, w_out, g_post)


def _ffn_kernel(x_ref, shift_ref, scale_ref, gatef_ref, g_ref, win_ref, wout_ref, gpost_ref, out_ref):
    dff = wout_ref.shape[0]
    for r0 in range(0, x_ref.shape[1], SUB_ROWS):
        rows = slice(r0, r0 + SUB_ROWS)
        x = x_ref[0, rows, :]
        h = _bf(_norm_modulate(x, g_ref[...], shift_ref[0], scale_ref[0]))
        u = jnp.dot(h, win_ref[:, :dff], preferred_element_type=F32)
        gt = jnp.dot(h, win_ref[:, dff:], preferred_element_type=F32)
        act = _bf(u * _sigmoid(u) * gt)
        f = jnp.dot(act, wout_ref[...], preferred_element_type=F32)
        out_ref[0, rows, :] = x + gatef_ref[0] * _rms(f, gpost_ref[...])


def _ffn(x, mod3, g_pre, w_in, w_out, g_post, tm):
    bsz, seq, d = x.shape
    dff = w_out.shape[0]
    const = lambda shape: pl.BlockSpec(shape, lambda b, i: (0, 0), pipeline_mode=pl.Buffered(1))
    row = pl.BlockSpec((1, tm, d), lambda b, i: (b, i, 0))
    modv = lambda j: pl.BlockSpec((1, 1, d), lambda b, i: (b, 0, j))
    return pl.pallas_call(
        _ffn_kernel,
        grid=(bsz, seq // tm),
        in_specs=[row, modv(3), modv(4), modv(5), const((1, d)), const((d, 2 * dff)),
                  const((dff, d)), const((1, d))],
        out_specs=row,
        out_shape=jax.ShapeDtypeStruct((bsz, seq, d), F32),
        compiler_params=pltpu.CompilerParams(
            dimension_semantics=("arbitrary", "arbitrary"), vmem_limit_bytes=VMEM_LIMIT),
        name="ffn",
    )(x, mod3, mod3, mod3, g_pre, w_in, w_out, g_post)


def _rope_tables(seq):
    half = HEAD_DIM // 2
    inv_freq = ROPE_THETA ** (-jnp.arange(half, dtype=F32) / half)
    ang = jnp.arange(seq, dtype=F32)[:, None] * inv_freq[None, :]
    cos, sin = jnp.cos(ang), jnp.sin(ang)
    reps = LANES // HEAD_DIM
    cos_t = jnp.tile(jnp.concatenate([cos, cos], axis=-1), (1, reps))
    sin_t = jnp.tile(jnp.concatenate([-sin, sin], axis=-1), (1, reps))
    return cos_t, sin_t


def _layer(x, c, w_ada, b_ada, g_pre_mix, g_post_mix, w_in, mu_shift, w0, w2, a0, a2, g2,
           k_k, k_a, r_k, lnx_w, lnx_b, w_a, w_b, w_out, g_pre_ffn, g_post_ffn, w_ffn_in, w_ffn_out):
    bsz, seq, d = x.shape
    rw = RWKV_WIDTH
    row1 = lambda t: t.reshape(1, -1)
    mod3 = _mod(c, w_ada, b_ada).reshape(bsz, 1, 6 * d)

    w_rwkv = _bf(w_in[:, :SHIFT_WIDTH])
    w_qkv = _bf(w_in[:, SHIFT_WIDTH:SHIFT_WIDTH + 3 * ATTN_WIDTH])
    w_gates = _bf(w_in[:, SHIFT_WIDTH + 3 * ATTN_WIDTH:])
    w_lora = jnp.zeros((LORA_WIDTH, 3 * rw), F32)
    w_lora = w_lora.at[:DECAY_LORA, :rw].set(w2)
    w_lora = w_lora.at[DECAY_LORA:DECAY_LORA + ICLR_LORA, rw:2 * rw].set(a2)
    w_lora = _bf(w_lora.at[DECAY_LORA + ICLR_LORA:, 2 * rw:].set(g2))

    prep = _rwkv_prep(x, mod3, row1(g_pre_mix), w_rwkv, row1(mu_shift), w_lora, row1(w0), row1(a0),
                      row1(k_k), row1(k_a), row1(r_k), tm=512)
    aq, rq, bk, kd, bg, kg, v, gc, bonus, gate = prep
    y = _rwkv_scan(aq, rq, bk, kd, bg, kg, v, gc.reshape(bsz, seq // CHUNK, rw),
                   rows_per_step=4, chunks_per_step=1)

    cos_t, sin_t = _rope_tables(seq)
    qkv = _attn_proj(x, mod3, row1(g_pre_mix), w_qkv, cos_t, sin_t, tm=512)
    o_att = _attn(qkv, seq)

    x1 = _merge(x, mod3, row1(g_pre_mix), w_gates, y, bonus, gate, row1(lnx_w), row1(lnx_b),
                _bf(w_a), o_att, _bf(w_b), _bf(w_out), row1(g_post_mix), tm=512)
    return _ffn(x1, mod3, row1(g_pre_ffn), _bf(w_ffn_in), _bf(w_ffn_out), row1(g_post_ffn), tm=512)


def kernel(x, c, w_ada, b_ada, g_pre_mix, g_post_mix, w_in, mu_shift, w0, w2, a0, a2, g2, k_k, k_a, r_k,
           lnx_w, lnx_b, w_a, w_b, w_out, g_pre_ffn, g_post_ffn, w_ffn_in, w_ffn_out):
    for i in range(w_in.shape[0]):
        x = _layer(x, c, w_ada[i], b_ada[i], g_pre_mix[i], g_post_mix[i], w_in[i], mu_shift[i], w0[i],
                   w2[i], a0[i], a2[i], g2[i], k_k[i], k_a[i], r_k[i], lnx_w[i], lnx_b[i], w_a[i], w_b[i],
                   w_out[i], g_pre_ffn[i], g_post_ffn[i], w_ffn_in[i], w_ffn_out[i])
    return x
```

```python
import functools
import math

import jax
import jax.numpy as jnp
from jax import lax
from jax.experimental import pallas as pl
from jax.experimental.pallas import tpu as pltpu

D_MODEL = 1024
HEAD_DIM = 64
RWKV_HEADS = 12
RWKV_WIDTH = RWKV_HEADS * HEAD_DIM
DECAY_LORA = 64
ICLR_LORA = 64
GATE_LORA = 128
LORA_WIDTH = DECAY_LORA + ICLR_LORA + GATE_LORA
ATTN_GROUPS = ((128, 1), (512, 4), (2048, 16))
HEADS_PER_GROUP = 4
GROUP_WIDTH = HEADS_PER_GROUP * HEAD_DIM
ATTN_WIDTH = len(ATTN_GROUPS) * GROUP_WIDTH
ATTN_BLOCK = 128
ROPE_THETA = 10000.0
SHIFT_WIDTH = 3 * RWKV_WIDTH + LORA_WIDTH
NORM_EPS = 1e-6
GN_EPS = 64e-5

CHUNK = 64
INV_BASE = 16
LANES = 128
NEG_BIG = -1e30
LOG2E = math.log2(math.e)
ATTN_BLOCKS_PER_ITER = 8
SUB_ROWS = 256
VMEM_LIMIT = 56 * 1024 * 1024

F32 = jnp.float32
BF16 = jnp.bfloat16


def _bf(x):
    return x.astype(BF16)


def _div(x, n):
    assert n & (n - 1) == 0
    return jnp.right_shift(x, n.bit_length() - 1)


def _rem(x, n):
    assert n & (n - 1) == 0
    return jnp.bitwise_and(x, n - 1)


def _sigmoid(x):
    return 1.0 / (1.0 + jnp.exp2(x * (-LOG2E)))


def _split_hi_lo(x):
    hi = _bf(x)
    lo = _bf(x - hi.astype(F32))
    return hi, lo


def _mm_exact_lhs(m_bf, x):
    hi, lo = _split_hi_lo(x)
    return (jnp.dot(m_bf, hi, preferred_element_type=F32)
            + jnp.dot(m_bf, lo, preferred_element_type=F32))


def _head_sum(x):
    tile = 2 * LANES
    ri = _div(lax.broadcasted_iota(jnp.int32, (tile, tile), 0), HEAD_DIM)
    ci = _div(lax.broadcasted_iota(jnp.int32, (tile, tile), 1), HEAD_DIM)
    ones_bd = jnp.where(ri == ci, 1.0, 0.0).astype(BF16)
    xb = _bf(x)
    parts = [jnp.dot(xb[:, j * tile:(j + 1) * tile], ones_bd, preferred_element_type=F32)
             for j in range(x.shape[-1] // tile)]
    return jnp.concatenate(parts, axis=-1)


def _norm_modulate(x, g, shift, scale):
    y = x * lax.rsqrt(jnp.mean(x * x, axis=-1, keepdims=True) + NORM_EPS)
    return y * (g * (1.0 + scale)) + shift


def _rms(x, g):
    return x * lax.rsqrt(jnp.mean(x * x, axis=-1, keepdims=True) + NORM_EPS) * g


def _mod_kernel(c_ref, w_ref, b_ref, o_ref):
    c = c_ref[...]
    o_ref[...] = jnp.dot(c * _sigmoid(c), w_ref[...], preferred_element_type=F32,
                         precision=lax.Precision.HIGHEST) + b_ref[...]


def _mod(c, w_ada, b_ada):
    bsz, d = c.shape
    n = w_ada.shape[1]
    tn = d
    return pl.pallas_call(
        _mod_kernel,
        grid=(n // tn,),
        in_specs=[pl.BlockSpec((bsz, d), lambda j: (0, 0)),
                  pl.BlockSpec((d, tn), lambda j: (0, j)),
                  pl.BlockSpec((1, tn), lambda j: (0, j))],
        out_specs=pl.BlockSpec((bsz, tn), lambda j: (0, j)),
        out_shape=jax.ShapeDtypeStruct((bsz, n), F32),
        name="mod",
    )(c, w_ada, b_ada.reshape(1, n))


def _rwkv_prep_kernel(x_ref, shift_ref, scale_ref, g_ref, w_ref, mu_ref, wl_ref, w0_ref, a0_ref,
                      kk_ref, ka_ref, rk_ref,
                      r_o, k_o, v_o, nkk_o, b_o, cum_o, bonus_o, gate_o,
                      carry_ref):
    i = pl.program_id(1)
    rw = RWKV_WIDTH
    sub = SUB_ROWS

    @pl.when(i == 0)
    def _():
        carry_ref[...] = jnp.zeros_like(carry_ref)

    ri = lax.broadcasted_iota(jnp.int32, (sub, sub), 0)
    ci = lax.broadcasted_iota(jnp.int32, (sub, sub), 1)
    same = _div(ri, CHUNK) == _div(ci, CHUNK)
    lower = jnp.where(same & (ci <= ri), 1.0, 0.0).astype(BF16)
    row0 = lax.broadcasted_iota(jnp.int32, (8, SHIFT_WIDTH), 0) == 0
    lane = lax.broadcasted_iota(jnp.int32, (sub, LORA_WIDTH), 1)

    prev_row = carry_ref[7:8, :]
    for sb in range(x_ref.shape[1] // sub):
        rows = slice(sb * sub, (sb + 1) * sub)
        h = _norm_modulate(x_ref[0, rows, :], g_ref[...], shift_ref[0], scale_ref[0])
        p = jnp.dot(_bf(h), w_ref[...], preferred_element_type=F32)
        rolled = pltpu.roll(p, 1, axis=0)
        shifted = jnp.concatenate([jnp.where(row0, prev_row, rolled[:8]), rolled[8:]], axis=0)
        last8 = p[sub - 8:, :]
        prev_row = last8[7:8, :]
        ps = p + mu_ref[...] * (shifted - p)

        pr, pk, pv = ps[:, :rw], ps[:, rw:2 * rw], ps[:, 2 * rw:3 * rw]
        lora_in = ps[:, 3 * rw:]
        z = jnp.where(lane < DECAY_LORA, jnp.tanh(lora_in),
                      jnp.where(lane < DECAY_LORA + ICLR_LORA, lora_in, _sigmoid(lora_in)))
        lo = jnp.dot(_bf(z), wl_ref[...], preferred_element_type=F32)
        wf = w0_ref[...] + lo[:, :rw]
        a = _sigmoid(a0_ref[...] + lo[:, rw:2 * rw])
        gate_o[0, rows, :] = lo[:, 2 * rw:]

        lw = (-math.exp(-0.5) * LOG2E) * _sigmoid(wf)

        kkr = pk * kk_ref[...]
        kkn = kkr * (1.0 / jnp.maximum(jnp.sqrt(_head_sum(kkr * kkr)), 1e-12))
        k = pk * (1.0 + (a - 1.0) * ka_ref[...])
        b = kkn * a
        bonus_o[0, rows, :] = _head_sum(pr * k * rk_ref[...]) * pv
        v_o[0, rows, :] = _bf(pv)

        cum_o[0, rows, :] = _mm_exact_lhs(lower, lw)
        r_o[0, rows, :] = _bf(pr)
        k_o[0, rows, :] = _bf(k)
        nkk_o[0, rows, :] = _bf(-kkn)
        b_o[0, rows, :] = _bf(b)
    carry_ref[...] = last8


def _rwkv_prep(x, mod3, g_pre, w_rwkv, mu, w_lora, w0, a0, k_k, k_a, r_k, tm):
    bsz, seq, d = x.shape
    rw = RWKV_WIDTH
    row_spec = pl.BlockSpec((1, tm, rw), lambda b, i: (b, i, 0))
    vec = lambda n: pl.BlockSpec((1, n), lambda b, i: (0, 0))
    big = jax.ShapeDtypeStruct((bsz, seq, rw), F32)
    half = jax.ShapeDtypeStruct((bsz, seq, rw), BF16)
    return pl.pallas_call(
        _rwkv_prep_kernel,
        grid=(bsz, seq // tm),
        in_specs=[pl.BlockSpec((1, tm, d), lambda b, i: (b, i, 0)),
                  pl.BlockSpec((1, 1, d), lambda b, i: (b, 0, 0)),
                  pl.BlockSpec((1, 1, d), lambda b, i: (b, 0, 1)),
                  vec(d),
                  pl.BlockSpec((d, SHIFT_WIDTH), lambda b, i: (0, 0), pipeline_mode=pl.Buffered(1)),
                  vec(SHIFT_WIDTH),
                  pl.BlockSpec((LORA_WIDTH, 3 * rw), lambda b, i: (0, 0), pipeline_mode=pl.Buffered(1)),
                  vec(rw), vec(rw), vec(rw), vec(rw), vec(rw)],
        out_specs=[row_spec] * 8,
        out_shape=[half] * 5 + [big] * 3,
        scratch_shapes=[pltpu.VMEM((8, SHIFT_WIDTH), F32)],
        compiler_params=pltpu.CompilerParams(
            dimension_semantics=("arbitrary", "arbitrary"), vmem_limit_bytes=VMEM_LIMIT),
        name="rwkv_prep",
    )(x, mod3, mod3, g_pre, w_rwkv, mu, w_lora, w0, a0, k_k, k_a, r_k)


def _pair_blockdiag(x):
    first = lax.broadcasted_iota(jnp.int32, x.shape, 1) < HEAD_DIM
    zero = jnp.zeros_like(x)
    return jnp.concatenate([jnp.where(first, x, zero), jnp.where(first, zero, x)], axis=0)


def _pmm(a, b):
    return jnp.dot(_bf(a), _pair_blockdiag(_bf(b)), preferred_element_type=F32)


def _pmm_nt(a, b):
    return lax.dot_general(_bf(a), _pair_blockdiag(_bf(b)), (((1,), (1,)), ((), ())),
                           preferred_element_type=F32)


def _pmm_tn(a, b):
    full = lax.dot_general(_bf(a), _bf(b), (((0,), (0,)), ((), ())), preferred_element_type=F32)
    first = lax.broadcasted_iota(jnp.int32, (HEAD_DIM, LANES), 1) < HEAD_DIM
    return jnp.where(first, full[:HEAD_DIM], full[HEAD_DIM:])


def _unit_lower_inverse(a_list, masks):
    eye, diag_mask, merge_masks = masks
    ps = [jnp.where(diag_mask, a, 0.0) for a in a_list]
    ts = [eye + p for p in ps]
    ps = [_pmm(p, p) for p in ps]
    m = 2
    while 2 * m < INV_BASE:
        both = [_pmm(jnp.concatenate([p, t], axis=0), p) for p, t in zip(ps, ts)]
        ps = [x[:CHUNK] for x in both]
        ts = [t + x[CHUNK:] for t, x in zip(ts, both)]
        m *= 2
    ts = [t + _pmm(t, p) for t, p in zip(ts, ps)]
    for mk in merge_masks:
        ws = [_pmm(jnp.where(mk, a, 0.0), t) for a, t in zip(a_list, ts)]
        ts = [t + _pmm(t, w) for t, w in zip(ts, ws)]
    return ts


def _rwkv_scan_kernel(r_ref, k_ref, v_ref, nkk_ref, b_ref, cum_ref, y_ref, st_ref):
    @pl.when(pl.program_id(1) == 0)
    def _():
        st_ref[...] = jnp.zeros_like(st_ref)

    ri = lax.broadcasted_iota(jnp.int32, (CHUNK, LANES), 0)
    ci = _rem(lax.broadcasted_iota(jnp.int32, (CHUNK, LANES), 1), HEAD_DIM)
    strict = ri > ci
    incl = ri >= ci
    ri2 = lax.broadcasted_iota(jnp.int32, (2 * CHUNK, LANES), 0)
    ci2 = _rem(lax.broadcasted_iota(jnp.int32, (2 * CHUNK, LANES), 1), HEAD_DIM)
    rloc = _rem(ri2, CHUNK)
    stacked = (rloc > ci2) | ((rloc == ci2) & (ri2 >= CHUNK))
    eye = jnp.where(ri == ci, 1.0, 0.0)
    diag_mask = _div(ri, INV_BASE) == _div(ci, INV_BASE)
    merge_masks = []
    bsize = INV_BASE
    while bsize < CHUNK:
        merge_masks.append((_div(ri, 2 * bsize) == _div(ci, 2 * bsize))
                           & (_div(ri, bsize) != _div(ci, bsize)))
        bsize *= 2
    masks = (eye, diag_mask, merge_masks)

    n_rows = r_ref.shape[0]
    n_chunks = r_ref.shape[1] // CHUNK
    n_pairs = RWKV_WIDTH // LANES
    seqs = [(b, j) for b in range(n_rows) for j in range(n_pairs)]
    items = [(c, s) for c in range(n_chunks) for s in seqs]

    def tile(ref, c, s):
        b, j = s
        return ref[b, c * CHUNK:(c + 1) * CHUNK, j * LANES:(j + 1) * LANES]

    first_row = ri == 0
    qs, bk, kd, bkg, gcs = {}, {}, {}, {}, {}
    for it in items:
        cum = tile(cum_ref, *it)
        cum_last = cum[CHUNK - 1:CHUNK, :]
        e_cum = jnp.exp2(cum)
        e_prev = jnp.exp2(jnp.where(first_row, 0.0, pltpu.roll(cum, 1, axis=0)))
        e_inv = jnp.exp2(-cum)
        e_rest = jnp.exp2(cum_last - cum)
        gcs[it] = jnp.exp2(cum_last)
        r, k = tile(r_ref, *it).astype(F32), tile(k_ref, *it).astype(F32)
        nkk, b = tile(nkk_ref, *it).astype(F32), tile(b_ref, *it).astype(F32)
        qs[it] = _bf(jnp.concatenate([nkk * e_prev, r * e_cum], axis=0))
        bk[it], kd[it] = _bf(b * e_inv), _bf(k * e_inv)
        bkg[it] = _bf(jnp.concatenate([b * e_rest, k * e_rest], axis=0))

    nt = lambda a, b: lax.dot_general(a, b, (((1,), (1,)), ((), ())), preferred_element_type=F32)
    abk = {it: nt(qs[it], jnp.concatenate([_pair_blockdiag(bk[it]), _pair_blockdiag(kd[it])], axis=0))
           for it in items}
    ab = {it: abk[it][:, :LANES] for it in items}
    ak = {it: jnp.where(stacked, abk[it][:, LANES:], 0.0) for it in items}
    akv = {it: _pmm(ak[it], tile(v_ref, *it)) for it in items}
    t_list = _unit_lower_inverse([jnp.where(strict, ab[it][:CHUNK], 0.0) for it in items], masks)
    t = dict(zip(items, t_list))

    state = {s: st_ref[s[0], s[1]] for s in seqs}
    for c in range(n_chunks):
        xs = {s: _pmm_nt(qs[(c, s)], state[s]) for s in seqs}
        u = {s: _pmm(t[(c, s)], xs[s][:CHUNK] + akv[(c, s)][:CHUNK]) for s in seqs}
        for s in seqs:
            b, j = s
            y = xs[s][CHUNK:] + akv[(c, s)][CHUNK:] + _pmm(
                jnp.where(incl, ab[(c, s)][CHUNK:], 0.0), u[s])
            y_ref[b, c * CHUNK:(c + 1) * CHUNK, j * LANES:(j + 1) * LANES] = y
        new_state = {}
        for s in seqs:
            uv = jnp.concatenate([_bf(u[s]), tile(v_ref, c, s)], axis=0)
            new_state[s] = state[s] * gcs[(c, s)] + _pmm_tn(uv, bkg[(c, s)])
        state = new_state
    for s in seqs:
        st_ref[s[0], s[1]] = state[s]


def _rwkv_scan(r, k, v, nkk, b, cum, rows_per_step, chunks_per_step):
    bsz, seq, rw = r.shape
    ts = chunks_per_step * CHUNK
    row_spec = pl.BlockSpec((rows_per_step, ts, rw), lambda b, c: (b, c, 0))
    return pl.pallas_call(
        _rwkv_scan_kernel,
        grid=(bsz // rows_per_step, seq // ts),
        in_specs=[row_spec] * 6,
        out_specs=row_spec,
        out_shape=jax.ShapeDtypeStruct((bsz, seq, rw), F32),
        scratch_shapes=[pltpu.VMEM((rows_per_step, rw // LANES, HEAD_DIM, LANES), F32)],
        compiler_params=pltpu.CompilerParams(dimension_semantics=("arbitrary", "arbitrary")),
        name="rwkv_scan",
    )(r, k, v, nkk, b, cum)


def _attn_proj_kernel(x_ref, shift_ref, scale_ref, g_ref, w_ref, cos_ref, sin_ref, *rest):
    outs, buf = rest[:-1], rest[-1]
    aw = ATTN_WIDTH
    reps = aw // LANES
    lane = lax.broadcasted_iota(jnp.int32, (SUB_ROWS, aw), 1)
    first_half = _rem(lane, HEAD_DIM) < (HEAD_DIM // 2)
    tiles_per_group = GROUP_WIDTH // LANES

    for sb in range(x_ref.shape[1] // SUB_ROWS):
        rows = slice(sb * SUB_ROWS, (sb + 1) * SUB_ROWS)
        h = _norm_modulate(x_ref[0, rows, :], g_ref[...], shift_ref[0], scale_ref[0])
        p = jnp.dot(_bf(h), w_ref[...], preferred_element_type=F32)
        cos = jnp.concatenate([cos_ref[rows, :]] * reps, axis=-1)
        sin = jnp.concatenate([sin_ref[rows, :]] * reps, axis=-1)

        def rope(t, cos=cos, sin=sin):
            partner = jnp.where(first_half, pltpu.roll(t, aw - HEAD_DIM // 2, axis=1),
                                pltpu.roll(t, HEAD_DIM // 2, axis=1))
            return t * cos + partner * sin

        roped = (rope(p[:, :aw]) * (HEAD_DIM ** -0.5), rope(p[:, aw:2 * aw]), p[:, 2 * aw:])
        n = 0
        for which in range(3):
            for g, (_, dil) in enumerate(ATTN_GROUPS):
                o_ref = outs[n]
                n += 1
                out_rows = slice(sb * SUB_ROWS // dil, (sb + 1) * SUB_ROWS // dil)
                for ct in range(tiles_per_group):
                    c0 = g * GROUP_WIDTH + ct * LANES
                    val = roped[which][:, c0:c0 + LANES]
                    if dil == 1:
                        o_ref[0, 0, out_rows, ct * LANES:(ct + 1) * LANES] = _bf(val)
                        continue
                    slot = (which * len(ATTN_GROUPS) + g) * tiles_per_group + ct
                    buf[sb, slot] = val
                    for r in range(dil):
                        o_ref[0, r, out_rows, ct * LANES:(ct + 1) * LANES] = _bf(
                            buf[sb, slot, pl.ds(r, SUB_ROWS // dil, stride=dil), :])


def _attn_proj(x, mod3, g_pre, w_qkv, cos_t, sin_t, tm):
    bsz, seq, d = x.shape
    out_specs, out_shape = [], []
    for _ in range(3):
        for _, dil in ATTN_GROUPS:
            out_specs.append(pl.BlockSpec((1, dil, tm // dil, GROUP_WIDTH), lambda b, i: (b, 0, i, 0)))
            out_shape.append(jax.ShapeDtypeStruct((bsz, dil, seq // dil, GROUP_WIDTH), BF16))
    return pl.pallas_call(
        _attn_proj_kernel,
        grid=(bsz, seq // tm),
        in_specs=[pl.BlockSpec((1, tm, d), lambda b, i: (b, i, 0)),
                  pl.BlockSpec((1, 1, d), lambda b, i: (b, 0, 0)),
                  pl.BlockSpec((1, 1, d), lambda b, i: (b, 0, 1)),
                  pl.BlockSpec((1, d), lambda b, i: (0, 0)),
                  pl.BlockSpec((d, 3 * ATTN_WIDTH), lambda b, i: (0, 0), pipeline_mode=pl.Buffered(1)),
                  pl.BlockSpec((tm, LANES), lambda b, i: (i, 0)),
                  pl.BlockSpec((tm, LANES), lambda b, i: (i, 0))],
        out_specs=out_specs,
        out_shape=out_shape,
        scratch_shapes=[pltpu.VMEM((tm // SUB_ROWS, 3 * ATTN_WIDTH // LANES, SUB_ROWS, LANES), F32)],
        compiler_params=pltpu.CompilerParams(
            dimension_semantics=("arbitrary", "arbitrary"), vmem_limit_bytes=VMEM_LIMIT),
        name="attn_proj",
    )(x, mod3, mod3, g_pre, w_qkv, cos_t, sin_t)


def _attn_kernel(*refs):
    qkv = refs[:9]
    o_ref = refs[9]
    num_ref, m_ref, den_ref = refs[10:]
    blk = ATTN_BLOCK
    qi = lax.broadcasted_iota(jnp.int32, (blk, blk), 0)
    kj = lax.broadcasted_iota(jnp.int32, (blk, blk), 1)
    bias_cur = jnp.where(kj <= qi, 0.0, NEG_BIG)
    bias_prev = jnp.where(kj >= qi, 0.0, NEG_BIG)
    lane = lax.broadcasted_iota(jnp.int32, (blk, LANES), 1)
    head0 = lane < HEAD_DIM

    for g, (_, dil) in enumerate(ATTN_GROUPS):
        q_ref, k_ref, v_ref = qkv[g], qkv[3 + g], qkv[6 + g]
        nb = q_ref.shape[2] // blk

        nper = min(nb, ATTN_BLOCKS_PER_ITER)
        rper = ATTN_BLOCKS_PER_ITER // nper
        whole = nper == nb

        def body(idx, carry, q_ref=q_ref, k_ref=k_ref, v_ref=v_ref, dil=dil, nb=nb, g=g,
                 nper=nper, rper=rper, whole=whole):
            if whole:
                r0, n0 = idx * rper, 0
                base = 0
            else:
                r0 = _div(idx, nb // nper)
                n0 = _rem(idx, nb // nper) * nper
                base = pl.multiple_of(n0 * blk, blk)
            nt = lambda a, b: lax.dot_general(a, b, (((1,), (1,)), ((), ())), preferred_element_type=F32)
            one = jnp.ones((blk, LANES), BF16)
            zero = jnp.zeros((blk, LANES), BF16)
            q_msk, ks, v_aug = {}, {}, {}
            for rr in range(rper):
                r = r0 + rr
                first = 0 if whole else -1
                for i in range(first, nper):
                    if i < 0:
                        st = pl.multiple_of(jnp.maximum(n0 - 1, 0) * blk, blk)
                    else:
                        st = base + i * blk
                    ks[(rr, i)] = k_ref[0, r, pl.ds(st, blk), :]
                    v = v_ref[0, r, pl.ds(st, blk), :]
                    v_aug[(rr, i)] = (jnp.where(head0, v, one), jnp.where(head0, one, v))
                    if i >= 0:
                        q = q_ref[0, r, pl.ds(st, blk), :]
                        q_msk[(rr, i)] = (jnp.where(head0, q, zero), jnp.where(head0, zero, q))
            items = [(rr, i, e) for rr in range(rper) for i in range(nper) for e in range(2)]
            has_prev = lambda it: (it[0], it[1] - 1) in ks
            s_c = {it: nt(q_msk[it[:2]][it[2]], ks[it[:2]]) for it in items}
            s_p = {it: nt(q_msk[it[:2]][it[2]], ks[(it[0], it[1] - 1)]) for it in items if has_prev(it)}
            bias_first = bias_prev if whole else jnp.where(n0 > 0, bias_prev, NEG_BIG)
            ms, pcs, pps = {}, {}, {}
            for it in items:
                sc = s_c[it] + bias_cur
                if has_prev(it):
                    sp = s_p[it] + (bias_first if it[1] == 0 else bias_prev)
                    m = jnp.max(jnp.maximum(sc, sp), axis=-1, keepdims=True)
                    pps[it] = _bf(jnp.exp(sp - m))
                else:
                    m = jnp.max(sc, axis=-1, keepdims=True)
                ms[it] = m
                pcs[it] = _bf(jnp.exp(sc - m))
            pv = {}
            for it in items:
                acc = jnp.dot(pcs[it], v_aug[it[:2]][it[2]], preferred_element_type=F32)
                if has_prev(it):
                    acc = acc + jnp.dot(pps[it], v_aug[(it[0], it[1] - 1)][it[2]],
                                        preferred_element_type=F32)
                pv[it] = acc
            for rr in range(rper):
                for i in range(nper):
                    m_b = jnp.where(head0, ms[(rr, i, 0)], ms[(rr, i, 1)])
                    num_b = jnp.where(head0, pv[(rr, i, 0)], pv[(rr, i, 1)])
                    l_b = pltpu.roll(jnp.where(head0, pv[(rr, i, 1)], pv[(rr, i, 0)]), HEAD_DIM, axis=1)
                    if g == 0:
                        rows = pl.ds(base + i * blk, blk)
                        num_ref[rows, :] = num_b
                        m_ref[rows, :] = m_b
                        den_ref[rows, :] = l_b
                    else:
                        rows = pl.ds((n0 + i) * (blk * dil) + r0 + rr, blk, stride=dil)
                        m_old = m_ref[rows, :]
                        m_new = jnp.maximum(m_old, m_b)
                        w_old = jnp.exp(m_old - m_new)
                        w_b = jnp.exp(m_b - m_new)
                        num_ref[rows, :] = num_ref[rows, :] * w_old + num_b * w_b
                        den_ref[rows, :] = den_ref[rows, :] * w_old + l_b * w_b
                        m_ref[rows, :] = m_new
            return carry

        lax.fori_loop(0, dil * nb // (nper * rper), body, 0)

    o_ref[0] = _bf(num_ref[...] / den_ref[...])


def _attn(qkv, seq):
    bsz = qkv[0].shape[0]
    in_specs = []
    for _ in range(3):
        for _, dil in ATTN_GROUPS:
            in_specs.append(pl.BlockSpec((1, dil, seq // dil, LANES), lambda b, sp: (b, 0, 0, sp)))
    return pl.pallas_call(
        _attn_kernel,
        grid=(bsz, GROUP_WIDTH // LANES),
        in_specs=in_specs,
        out_specs=pl.BlockSpec((1, seq, LANES), lambda b, sp: (b, 0, sp)),
        out_shape=jax.ShapeDtypeStruct((bsz, seq, GROUP_WIDTH), BF16),
        scratch_shapes=[pltpu.VMEM((seq, LANES), F32)] * 3,
        compiler_params=pltpu.CompilerParams(
            dimension_semantics=("arbitrary", "arbitrary"), vmem_limit_bytes=VMEM_LIMIT),
        name="attn",
    )(*qkv)


def _merge_kernel(x_ref, shift_ref, scale_ref, gatem_ref, g_ref, wg_ref, y_ref, bonus_ref, gate_ref,
                  lnw_ref, lnb_ref, wa_ref, o_ref, wb_ref, wout_ref, gpost_ref, out_ref):
    inv_n = 1.0 / HEAD_DIM
    for r0 in range(0, x_ref.shape[1], SUB_ROWS):
        rows = slice(r0, r0 + SUB_ROWS)
        x = x_ref[0, rows, :]
        h = _norm_modulate(x, g_ref[...], shift_ref[0], scale_ref[0])
        pg = jnp.dot(_bf(h), wg_ref[...], preferred_element_type=F32)
        y = y_ref[0, rows, :]
        yc = y - _head_sum(y) * inv_n
        var = _head_sum(yc * yc) * inv_n
        yn = yc * lax.rsqrt(var + GN_EPS) * lnw_ref[...] + lnb_ref[...]
        ya = jnp.dot(_bf((yn + bonus_ref[0, rows, :]) * gate_ref[0, rows, :]), wa_ref[...],
                     preferred_element_type=F32)
        yb = jnp.dot(o_ref[0, rows, :], wb_ref[...], preferred_element_type=F32)
        merged = _sigmoid(pg[:, :D_MODEL]) * ya + _sigmoid(pg[:, D_MODEL:]) * yb
        zz = jnp.dot(_bf(merged), wout_ref[...], preferred_element_type=F32)
        out_ref[0, rows, :] = x + gatem_ref[0] * _rms(zz, gpost_ref[...])


def _merge(x, mod3, g_pre, w_g, y, bonus, gate, lnx_w, lnx_b, w_a, o_att, w_b, w_out, g_post, tm):
    bsz, seq, d = x.shape
    rw = RWKV_WIDTH
    const = lambda shape: pl.BlockSpec(shape, lambda b, i: (0, 0), pipeline_mode=pl.Buffered(1))
    row = lambda n: pl.BlockSpec((1, tm, n), lambda b, i: (b, i, 0))
    modv = lambda j: pl.BlockSpec((1, 1, d), lambda b, i: (b, 0, j))
    return pl.pallas_call(
        _merge_kernel,
        grid=(bsz, seq // tm),
        in_specs=[row(d), modv(0), modv(1), modv(2), const((1, d)), const((d, 2 * d)),
                  row(rw), row(rw), row(rw), const((1, rw)), const((1, rw)), const((rw, d)),
                  row(GROUP_WIDTH), const((GROUP_WIDTH, d)), const((d, d)), const((1, d))],
        out_specs=row(d),
        out_shape=jax.ShapeDtypeStruct((bsz, seq, d), F32),
        compiler_params=pltpu.CompilerParams(
            dimension_semantics=("arbitrary", "arbitrary"), vmem_limit_bytes=VMEM_LIMIT),
        name="merge",
    )(x, mod3, mod3, mod3, g_pre, w_g, y, bonus, gate, lnx_w, lnx_b, w_a, o_att, w_b, w_out, g_post)


def _ffn_kernel(x_ref, shift_ref, scale_ref, gatef_ref, g_ref, win_ref, wout_ref, gpost_ref, out_ref):
    dff = wout_ref.shape[0]
    for r0 in range(0, x_ref.shape[1], SUB_ROWS):
        rows = slice(r0, r0 + SUB_ROWS)
        x = x_ref[0, rows, :]
        h = _bf(_norm_modulate(x, g_ref[...], shift_ref[0], scale_ref[0]))
        u = jnp.dot(h, win_ref[:, :dff], preferred_element_type=F32)
        gt = jnp.dot(h, win_ref[:, dff:], preferred_element_type=F32)
        act = _bf(u * _sigmoid(u) * gt)
        f = jnp.dot(act, wout_ref[...], preferred_element_type=F32)
        out_ref[0, rows, :] = x + gatef_ref[0] * _rms(f, gpost_ref[...])


def _ffn(x, mod3, g_pre, w_in, w_out, g_post, tm):
    bsz, seq, d = x.shape
    dff = w_out.shape[0]
    const = lambda shape: pl.BlockSpec(shape, lambda b, i: (0, 0), pipeline_mode=pl.Buffered(1))
    row = pl.BlockSpec((1, tm, d), lambda b, i: (b, i, 0))
    modv = lambda j: pl.BlockSpec((1, 1, d), lambda b, i: (b, 0, j))
    return pl.pallas_call(
        _ffn_kernel,
        grid=(bsz, seq // tm),
        in_specs=[row, modv(3), modv(4), modv(5), const((1, d)), const((d, 2 * dff)),
                  const((dff, d)), const((1, d))],
        out_specs=row,
        out_shape=jax.ShapeDtypeStruct((bsz, seq, d), F32),
        compiler_params=pltpu.CompilerParams(
            dimension_semantics=("arbitrary", "arbitrary"), vmem_limit_bytes=VMEM_LIMIT),
        name="ffn",
    )(x, mod3, mod3, mod3, g_pre, w_in, w_out, g_post)


def _rope_tables(seq):
    half = HEAD_DIM // 2
    inv_freq = ROPE_THETA ** (-jnp.arange(half, dtype=F32) / half)
    ang = jnp.arange(seq, dtype=F32)[:, None] * inv_freq[None, :]
    cos, sin = jnp.cos(ang), jnp.sin(ang)
    reps = LANES // HEAD_DIM
    cos_t = jnp.tile(jnp.concatenate([cos, cos], axis=-1), (1, reps))
    sin_t = jnp.tile(jnp.concatenate([-sin, sin], axis=-1), (1, reps))
    return cos_t, sin_t


def _layer(x, c, w_ada, b_ada, g_pre_mix, g_post_mix, w_in, mu_shift, w0, w2, a0, a2, g2,
           k_k, k_a, r_k, lnx_w, lnx_b, w_a, w_b, w_out, g_pre_ffn, g_post_ffn, w_ffn_in, w_ffn_out):
    bsz, seq, d = x.shape
    rw = RWKV_WIDTH
    row1 = lambda t: t.reshape(1, -1)
    mod3 = _mod(c, w_ada, b_ada).reshape(bsz, 1, 6 * d)

    w_rwkv = _bf(w_in[:, :SHIFT_WIDTH])
    w_qkv = _bf(w_in[:, SHIFT_WIDTH:SHIFT_WIDTH + 3 * ATTN_WIDTH])
    w_gates = _bf(w_in[:, SHIFT_WIDTH + 3 * ATTN_WIDTH:])
    w_lora = jnp.zeros((LORA_WIDTH, 3 * rw), F32)
    w_lora = w_lora.at[:DECAY_LORA, :rw].set(w2)
    w_lora = w_lora.at[DECAY_LORA:DECAY_LORA + ICLR_LORA, rw:2 * rw].set(a2)
    w_lora = _bf(w_lora.at[DECAY_LORA + ICLR_LORA:, 2 * rw:].set(g2))

    prep = _rwkv_prep(x, mod3, row1(g_pre_mix), w_rwkv, row1(mu_shift), w_lora, row1(w0), row1(a0),
                      row1(k_k), row1(k_a), row1(r_k), tm=512)
    r, k, v, nkk, b, cum, bonus, gate = prep
    y = _rwkv_scan(r, k, v, nkk, b, cum, rows_per_step=4, chunks_per_step=1)

    cos_t, sin_t = _rope_tables(seq)
    qkv = _attn_proj(x, mod3, row1(g_pre_mix), w_qkv, cos_t, sin_t, tm=512)
    o_att = _attn(qkv, seq)

    x1 = _merge(x, mod3, row1(g_pre_mix), w_gates, y, bonus, gate, row1(lnx_w), row1(lnx_b),
                _bf(w_a), o_att, _bf(w_b), _bf(w_out), row1(g_post_mix), tm=512)
    return _ffn(x1, mod3, row1(g_pre_ffn), _bf(w_ffn_in), _bf(w_ffn_out), row1(g_post_ffn), tm=512)


def kernel(x, c, w_ada, b_ada, g_pre_mix, g_post_mix, w_in, mu_shift, w0, w2, a0, a2, g2, k_k, k_a, r_k,
           lnx_w, lnx_b, w_a, w_b, w_out, g_pre_ffn, g_post_ffn, w_ffn_in, w_ffn_out):
    for i in range(w_in.shape[0]):
        x = _layer(x, c, w_ada[i], b_ada[i], g_pre_mix[i], g_post_mix[i], w_in[i], mu_shift[i], w0[i],
                   w2[i], a0[i], a2[i], g2[i], k_k[i], k_a[i], r_k[i], lnx_w[i], lnx_b[i], w_a[i], w_b[i],
                   w_out[i], g_pre_ffn[i], g_post_ffn[i], w_ffn_in[i], w_ffn_out[i])
    return x
```

```python
import functools
import math

import jax
import jax.numpy as jnp
from jax import lax
from jax.experimental import pallas as pl
from jax.experimental.pallas import tpu as pltpu

D_MODEL = 1024
HEAD_DIM = 64
RWKV_HEADS = 12
RWKV_WIDTH = RWKV_HEADS * HEAD_DIM
DECAY_LORA = 64
ICLR_LORA = 64
GATE_LORA = 128
LORA_WIDTH = DECAY_LORA + ICLR_LORA + GATE_LORA
ATTN_GROUPS = ((128, 1), (512, 4), (2048, 16))
HEADS_PER_GROUP = 4
GROUP_WIDTH = HEADS_PER_GROUP * HEAD_DIM
ATTN_WIDTH = len(ATTN_GROUPS) * GROUP_WIDTH
ATTN_BLOCK = 128
ROPE_THETA = 10000.0
SHIFT_WIDTH = 3 * RWKV_WIDTH + LORA_WIDTH
NORM_EPS = 1e-6
GN_EPS = 64e-5

CHUNK = 64
INV_BASE = 16
LANES = 128
NEG_BIG = -1e30
LOG2E = math.log2(math.e)
ATTN_BLOCKS_PER_ITER = 8
SUB_ROWS = 256
VMEM_LIMIT = 56 * 1024 * 1024

F32 = jnp.float32
BF16 = jnp.bfloat16


def _bf(x):
    return x.astype(BF16)


def _div(x, n):
    assert n & (n - 1) == 0
    return jnp.right_shift(x, n.bit_length() - 1)


def _rem(x, n):
    assert n & (n - 1) == 0
    return jnp.bitwise_and(x, n - 1)


def _sigmoid(x):
    return 1.0 / (1.0 + jnp.exp2(x * (-LOG2E)))


def _split_hi_lo(x):
    hi = _bf(x)
    lo = _bf(x - hi.astype(F32))
    return hi, lo


def _mm_exact_lhs(m_bf, x):
    hi, lo = _split_hi_lo(x)
    return (jnp.dot(m_bf, hi, preferred_element_type=F32)
            + jnp.dot(m_bf, lo, preferred_element_type=F32))


def _head_sum(x):
    tile = 2 * LANES
    ri = _div(lax.broadcasted_iota(jnp.int32, (tile, tile), 0), HEAD_DIM)
    ci = _div(lax.broadcasted_iota(jnp.int32, (tile, tile), 1), HEAD_DIM)
    ones_bd = jnp.where(ri == ci, 1.0, 0.0).astype(BF16)
    xb = _bf(x)
    parts = [jnp.dot(xb[:, j * tile:(j + 1) * tile], ones_bd, preferred_element_type=F32)
             for j in range(x.shape[-1] // tile)]
    return jnp.concatenate(parts, axis=-1)


def _norm_modulate(x, g, shift, scale):
    y = x * lax.rsqrt(jnp.mean(x * x, axis=-1, keepdims=True) + NORM_EPS)
    return y * (g * (1.0 + scale)) + shift


def _rms(x, g):
    return x * lax.rsqrt(jnp.mean(x * x, axis=-1, keepdims=True) + NORM_EPS) * g


def _mod_kernel(c_ref, w_ref, b_ref, o_ref):
    c = c_ref[...]
    o_ref[...] = jnp.dot(c * _sigmoid(c), w_ref[...], preferred_element_type=F32,
                         precision=lax.Precision.HIGHEST) + b_ref[...]


def _mod(c, w_ada, b_ada):
    bsz, d = c.shape
    n = w_ada.shape[1]
    tn = d
    return pl.pallas_call(
        _mod_kernel,
        grid=(n // tn,),
        in_specs=[pl.BlockSpec((bsz, d), lambda j: (0, 0)),
                  pl.BlockSpec((d, tn), lambda j: (0, j)),
                  pl.BlockSpec((1, tn), lambda j: (0, j))],
        out_specs=pl.BlockSpec((bsz, tn), lambda j: (0, j)),
        out_shape=jax.ShapeDtypeStruct((bsz, n), F32),
        name="mod",
    )(c, w_ada, b_ada.reshape(1, n))


def _rwkv_prep_kernel(x_ref, shift_ref, scale_ref, g_ref, w_ref, mu_ref, wl_ref, w0_ref, a0_ref,
                      kk_ref, ka_ref, rk_ref,
                      r_o, k_o, v_o, nkk_o, b_o, cum_o, bonus_o, gate_o,
                      carry_ref):
    i = pl.program_id(1)
    rw = RWKV_WIDTH
    sub = SUB_ROWS

    @pl.when(i == 0)
    def _():
        carry_ref[...] = jnp.zeros_like(carry_ref)

    ri = lax.broadcasted_iota(jnp.int32, (sub, sub), 0)
    ci = lax.broadcasted_iota(jnp.int32, (sub, sub), 1)
    same = _div(ri, CHUNK) == _div(ci, CHUNK)
    lower = jnp.where(same & (ci <= ri), 1.0, 0.0).astype(BF16)
    row0 = lax.broadcasted_iota(jnp.int32, (8, SHIFT_WIDTH), 0) == 0
    lane = lax.broadcasted_iota(jnp.int32, (sub, LORA_WIDTH), 1)

    subs = range(x_ref.shape[1] // sub)
    rows = [slice(sb * sub, (sb + 1) * sub) for sb in subs]
    h = [_bf(_norm_modulate(x_ref[0, rows[sb], :], g_ref[...], shift_ref[0], scale_ref[0])) for sb in subs]
    p = [jnp.dot(h[sb], w_ref[...], preferred_element_type=F32) for sb in subs]

    prev_row = carry_ref[7:8, :]
    ps = []
    for sb in subs:
        rolled = pltpu.roll(p[sb], 1, axis=0)
        shifted = jnp.concatenate([jnp.where(row0, prev_row, rolled[:8]), rolled[8:]], axis=0)
        last8 = p[sb][sub - 8:, :]
        prev_row = last8[7:8, :]
        ps.append(p[sb] + mu_ref[...] * (shifted - p[sb]))
    carry_ref[...] = last8

    z = []
    for sb in subs:
        lora_in = ps[sb][:, 3 * rw:]
        z.append(_bf(jnp.where(lane < DECAY_LORA, jnp.tanh(lora_in),
                               jnp.where(lane < DECAY_LORA + ICLR_LORA, lora_in, _sigmoid(lora_in)))))
    lo = [jnp.dot(z[sb], wl_ref[...], preferred_element_type=F32) for sb in subs]

    pr = [ps[sb][:, :rw] for sb in subs]
    pk = [ps[sb][:, rw:2 * rw] for sb in subs]
    pv = [ps[sb][:, 2 * rw:3 * rw] for sb in subs]
    kkr = [pk[sb] * kk_ref[...] for sb in subs]
    kk_sq = [_head_sum(kkr[sb] * kkr[sb]) for sb in subs]
    a = [_sigmoid(a0_ref[...] + lo[sb][:, rw:2 * rw]) for sb in subs]
    k = [pk[sb] * (1.0 + (a[sb] - 1.0) * ka_ref[...]) for sb in subs]
    rk_sum = [_head_sum(pr[sb] * k[sb] * rk_ref[...]) for sb in subs]
    lw = [(-math.exp(-0.5) * LOG2E) * _sigmoid(w0_ref[...] + lo[sb][:, :rw]) for sb in subs]
    cum = [_mm_exact_lhs(lower, lw[sb]) for sb in subs]

    for sb in subs:
        kkn = kkr[sb] * (1.0 / jnp.maximum(jnp.sqrt(kk_sq[sb]), 1e-12))
        gate_o[0, rows[sb], :] = lo[sb][:, 2 * rw:]
        bonus_o[0, rows[sb], :] = rk_sum[sb] * pv[sb]
        cum_o[0, rows[sb], :] = cum[sb]
        r_o[0, rows[sb], :] = _bf(pr[sb])
        k_o[0, rows[sb], :] = _bf(k[sb])
        v_o[0, rows[sb], :] = _bf(pv[sb])
        nkk_o[0, rows[sb], :] = _bf(-kkn)
        b_o[0, rows[sb], :] = _bf(kkn * a[sb])


def _rwkv_prep(x, mod3, g_pre, w_rwkv, mu, w_lora, w0, a0, k_k, k_a, r_k, tm):
    bsz, seq, d = x.shape
    rw = RWKV_WIDTH
    row_spec = pl.BlockSpec((1, tm, rw), lambda b, i: (b, i, 0))
    vec = lambda n: pl.BlockSpec((1, n), lambda b, i: (0, 0))
    big = jax.ShapeDtypeStruct((bsz, seq, rw), F32)
    half = jax.ShapeDtypeStruct((bsz, seq, rw), BF16)
    return pl.pallas_call(
        _rwkv_prep_kernel,
        grid=(bsz, seq // tm),
        in_specs=[pl.BlockSpec((1, tm, d), lambda b, i: (b, i, 0)),
                  pl.BlockSpec((1, 1, d), lambda b, i: (b, 0, 0)),
                  pl.BlockSpec((1, 1, d), lambda b, i: (b, 0, 1)),
                  vec(d),
                  pl.BlockSpec((d, SHIFT_WIDTH), lambda b, i: (0, 0), pipeline_mode=pl.Buffered(1)),
                  vec(SHIFT_WIDTH),
                  pl.BlockSpec((LORA_WIDTH, 3 * rw), lambda b, i: (0, 0), pipeline_mode=pl.Buffered(1)),
                  vec(rw), vec(rw), vec(rw), vec(rw), vec(rw)],
        out_specs=[row_spec] * 8,
        out_shape=[half] * 5 + [big] * 3,
        scratch_shapes=[pltpu.VMEM((8, SHIFT_WIDTH), F32)],
        compiler_params=pltpu.CompilerParams(
            dimension_semantics=("arbitrary", "arbitrary"), vmem_limit_bytes=VMEM_LIMIT),
        name="rwkv_prep",
    )(x, mod3, mod3, g_pre, w_rwkv, mu, w_lora, w0, a0, k_k, k_a, r_k)


def _pair_blockdiag(x):
    first = lax.broadcasted_iota(jnp.int32, x.shape, 1) < HEAD_DIM
    zero = jnp.zeros_like(x)
    return jnp.concatenate([jnp.where(first, x, zero), jnp.where(first, zero, x)], axis=0)


def _pmm(a, b):
    return jnp.dot(_bf(a), _pair_blockdiag(_bf(b)), preferred_element_type=F32)


def _pmm_nt(a, b):
    return lax.dot_general(_bf(a), _pair_blockdiag(_bf(b)), (((1,), (1,)), ((), ())),
                           preferred_element_type=F32)


def _pmm_tn(a, b):
    full = lax.dot_general(_bf(a), _bf(b), (((0,), (0,)), ((), ())), preferred_element_type=F32)
    first = lax.broadcasted_iota(jnp.int32, (HEAD_DIM, LANES), 1) < HEAD_DIM
    return jnp.where(first, full[:HEAD_DIM], full[HEAD_DIM:])


def _unit_lower_inverse(a_list, masks):
    eye, diag_mask, merge_masks = masks
    ps = [jnp.where(diag_mask, a, 0.0) for a in a_list]
    ts = [eye + p for p in ps]
    ps = [_pmm(p, p) for p in ps]
    m = 2
    while 2 * m < INV_BASE:
        both = [_pmm(jnp.concatenate([p, t], axis=0), p) for p, t in zip(ps, ts)]
        ps = [x[:CHUNK] for x in both]
        ts = [t + x[CHUNK:] for t, x in zip(ts, both)]
        m *= 2
    ts = [t + _pmm(t, p) for t, p in zip(ts, ps)]
    for mk in merge_masks:
        ws = [_pmm(jnp.where(mk, a, 0.0), t) for a, t in zip(a_list, ts)]
        ts = [t + _pmm(t, w) for t, w in zip(ts, ws)]
    return ts


def _rwkv_scan_kernel(r_ref, k_ref, v_ref, nkk_ref, b_ref, cum_ref, y_ref, st_ref):
    @pl.when(pl.program_id(1) == 0)
    def _():
        st_ref[...] = jnp.zeros_like(st_ref)

    ri = lax.broadcasted_iota(jnp.int32, (CHUNK, LANES), 0)
    ci = _rem(lax.broadcasted_iota(jnp.int32, (CHUNK, LANES), 1), HEAD_DIM)
    strict = ri > ci
    incl = ri >= ci
    ri2 = lax.broadcasted_iota(jnp.int32, (2 * CHUNK, LANES), 0)
    ci2 = _rem(lax.broadcasted_iota(jnp.int32, (2 * CHUNK, LANES), 1), HEAD_DIM)
    rloc = _rem(ri2, CHUNK)
    stacked = (rloc > ci2) | ((rloc == ci2) & (ri2 >= CHUNK))
    eye = jnp.where(ri == ci, 1.0, 0.0)
    diag_mask = _div(ri, INV_BASE) == _div(ci, INV_BASE)
    merge_masks = []
    bsize = INV_BASE
    while bsize < CHUNK:
        merge_masks.append((_div(ri, 2 * bsize) == _div(ci, 2 * bsize))
                           & (_div(ri, bsize) != _div(ci, bsize)))
        bsize *= 2
    masks = (eye, diag_mask, merge_masks)

    n_rows = r_ref.shape[0]
    n_chunks = r_ref.shape[1] // CHUNK
    n_pairs = RWKV_WIDTH // LANES
    seqs = [(b, j) for b in range(n_rows) for j in range(n_pairs)]
    items = [(c, s) for c in range(n_chunks) for s in seqs]

    def tile(ref, c, s):
        b, j = s
        return ref[b, c * CHUNK:(c + 1) * CHUNK, j * LANES:(j + 1) * LANES]

    first_row = ri == 0
    qs, bk, kd, bkg, gcs = {}, {}, {}, {}, {}
    for it in items:
        cum = tile(cum_ref, *it)
        cum_last = cum[CHUNK - 1:CHUNK, :]
        e_cum = jnp.exp2(cum)
        e_prev = jnp.exp2(jnp.where(first_row, 0.0, pltpu.roll(cum, 1, axis=0)))
        e_inv = jnp.exp2(-cum)
        e_rest = jnp.exp2(cum_last - cum)
        gcs[it] = jnp.exp2(cum_last)
        r, k = tile(r_ref, *it).astype(F32), tile(k_ref, *it).astype(F32)
        nkk, b = tile(nkk_ref, *it).astype(F32), tile(b_ref, *it).astype(F32)
        qs[it] = _bf(jnp.concatenate([nkk * e_prev, r * e_cum], axis=0))
        bk[it], kd[it] = _bf(b * e_inv), _bf(k * e_inv)
        bkg[it] = _bf(jnp.concatenate([b * e_rest, k * e_rest], axis=0))

    nt = lambda a, b: lax.dot_general(a, b, (((1,), (1,)), ((), ())), preferred_element_type=F32)
    abk = {it: nt(qs[it], jnp.concatenate([_pair_blockdiag(bk[it]), _pair_blockdiag(kd[it])], axis=0))
           for it in items}
    ab = {it: abk[it][:, :LANES] for it in items}
    ak = {it: jnp.where(stacked, abk[it][:, LANES:], 0.0) for it in items}
    akv = {it: _pmm(ak[it], tile(v_ref, *it)) for it in items}
    t_list = _unit_lower_inverse([jnp.where(strict, ab[it][:CHUNK], 0.0) for it in items], masks)
    t = dict(zip(items, t_list))

    state = {s: st_ref[s[0], s[1]] for s in seqs}
    for c in range(n_chunks):
        xs = {s: _pmm_nt(qs[(c, s)], state[s]) for s in seqs}
        u = {s: _pmm(t[(c, s)], xs[s][:CHUNK] + akv[(c, s)][:CHUNK]) for s in seqs}
        for s in seqs:
            b, j = s
            y = xs[s][CHUNK:] + akv[(c, s)][CHUNK:] + _pmm(
                jnp.where(incl, ab[(c, s)][CHUNK:], 0.0), u[s])
            y_ref[b, c * CHUNK:(c + 1) * CHUNK, j * LANES:(j + 1) * LANES] = y
        new_state = {}
        for s in seqs:
            uv = jnp.concatenate([_bf(u[s]), tile(v_ref, c, s)], axis=0)
            new_state[s] = state[s] * gcs[(c, s)] + _pmm_tn(uv, bkg[(c, s)])
        state = new_state
    for s in seqs:
        st_ref[s[0], s[1]] = state[s]


def _rwkv_scan(r, k, v, nkk, b, cum, rows_per_step, chunks_per_step):
    bsz, seq, rw = r.shape
    ts = chunks_per_step * CHUNK
    row_spec = pl.BlockSpec((rows_per_step, ts, rw), lambda b, c: (b, c, 0))
    return pl.pallas_call(
        _rwkv_scan_kernel,
        grid=(bsz // rows_per_step, seq // ts),
        in_specs=[row_spec] * 6,
        out_specs=row_spec,
        out_shape=jax.ShapeDtypeStruct((bsz, seq, rw), F32),
        scratch_shapes=[pltpu.VMEM((rows_per_step, rw // LANES, HEAD_DIM, LANES), F32)],
        compiler_params=pltpu.CompilerParams(dimension_semantics=("arbitrary", "arbitrary")),
        name="rwkv_scan",
    )(r, k, v, nkk, b, cum)


def _attn_proj_kernel(x_ref, shift_ref, scale_ref, g_ref, w_ref, cos_ref, sin_ref, *rest):
    outs, buf = rest[:-1], rest[-1]
    aw = ATTN_WIDTH
    reps = aw // LANES
    lane = lax.broadcasted_iota(jnp.int32, (SUB_ROWS, aw), 1)
    first_half = _rem(lane, HEAD_DIM) < (HEAD_DIM // 2)
    tiles_per_group = GROUP_WIDTH // LANES

    subs = range(x_ref.shape[1] // SUB_ROWS)
    ps = []
    for sb in subs:
        rows = slice(sb * SUB_ROWS, (sb + 1) * SUB_ROWS)
        h = _norm_modulate(x_ref[0, rows, :], g_ref[...], shift_ref[0], scale_ref[0])
        ps.append(jnp.dot(_bf(h), w_ref[...], preferred_element_type=F32))
    for sb in subs:
        rows = slice(sb * SUB_ROWS, (sb + 1) * SUB_ROWS)
        p = ps[sb]
        cos = jnp.concatenate([cos_ref[rows, :]] * reps, axis=-1)
        sin = jnp.concatenate([sin_ref[rows, :]] * reps, axis=-1)

        def rope(t, cos=cos, sin=sin):
            partner = jnp.where(first_half, pltpu.roll(t, aw - HEAD_DIM // 2, axis=1),
                                pltpu.roll(t, HEAD_DIM // 2, axis=1))
            return t * cos + partner * sin

        roped = (rope(p[:, :aw]) * (HEAD_DIM ** -0.5), rope(p[:, aw:2 * aw]), p[:, 2 * aw:])
        n = 0
        for which in range(3):
            for g, (_, dil) in enumerate(ATTN_GROUPS):
                o_ref = outs[n]
                n += 1
                out_rows = slice(sb * SUB_ROWS // dil, (sb + 1) * SUB_ROWS // dil)
                for ct in range(tiles_per_group):
                    c0 = g * GROUP_WIDTH + ct * LANES
                    val = roped[which][:, c0:c0 + LANES]
                    if dil == 1:
                        o_ref[0, 0, out_rows, ct * LANES:(ct + 1) * LANES] = _bf(val)
                        continue
                    slot = (which * len(ATTN_GROUPS) + g) * tiles_per_group + ct
                    buf[sb, slot] = val
                    for r in range(dil):
                        o_ref[0, r, out_rows, ct * LANES:(ct + 1) * LANES] = _bf(
                            buf[sb, slot, pl.ds(r, SUB_ROWS // dil, stride=dil), :])


def _attn_proj(x, mod3, g_pre, w_qkv, cos_t, sin_t, tm):
    bsz, seq, d = x.shape
    out_specs, out_shape = [], []
    for _ in range(3):
        for _, dil in ATTN_GROUPS:
            out_specs.append(pl.BlockSpec((1, dil, tm // dil, GROUP_WIDTH), lambda b, i: (b, 0, i, 0)))
            out_shape.append(jax.ShapeDtypeStruct((bsz, dil, seq // dil, GROUP_WIDTH), BF16))
    return pl.pallas_call(
        _attn_proj_kernel,
        grid=(bsz, seq // tm),
        in_specs=[pl.BlockSpec((1, tm, d), lambda b, i: (b, i, 0)),
                  pl.BlockSpec((1, 1, d), lambda b, i: (b, 0, 0)),
                  pl.BlockSpec((1, 1, d), lambda b, i: (b, 0, 1)),
                  pl.BlockSpec((1, d), lambda b, i: (0, 0)),
                  pl.BlockSpec((d, 3 * ATTN_WIDTH), lambda b, i: (0, 0), pipeline_mode=pl.Buffered(1)),
                  pl.BlockSpec((tm, LANES), lambda b, i: (i, 0)),
                  pl.BlockSpec((tm, LANES), lambda b, i: (i, 0))],
        out_specs=out_specs,
        out_shape=out_shape,
        scratch_shapes=[pltpu.VMEM((tm // SUB_ROWS, 3 * ATTN_WIDTH // LANES, SUB_ROWS, LANES), F32)],
        compiler_params=pltpu.CompilerParams(
            dimension_semantics=("arbitrary", "arbitrary"), vmem_limit_bytes=VMEM_LIMIT),
        name="attn_proj",
    )(x, mod3, mod3, g_pre, w_qkv, cos_t, sin_t)


def _attn_kernel(*refs):
    qkv = refs[:9]
    o_ref = refs[9]
    num_ref, m_ref, den_ref = refs[10:]
    blk = ATTN_BLOCK
    qi = lax.broadcasted_iota(jnp.int32, (blk, blk), 0)
    kj = lax.broadcasted_iota(jnp.int32, (blk, blk), 1)
    bias_cur = jnp.where(kj <= qi, 0.0, NEG_BIG)
    bias_prev = jnp.where(kj >= qi, 0.0, NEG_BIG)
    lane = lax.broadcasted_iota(jnp.int32, (blk, LANES), 1)
    head0 = lane < HEAD_DIM

    for g, (_, dil) in enumerate(ATTN_GROUPS):
        q_ref, k_ref, v_ref = qkv[g], qkv[3 + g], qkv[6 + g]
        nb = q_ref.shape[2] // blk

        nper = min(nb, ATTN_BLOCKS_PER_ITER)
        rper = ATTN_BLOCKS_PER_ITER // nper
        whole = nper == nb

        def body(idx, carry, q_ref=q_ref, k_ref=k_ref, v_ref=v_ref, dil=dil, nb=nb, g=g,
                 nper=nper, rper=rper, whole=whole):
            if whole:
                r0, n0 = idx * rper, 0
                base = 0
            else:
                r0 = _div(idx, nb // nper)
                n0 = _rem(idx, nb // nper) * nper
                base = pl.multiple_of(n0 * blk, blk)
            nt = lambda a, b: lax.dot_general(a, b, (((1,), (1,)), ((), ())), preferred_element_type=F32)
            one = jnp.ones((blk, LANES), BF16)
            zero = jnp.zeros((blk, LANES), BF16)
            q_msk, ks, v_aug = {}, {}, {}
            for rr in range(rper):
                r = r0 + rr
                first = 0 if whole else -1
                for i in range(first, nper):
                    if i < 0:
                        st = pl.multiple_of(jnp.maximum(n0 - 1, 0) * blk, blk)
                    else:
                        st = base + i * blk
                    ks[(rr, i)] = k_ref[0, r, pl.ds(st, blk), :]
                    v = v_ref[0, r, pl.ds(st, blk), :]
                    v_aug[(rr, i)] = (jnp.where(head0, v, one), jnp.where(head0, one, v))
                    if i >= 0:
                        q = q_ref[0, r, pl.ds(st, blk), :]
                        q_msk[(rr, i)] = (jnp.where(head0, q, zero), jnp.where(head0, zero, q))
            items = [(rr, i, e) for rr in range(rper) for i in range(nper) for e in range(2)]
            has_prev = lambda it: (it[0], it[1] - 1) in ks
            s_c = {it: nt(q_msk[it[:2]][it[2]], ks[it[:2]]) for it in items}
            s_p = {it: nt(q_msk[it[:2]][it[2]], ks[(it[0], it[1] - 1)]) for it in items if has_prev(it)}
            bias_first = bias_prev if whole else jnp.where(n0 > 0, bias_prev, NEG_BIG)
            ms, pcs, pps = {}, {}, {}
            for it in items:
                sc = s_c[it] + bias_cur
                if has_prev(it):
                    sp = s_p[it] + (bias_first if it[1] == 0 else bias_prev)
                    m = jnp.max(jnp.maximum(sc, sp), axis=-1, keepdims=True)
                    pps[it] = _bf(jnp.exp(sp - m))
                else:
                    m = jnp.max(sc, axis=-1, keepdims=True)
                ms[it] = m
                pcs[it] = _bf(jnp.exp(sc - m))
            pv = {}
            for it in items:
                acc = jnp.dot(pcs[it], v_aug[it[:2]][it[2]], preferred_element_type=F32)
                if has_prev(it):
                    acc = acc + jnp.dot(pps[it], v_aug[(it[0], it[1] - 1)][it[2]],
                                        preferred_element_type=F32)
                pv[it] = acc
            for rr in range(rper):
                for i in range(nper):
                    m_b = jnp.where(head0, ms[(rr, i, 0)], ms[(rr, i, 1)])
                    num_b = jnp.where(head0, pv[(rr, i, 0)], pv[(rr, i, 1)])
                    l_b = pltpu.roll(jnp.where(head0, pv[(rr, i, 1)], pv[(rr, i, 0)]), HEAD_DIM, axis=1)
                    if g == 0:
                        rows = pl.ds(base + i * blk, blk)
                        num_ref[rows, :] = num_b
                        m_ref[rows, :] = m_b
                        den_ref[rows, :] = l_b
                    else:
                        rows = pl.ds((n0 + i) * (blk * dil) + r0 + rr, blk, stride=dil)
                        m_old = m_ref[rows, :]
                        m_new = jnp.maximum(m_old, m_b)
                        w_old = jnp.exp(m_old - m_new)
                        w_b = jnp.exp(m_b - m_new)
                        num_ref[rows, :] = num_ref[rows, :] * w_old + num_b * w_b
                        den_ref[rows, :] = den_ref[rows, :] * w_old + l_b * w_b
                        m_ref[rows, :] = m_new
            return carry

        lax.fori_loop(0, dil * nb // (nper * rper), body, 0)

    o_ref[0] = _bf(num_ref[...] / den_ref[...])


def _attn(qkv, seq):
    bsz = qkv[0].shape[0]
    in_specs = []
    for _ in range(3):
        for _, dil in ATTN_GROUPS:
            in_specs.append(pl.BlockSpec((1, dil, seq // dil, LANES), lambda b, sp: (b, 0, 0, sp)))
    return pl.pallas_call(
        _attn_kernel,
        grid=(bsz, GROUP_WIDTH // LANES),
        in_specs=in_specs,
        out_specs=pl.BlockSpec((1, seq, LANES), lambda b, sp: (b, 0, sp)),
        out_shape=jax.ShapeDtypeStruct((bsz, seq, GROUP_WIDTH), BF16),
        scratch_shapes=[pltpu.VMEM((seq, LANES), F32)] * 3,
        compiler_params=pltpu.CompilerParams(
            dimension_semantics=("arbitrary", "arbitrary"), vmem_limit_bytes=VMEM_LIMIT),
        name="attn",
    )(*qkv)


def _merge_kernel(x_ref, shift_ref, scale_ref, gatem_ref, g_ref, wg_ref, y_ref, bonus_ref, gate_ref,
                  lnw_ref, lnb_ref, wa_ref, o_ref, wb_ref, wout_ref, gpost_ref, out_ref):
    inv_n = 1.0 / HEAD_DIM
    subs = range(x_ref.shape[1] // SUB_ROWS)
    rows = [slice(sb * SUB_ROWS, (sb + 1) * SUB_ROWS) for sb in subs]
    yb = [jnp.dot(o_ref[0, rows[sb], :], wb_ref[...], preferred_element_type=F32) for sb in subs]
    y = [y_ref[0, rows[sb], :] for sb in subs]
    mean = [_head_sum(y[sb]) * inv_n for sb in subs]
    h = [_bf(_norm_modulate(x_ref[0, rows[sb], :], g_ref[...], shift_ref[0], scale_ref[0])) for sb in subs]
    pg = [jnp.dot(h[sb], wg_ref[...], preferred_element_type=F32) for sb in subs]
    yc = [y[sb] - mean[sb] for sb in subs]
    var = [_head_sum(yc[sb] * yc[sb]) * inv_n for sb in subs]
    pre = [_bf((yc[sb] * lax.rsqrt(var[sb] + GN_EPS) * lnw_ref[...] + lnb_ref[...]
                + bonus_ref[0, rows[sb], :]) * gate_ref[0, rows[sb], :]) for sb in subs]
    ya = [jnp.dot(pre[sb], wa_ref[...], preferred_element_type=F32) for sb in subs]
    merged = [_bf(_sigmoid(pg[sb][:, :D_MODEL]) * ya[sb] + _sigmoid(pg[sb][:, D_MODEL:]) * yb[sb])
              for sb in subs]
    zz = [jnp.dot(merged[sb], wout_ref[...], preferred_element_type=F32) for sb in subs]
    for sb in subs:
        out_ref[0, rows[sb], :] = x_ref[0, rows[sb], :] + gatem_ref[0] * _rms(zz[sb], gpost_ref[...])


def _merge(x, mod3, g_pre, w_g, y, bonus, gate, lnx_w, lnx_b, w_a, o_att, w_b, w_out, g_post, tm):
    bsz, seq, d = x.shape
    rw = RWKV_WIDTH
    const = lambda shape: pl.BlockSpec(shape, lambda b, i: (0, 0), pipeline_mode=pl.Buffered(1))
    row = lambda n: pl.BlockSpec((1, tm, n), lambda b, i: (b, i, 0))
    modv = lambda j: pl.BlockSpec((1, 1, d), lambda b, i: (b, 0, j))
    return pl.pallas_call(
        _merge_kernel,
        grid=(bsz, seq // tm),
        in_specs=[row(d), modv(0), modv(1), modv(2), const((1, d)), const((d, 2 * d)),
                  row(rw), row(rw), row(rw), const((1, rw)), const((1, rw)), const((rw, d)),
                  row(GROUP_WIDTH), const((GROUP_WIDTH, d)), const((d, d)), const((1, d))],
        out_specs=row(d),
        out_shape=jax.ShapeDtypeStruct((bsz, seq, d), F32),
        compiler_params=pltpu.CompilerParams(
            dimension_semantics=("arbitrary", "arbitrary"), vmem_limit_bytes=VMEM_LIMIT),
        name="merge",
    )(x, mod3, mod3, mod3, g_pre, w_g, y, bonus, gate, lnx_w, lnx_b, w_a, o_att, w_b, w_out, g_post)


def _ffn_kernel(x_ref, shift_ref, scale_ref, gatef_ref, g_ref, win_ref, wout_ref, gpost_ref, out_ref):
    dff = wout_ref.shape[0]
    subs = range(x_ref.shape[1] // SUB_ROWS)
    rows = [slice(sb * SUB_ROWS, (sb + 1) * SUB_ROWS) for sb in subs]
    h = [_bf(_norm_modulate(x_ref[0, rows[sb], :], g_ref[...], shift_ref[0], scale_ref[0])) for sb in subs]
    u = [jnp.dot(h[sb], win_ref[:, :dff], preferred_element_type=F32) for sb in subs]
    gt = [jnp.dot(h[sb], win_ref[:, dff:], preferred_element_type=F32) for sb in subs]
    act = [_bf(u[sb] * _sigmoid(u[sb]) * gt[sb]) for sb in subs]
    f = [jnp.dot(act[sb], wout_ref[...], preferred_element_type=F32) for sb in subs]
    for sb in subs:
        out_ref[0, rows[sb], :] = x_ref[0, rows[sb], :] + gatef_ref[0] * _rms(f[sb], gpost_ref[...])


def _ffn(x, mod3, g_pre, w_in, w_out, g_post, tm):
    bsz, seq, d = x.shape
    dff = w_out.shape[0]
    const = lambda shape: pl.BlockSpec(shape, lambda b, i: (0, 0), pipeline_mode=pl.Buffered(1))
    row = pl.BlockSpec((1, tm, d), lambda b, i: (b, i, 0))
    modv = lambda j: pl.BlockSpec((1, 1, d), lambda b, i: (b, 0, j))
    return pl.pallas_call(
        _ffn_kernel,
        grid=(bsz, seq // tm),
        in_specs=[row, modv(3), modv(4), modv(5), const((1, d)), const((d, 2 * dff)),
                  const((dff, d)), const((1, d))],
        out_specs=row,
        out_shape=jax.ShapeDtypeStruct((bsz, seq, d), F32),
        compiler_params=pltpu.CompilerParams(
            dimension_semantics=("arbitrary", "arbitrary"), vmem_limit_bytes=VMEM_LIMIT),
        name="ffn",
    )(x, mod3, mod3, mod3, g_pre, w_in, w_out, g_post)


def _rope_tables(seq):
    half = HEAD_DIM // 2
    inv_freq = ROPE_THETA ** (-jnp.arange(half, dtype=F32) / half)
    ang = jnp.arange(seq, dtype=F32)[:, None] * inv_freq[None, :]
    cos, sin = jnp.cos(ang), jnp.sin(ang)
    reps = LANES // HEAD_DIM
    cos_t = jnp.tile(jnp.concatenate([cos, cos], axis=-1), (1, reps))
    sin_t = jnp.tile(jnp.concatenate([-sin, sin], axis=-1), (1, reps))
    return cos_t, sin_t


def _layer(x, c, w_ada, b_ada, g_pre_mix, g_post_mix, w_in, mu_shift, w0, w2, a0, a2, g2,
           k_k, k_a, r_k, lnx_w, lnx_b, w_a, w_b, w_out, g_pre_ffn, g_post_ffn, w_ffn_in, w_ffn_out):
    bsz, seq, d = x.shape
    rw = RWKV_WIDTH
    row1 = lambda t: t.reshape(1, -1)
    mod3 = _mod(c, w_ada, b_ada).reshape(bsz, 1, 6 * d)

    w_rwkv = _bf(w_in[:, :SHIFT_WIDTH])
    w_qkv = _bf(w_in[:, SHIFT_WIDTH:SHIFT_WIDTH + 3 * ATTN_WIDTH])
    w_gates = _bf(w_in[:, SHIFT_WIDTH + 3 * ATTN_WIDTH:])
    w_lora = jnp.zeros((LORA_WIDTH, 3 * rw), F32)
    w_lora = w_lora.at[:DECAY_LORA, :rw].set(w2)
    w_lora = w_lora.at[DECAY_LORA:DECAY_LORA + ICLR_LORA, rw:2 * rw].set(a2)
    w_lora = _bf(w_lora.at[DECAY_LORA + ICLR_LORA:, 2 * rw:].set(g2))

    prep = _rwkv_prep(x, mod3, row1(g_pre_mix), w_rwkv, row1(mu_shift), w_lora, row1(w0), row1(a0),
                      row1(k_k), row1(k_a), row1(r_k), tm=512)
    r, k, v, nkk, b, cum, bonus, gate = prep
    y = _rwkv_scan(r, k, v, nkk, b, cum, rows_per_step=8, chunks_per_step=1)

    cos_t, sin_t = _rope_tables(seq)
    qkv = _attn_proj(x, mod3, row1(g_pre_mix), w_qkv, cos_t, sin_t, tm=1024)
    o_att = _attn(qkv, seq)

    x1 = _merge(x, mod3, row1(g_pre_mix), w_gates, y, bonus, gate, row1(lnx_w), row1(lnx_b),
                _bf(w_a), o_att, _bf(w_b), _bf(w_out), row1(g_post_mix), tm=512)
    return _ffn(x1, mod3, row1(g_pre_ffn), _bf(w_ffn_in), _bf(w_ffn_out), row1(g_post_ffn), tm=512)


def kernel(x, c, w_ada, b_ada, g_pre_mix, g_post_mix, w_in, mu_shift, w0, w2, a0, a2, g2, k_k, k_a, r_k,
           lnx_w, lnx_b, w_a, w_b, w_out, g_pre_ffn, g_post_ffn, w_ffn_in, w_ffn_out):
    for i in range(w_in.shape[0]):
        x = _layer(x, c, w_ada[i], b_ada[i], g_pre_mix[i], g_post_mix[i], w_in[i], mu_shift[i], w0[i],
                   w2[i], a0[i], a2[i], g2[i], k_k[i], k_a[i], r_k[i], lnx_w[i], lnx_b[i], w_a[i], w_b[i],
                   w_out[i], g_pre_ffn[i], g_post_ffn[i], w_ffn_in[i], w_ffn_out[i])
    return x
```

```python
import functools
import math

import jax
import jax.numpy as jnp
from jax import lax
from jax.experimental import pallas as pl
from jax.experimental.pallas import tpu as pltpu

D_MODEL = 1024
HEAD_DIM = 64
RWKV_HEADS = 12
RWKV_WIDTH = RWKV_HEADS * HEAD_DIM
DECAY_LORA = 64
ICLR_LORA = 64
GATE_LORA = 128
LORA_WIDTH = DECAY_LORA + ICLR_LORA + GATE_LORA
ATTN_GROUPS = ((128, 1), (512, 4), (2048, 16))
HEADS_PER_GROUP = 4
GROUP_WIDTH = HEADS_PER_GROUP * HEAD_DIM
ATTN_WIDTH = len(ATTN_GROUPS) * GROUP_WIDTH
ATTN_BLOCK = 128
ROPE_THETA = 10000.0
SHIFT_WIDTH = 3 * RWKV_WIDTH + LORA_WIDTH
NORM_EPS = 1e-6
GN_EPS = 64e-5

CHUNK = 64
INV_BASE = 16
LANES = 128
NEG_BIG = -1e30
LOG2E = math.log2(math.e)
ATTN_BLOCKS_PER_ITER = 8
SUB_ROWS = 256
ATTN_PROJ_SUB_ROWS = 512
VMEM_LIMIT = 56 * 1024 * 1024

F32 = jnp.float32
BF16 = jnp.bfloat16


def _bf(x):
    return x.astype(BF16)


def _div(x, n):
    assert n & (n - 1) == 0
    return jnp.right_shift(x, n.bit_length() - 1)


def _rem(x, n):
    assert n & (n - 1) == 0
    return jnp.bitwise_and(x, n - 1)


def _sigmoid(x):
    return 1.0 / (1.0 + jnp.exp2(x * (-LOG2E)))


def _split_hi_lo(x):
    hi = _bf(x)
    lo = _bf(x - hi.astype(F32))
    return hi, lo


def _mm_exact_lhs(m_bf, x):
    hi, lo = _split_hi_lo(x)
    return (jnp.dot(m_bf, hi, preferred_element_type=F32)
            + jnp.dot(m_bf, lo, preferred_element_type=F32))


def _head_sum(x):
    tile = 2 * LANES
    ri = _div(lax.broadcasted_iota(jnp.int32, (tile, tile), 0), HEAD_DIM)
    ci = _div(lax.broadcasted_iota(jnp.int32, (tile, tile), 1), HEAD_DIM)
    ones_bd = jnp.where(ri == ci, 1.0, 0.0).astype(BF16)
    xb = _bf(x)
    parts = [jnp.dot(xb[:, j * tile:(j + 1) * tile], ones_bd, preferred_element_type=F32)
             for j in range(x.shape[-1] // tile)]
    return jnp.concatenate(parts, axis=-1)


def _norm_modulate(x, g, shift, scale):
    y = x * lax.rsqrt(jnp.mean(x * x, axis=-1, keepdims=True) + NORM_EPS)
    return y * (g * (1.0 + scale)) + shift


def _rms(x, g):
    return x * lax.rsqrt(jnp.mean(x * x, axis=-1, keepdims=True) + NORM_EPS) * g


def _mod_kernel(c_ref, w_ref, b_ref, o_ref):
    c = c_ref[...]
    hi, lo = _split_hi_lo(c * _sigmoid(c))
    w = _bf(w_ref[...])
    o_ref[...] = (jnp.dot(hi, w, preferred_element_type=F32)
                  + jnp.dot(lo, w, preferred_element_type=F32) + b_ref[...])


def _mod(c, w_ada, b_ada):
    bsz, d = c.shape
    n = w_ada.shape[1]
    tn = d
    return pl.pallas_call(
        _mod_kernel,
        grid=(n // tn,),
        in_specs=[pl.BlockSpec((bsz, d), lambda j: (0, 0)),
                  pl.BlockSpec((d, tn), lambda j: (0, j)),
                  pl.BlockSpec((1, tn), lambda j: (0, j))],
        out_specs=pl.BlockSpec((bsz, tn), lambda j: (0, j)),
        out_shape=jax.ShapeDtypeStruct((bsz, n), F32),
        name="mod",
    )(c, w_ada, b_ada.reshape(1, n))


def _rwkv_prep_kernel(x_ref, shift_ref, scale_ref, g_ref, w_ref, mu_ref, wl_ref, w0_ref, a0_ref,
                      kk_ref, ka_ref, rk_ref,
                      r_o, k_o, v_o, nkk_o, b_o, cum_o, bonus_o, gate_o,
                      carry_ref):
    i = pl.program_id(1)
    rw = RWKV_WIDTH
    sub = SUB_ROWS

    @pl.when(i == 0)
    def _():
        carry_ref[...] = jnp.zeros_like(carry_ref)

    ri = lax.broadcasted_iota(jnp.int32, (sub, sub), 0)
    ci = lax.broadcasted_iota(jnp.int32, (sub, sub), 1)
    same = _div(ri, CHUNK) == _div(ci, CHUNK)
    lower = jnp.where(same & (ci <= ri), 1.0, 0.0).astype(BF16)
    row0 = lax.broadcasted_iota(jnp.int32, (8, SHIFT_WIDTH), 0) == 0
    lane = lax.broadcasted_iota(jnp.int32, (sub, LORA_WIDTH), 1)

    subs = range(x_ref.shape[1] // sub)
    rows = [slice(sb * sub, (sb + 1) * sub) for sb in subs]
    h = [_bf(_norm_modulate(x_ref[0, rows[sb], :], g_ref[...], shift_ref[0], scale_ref[0])) for sb in subs]
    p = [jnp.dot(h[sb], w_ref[...], preferred_element_type=F32) for sb in subs]

    prev_row = carry_ref[7:8, :]
    ps = []
    for sb in subs:
        rolled = pltpu.roll(p[sb], 1, axis=0)
        shifted = jnp.concatenate([jnp.where(row0, prev_row, rolled[:8]), rolled[8:]], axis=0)
        last8 = p[sb][sub - 8:, :]
        prev_row = last8[7:8, :]
        ps.append(p[sb] + mu_ref[...] * (shifted - p[sb]))
    carry_ref[...] = last8

    z = []
    for sb in subs:
        lora_in = ps[sb][:, 3 * rw:]
        z.append(_bf(jnp.where(lane < DECAY_LORA, jnp.tanh(lora_in),
                               jnp.where(lane < DECAY_LORA + ICLR_LORA, lora_in, _sigmoid(lora_in)))))
    lo = [jnp.dot(z[sb], wl_ref[...], preferred_element_type=F32) for sb in subs]

    pr = [ps[sb][:, :rw] for sb in subs]
    pk = [ps[sb][:, rw:2 * rw] for sb in subs]
    pv = [ps[sb][:, 2 * rw:3 * rw] for sb in subs]
    kkr = [pk[sb] * kk_ref[...] for sb in subs]
    kk_sq = [_head_sum(kkr[sb] * kkr[sb]) for sb in subs]
    a = [_sigmoid(a0_ref[...] + lo[sb][:, rw:2 * rw]) for sb in subs]
    k = [pk[sb] * (1.0 + (a[sb] - 1.0) * ka_ref[...]) for sb in subs]
    rk_sum = [_head_sum(pr[sb] * k[sb] * rk_ref[...]) for sb in subs]
    lw = [(-math.exp(-0.5) * LOG2E) * _sigmoid(w0_ref[...] + lo[sb][:, :rw]) for sb in subs]
    cum = [_mm_exact_lhs(lower, lw[sb]) for sb in subs]

    for sb in subs:
        kkn = kkr[sb] * (1.0 / jnp.maximum(jnp.sqrt(kk_sq[sb]), 1e-12))
        gate_o[0, rows[sb], :] = lo[sb][:, 2 * rw:]
        bonus_o[0, rows[sb], :] = rk_sum[sb] * pv[sb]
        cum_o[0, rows[sb], :] = cum[sb]
        r_o[0, rows[sb], :] = _bf(pr[sb])
        k_o[0, rows[sb], :] = _bf(k[sb])
        v_o[0, rows[sb], :] = _bf(pv[sb])
        nkk_o[0, rows[sb], :] = _bf(-kkn)
        b_o[0, rows[sb], :] = _bf(kkn * a[sb])


def _rwkv_prep(x, mod3, g_pre, w_rwkv, mu, w_lora, w0, a0, k_k, k_a, r_k, tm):
    bsz, seq, d = x.shape
    rw = RWKV_WIDTH
    row_spec = pl.BlockSpec((1, tm, rw), lambda b, i: (b, i, 0))
    vec = lambda n: pl.BlockSpec((1, n), lambda b, i: (0, 0))
    big = jax.ShapeDtypeStruct((bsz, seq, rw), F32)
    half = jax.ShapeDtypeStruct((bsz, seq, rw), BF16)
    return pl.pallas_call(
        _rwkv_prep_kernel,
        grid=(bsz, seq // tm),
        in_specs=[pl.BlockSpec((1, tm, d), lambda b, i: (b, i, 0)),
                  pl.BlockSpec((1, 1, d), lambda b, i: (b, 0, 0)),
                  pl.BlockSpec((1, 1, d), lambda b, i: (b, 0, 1)),
                  vec(d),
                  pl.BlockSpec((d, SHIFT_WIDTH), lambda b, i: (0, 0), pipeline_mode=pl.Buffered(1)),
                  vec(SHIFT_WIDTH),
                  pl.BlockSpec((LORA_WIDTH, 3 * rw), lambda b, i: (0, 0), pipeline_mode=pl.Buffered(1)),
                  vec(rw), vec(rw), vec(rw), vec(rw), vec(rw)],
        out_specs=[row_spec] * 8,
        out_shape=[half] * 5 + [big] * 3,
        scratch_shapes=[pltpu.VMEM((8, SHIFT_WIDTH), F32)],
        compiler_params=pltpu.CompilerParams(
            dimension_semantics=("arbitrary", "arbitrary"), vmem_limit_bytes=VMEM_LIMIT),
        name="rwkv_prep",
    )(x, mod3, mod3, g_pre, w_rwkv, mu, w_lora, w0, a0, k_k, k_a, r_k)


def _pair_blockdiag(x):
    first = lax.broadcasted_iota(jnp.int32, x.shape, 1) < HEAD_DIM
    zero = jnp.zeros_like(x)
    return jnp.concatenate([jnp.where(first, x, zero), jnp.where(first, zero, x)], axis=0)


def _pmm(a, b):
    return jnp.dot(_bf(a), _pair_blockdiag(_bf(b)), preferred_element_type=F32)


def _pmm_nt(a, b):
    return lax.dot_general(_bf(a), _pair_blockdiag(_bf(b)), (((1,), (1,)), ((), ())),
                           preferred_element_type=F32)


def _pmm_tn(a, b):
    full = lax.dot_general(_bf(a), _bf(b), (((0,), (0,)), ((), ())), preferred_element_type=F32)
    first = lax.broadcasted_iota(jnp.int32, (HEAD_DIM, LANES), 1) < HEAD_DIM
    return jnp.where(first, full[:HEAD_DIM], full[HEAD_DIM:])


def _unit_lower_inverse(a_list, masks):
    eye, diag_mask, merge_masks = masks
    ps = [jnp.where(diag_mask, a, 0.0) for a in a_list]
    ts = [eye + p for p in ps]
    ps = [_pmm(p, p) for p in ps]
    m = 2
    while 2 * m < INV_BASE:
        both = [_pmm(jnp.concatenate([p, t], axis=0), p) for p, t in zip(ps, ts)]
        ps = [x[:CHUNK] for x in both]
        ts = [t + x[CHUNK:] for t, x in zip(ts, both)]
        m *= 2
    ts = [t + _pmm(t, p) for t, p in zip(ts, ps)]
    for mk in merge_masks:
        ws = [_pmm(jnp.where(mk, a, 0.0), t) for a, t in zip(a_list, ts)]
        ts = [t + _pmm(t, w) for t, w in zip(ts, ws)]
    return ts


def _rwkv_scan_kernel(r_ref, k_ref, v_ref, nkk_ref, b_ref, cum_ref, y_ref, st_ref):
    @pl.when(pl.program_id(1) == 0)
    def _():
        st_ref[...] = jnp.zeros_like(st_ref)

    ri = lax.broadcasted_iota(jnp.int32, (CHUNK, LANES), 0)
    ci = _rem(lax.broadcasted_iota(jnp.int32, (CHUNK, LANES), 1), HEAD_DIM)
    strict = ri > ci
    incl = ri >= ci
    ri2 = lax.broadcasted_iota(jnp.int32, (2 * CHUNK, LANES), 0)
    ci2 = _rem(lax.broadcasted_iota(jnp.int32, (2 * CHUNK, LANES), 1), HEAD_DIM)
    rloc = _rem(ri2, CHUNK)
    stacked = (rloc > ci2) | ((rloc == ci2) & (ri2 >= CHUNK))
    eye = jnp.where(ri == ci, 1.0, 0.0)
    diag_mask = _div(ri, INV_BASE) == _div(ci, INV_BASE)
    merge_masks = []
    bsize = INV_BASE
    while bsize < CHUNK:
        merge_masks.append((_div(ri, 2 * bsize) == _div(ci, 2 * bsize))
                           & (_div(ri, bsize) != _div(ci, bsize)))
        bsize *= 2
    masks = (eye, diag_mask, merge_masks)

    n_rows = r_ref.shape[0]
    n_chunks = r_ref.shape[1] // CHUNK
    n_pairs = RWKV_WIDTH // LANES
    seqs = [(b, j) for b in range(n_rows) for j in range(n_pairs)]
    items = [(c, s) for c in range(n_chunks) for s in seqs]

    def tile(ref, c, s):
        b, j = s
        return ref[b, c * CHUNK:(c + 1) * CHUNK, j * LANES:(j + 1) * LANES]

    first_row = ri == 0
    qs, bk, kd, bkg, gcs = {}, {}, {}, {}, {}
    for it in items:
        cum = tile(cum_ref, *it)
        cum_last = cum[CHUNK - 1:CHUNK, :]
        e_cum = jnp.exp2(cum)
        e_prev = jnp.exp2(jnp.where(first_row, 0.0, pltpu.roll(cum, 1, axis=0)))
        e_inv = jnp.exp2(-cum)
        e_rest = jnp.exp2(cum_last - cum)
        gcs[it] = jnp.exp2(cum_last)
        r, k = tile(r_ref, *it).astype(F32), tile(k_ref, *it).astype(F32)
        nkk, b = tile(nkk_ref, *it).astype(F32), tile(b_ref, *it).astype(F32)
        qs[it] = _bf(jnp.concatenate([nkk * e_prev, r * e_cum], axis=0))
        bk[it], kd[it] = _bf(b * e_inv), _bf(k * e_inv)
        bkg[it] = _bf(jnp.concatenate([b * e_rest, k * e_rest], axis=0))

    nt = lambda a, b: lax.dot_general(a, b, (((1,), (1,)), ((), ())), preferred_element_type=F32)
    abk = {it: nt(qs[it], jnp.concatenate([_pair_blockdiag(bk[it]), _pair_blockdiag(kd[it])], axis=0))
           for it in items}
    ab = {it: abk[it][:, :LANES] for it in items}
    ak = {it: jnp.where(stacked, abk[it][:, LANES:], 0.0) for it in items}
    akv = {it: _pmm(ak[it], tile(v_ref, *it)) for it in items}
    t_list = _unit_lower_inverse([jnp.where(strict, ab[it][:CHUNK], 0.0) for it in items], masks)
    t = dict(zip(items, t_list))

    state = {s: st_ref[s[0], s[1]] for s in seqs}
    for c in range(n_chunks):
        xs = {s: _pmm_nt(qs[(c, s)], state[s]) for s in seqs}
        u = {s: _pmm(t[(c, s)], xs[s][:CHUNK] + akv[(c, s)][:CHUNK]) for s in seqs}
        for s in seqs:
            b, j = s
            y = xs[s][CHUNK:] + akv[(c, s)][CHUNK:] + _pmm(
                jnp.where(incl, ab[(c, s)][CHUNK:], 0.0), u[s])
            y_ref[b, c * CHUNK:(c + 1) * CHUNK, j * LANES:(j + 1) * LANES] = y
        new_state = {}
        for s in seqs:
            uv = jnp.concatenate([_bf(u[s]), tile(v_ref, c, s)], axis=0)
            new_state[s] = state[s] * gcs[(c, s)] + _pmm_tn(uv, bkg[(c, s)])
        state = new_state
    for s in seqs:
        st_ref[s[0], s[1]] = state[s]


def _rwkv_scan(r, k, v, nkk, b, cum, rows_per_step, chunks_per_step):
    bsz, seq, rw = r.shape
    ts = chunks_per_step * CHUNK
    row_spec = pl.BlockSpec((rows_per_step, ts, rw), lambda b, c: (b, c, 0))
    return pl.pallas_call(
        _rwkv_scan_kernel,
        grid=(bsz // rows_per_step, seq // ts),
        in_specs=[row_spec] * 6,
        out_specs=row_spec,
        out_shape=jax.ShapeDtypeStruct((bsz, seq, rw), F32),
        scratch_shapes=[pltpu.VMEM((rows_per_step, rw // LANES, HEAD_DIM, LANES), F32)],
        compiler_params=pltpu.CompilerParams(dimension_semantics=("arbitrary", "arbitrary")),
        name="rwkv_scan",
    )(r, k, v, nkk, b, cum)


def _attn_proj_kernel(x_ref, shift_ref, scale_ref, g_ref, w_ref, cos_ref, sin_ref, *rest):
    outs, buf = rest[:-1], rest[-1]
    SUB_ROWS = ATTN_PROJ_SUB_ROWS
    aw = ATTN_WIDTH
    reps = aw // LANES
    lane = lax.broadcasted_iota(jnp.int32, (SUB_ROWS, aw), 1)
    first_half = _rem(lane, HEAD_DIM) < (HEAD_DIM // 2)
    tiles_per_group = GROUP_WIDTH // LANES

    subs = range(x_ref.shape[1] // SUB_ROWS)
    ps = []
    for sb in subs:
        rows = slice(sb * SUB_ROWS, (sb + 1) * SUB_ROWS)
        h = _norm_modulate(x_ref[0, rows, :], g_ref[...], shift_ref[0], scale_ref[0])
        ps.append(jnp.dot(_bf(h), w_ref[...], preferred_element_type=F32))
    for sb in subs:
        rows = slice(sb * SUB_ROWS, (sb + 1) * SUB_ROWS)
        p = ps[sb]
        cos_t, sin_t = cos_ref[rows, :], sin_ref[rows, :]

        def rope(t, scale):
            cos = jnp.concatenate([cos_t * scale] * reps, axis=-1)
            sin = jnp.concatenate([sin_t * scale] * reps, axis=-1)
            partner = jnp.where(first_half, pltpu.roll(t, aw - HEAD_DIM // 2, axis=1),
                                pltpu.roll(t, HEAD_DIM // 2, axis=1))
            return t * cos + partner * sin

        roped = (rope(p[:, :aw], HEAD_DIM ** -0.5 * LOG2E), rope(p[:, aw:2 * aw], 1.0), p[:, 2 * aw:])
        n = 0
        for which in range(3):
            for g, (_, dil) in enumerate(ATTN_GROUPS):
                o_ref = outs[n]
                n += 1
                out_rows = slice(sb * SUB_ROWS // dil, (sb + 1) * SUB_ROWS // dil)
                for ct in range(tiles_per_group):
                    c0 = g * GROUP_WIDTH + ct * LANES
                    val = roped[which][:, c0:c0 + LANES]
                    if dil == 1:
                        o_ref[0, 0, out_rows, ct * LANES:(ct + 1) * LANES] = _bf(val)
                        continue
                    slot = (which * len(ATTN_GROUPS) + g) * tiles_per_group + ct
                    buf[sb, slot] = val
                    for r in range(dil):
                        o_ref[0, r, out_rows, ct * LANES:(ct + 1) * LANES] = _bf(
                            buf[sb, slot, pl.ds(r, SUB_ROWS // dil, stride=dil), :])


def _attn_proj(x, mod3, g_pre, w_qkv, cos_t, sin_t, tm):
    bsz, seq, d = x.shape
    out_specs, out_shape = [], []
    for _ in range(3):
        for _, dil in ATTN_GROUPS:
            out_specs.append(pl.BlockSpec((1, dil, tm // dil, GROUP_WIDTH), lambda b, i: (b, 0, i, 0)))
            out_shape.append(jax.ShapeDtypeStruct((bsz, dil, seq // dil, GROUP_WIDTH), BF16))
    return pl.pallas_call(
        _attn_proj_kernel,
        grid=(bsz, seq // tm),
        in_specs=[pl.BlockSpec((1, tm, d), lambda b, i: (b, i, 0)),
                  pl.BlockSpec((1, 1, d), lambda b, i: (b, 0, 0)),
                  pl.BlockSpec((1, 1, d), lambda b, i: (b, 0, 1)),
                  pl.BlockSpec((1, d), lambda b, i: (0, 0)),
                  pl.BlockSpec((d, 3 * ATTN_WIDTH), lambda b, i: (0, 0), pipeline_mode=pl.Buffered(1)),
                  pl.BlockSpec((tm, LANES), lambda b, i: (i, 0)),
                  pl.BlockSpec((tm, LANES), lambda b, i: (i, 0))],
        out_specs=out_specs,
        out_shape=out_shape,
        scratch_shapes=[pltpu.VMEM((tm // ATTN_PROJ_SUB_ROWS, 3 * ATTN_WIDTH // LANES,
                                    ATTN_PROJ_SUB_ROWS, LANES), F32)],
        compiler_params=pltpu.CompilerParams(
            dimension_semantics=("arbitrary", "arbitrary"), vmem_limit_bytes=VMEM_LIMIT),
        name="attn_proj",
    )(x, mod3, mod3, g_pre, w_qkv, cos_t, sin_t)


def _attn_kernel(*refs):
    qkv = refs[:9]
    o_ref = refs[9]
    num_ref, m_ref, den_ref = refs[10:]
    blk = ATTN_BLOCK
    qi = lax.broadcasted_iota(jnp.int32, (blk, blk), 0)
    kj = lax.broadcasted_iota(jnp.int32, (blk, blk), 1)
    bias_cur = jnp.where(kj <= qi, 0.0, NEG_BIG)
    bias_prev = jnp.where(kj >= qi, 0.0, NEG_BIG)
    lane = lax.broadcasted_iota(jnp.int32, (blk, LANES), 1)
    head0 = lane < HEAD_DIM

    for g, (_, dil) in enumerate(ATTN_GROUPS):
        q_ref, k_ref, v_ref = qkv[g], qkv[3 + g], qkv[6 + g]
        nb = q_ref.shape[2] // blk

        nper = min(nb, ATTN_BLOCKS_PER_ITER)
        rper = ATTN_BLOCKS_PER_ITER // nper
        whole = nper == nb

        def body(idx, carry, q_ref=q_ref, k_ref=k_ref, v_ref=v_ref, dil=dil, nb=nb, g=g,
                 nper=nper, rper=rper, whole=whole):
            if whole:
                r0, n0 = idx * rper, 0
                base = 0
            else:
                r0 = _div(idx, nb // nper)
                n0 = _rem(idx, nb // nper) * nper
                base = pl.multiple_of(n0 * blk, blk)
            nt = lambda a, b: lax.dot_general(a, b, (((1,), (1,)), ((), ())), preferred_element_type=F32)
            one = jnp.ones((blk, LANES), BF16)
            zero = jnp.zeros((blk, LANES), BF16)
            q_msk, ks, v_aug = {}, {}, {}
            for rr in range(rper):
                r = r0 + rr
                first = 0 if whole else -1
                for i in range(first, nper):
                    if i < 0:
                        st = pl.multiple_of(jnp.maximum(n0 - 1, 0) * blk, blk)
                    else:
                        st = base + i * blk
                    ks[(rr, i)] = k_ref[0, r, pl.ds(st, blk), :]
                    v = v_ref[0, r, pl.ds(st, blk), :]
                    v_aug[(rr, i)] = (jnp.where(head0, v, one), jnp.where(head0, one, v))
                    if i >= 0:
                        q = q_ref[0, r, pl.ds(st, blk), :]
                        q_msk[(rr, i)] = (jnp.where(head0, q, zero), jnp.where(head0, zero, q))
            items = [(rr, i, e) for rr in range(rper) for i in range(nper) for e in range(2)]
            has_prev = lambda it: (it[0], it[1] - 1) in ks
            s_c = {it: nt(q_msk[it[:2]][it[2]], ks[it[:2]]) for it in items}
            s_p = {it: nt(q_msk[it[:2]][it[2]], ks[(it[0], it[1] - 1)]) for it in items if has_prev(it)}
            bias_first = bias_prev if whole else jnp.where(n0 > 0, bias_prev, NEG_BIG)
            ms, pcs, pps = {}, {}, {}
            for it in items:
                sc = s_c[it] + bias_cur
                if has_prev(it):
                    sp = s_p[it] + (bias_first if it[1] == 0 else bias_prev)
                    m = jnp.max(jnp.maximum(sc, sp), axis=-1, keepdims=True)
                    pps[it] = _bf(jnp.exp2(sp - m))
                else:
                    m = jnp.max(sc, axis=-1, keepdims=True)
                ms[it] = m
                pcs[it] = _bf(jnp.exp2(sc - m))
            pv = {}
            for it in items:
                acc = jnp.dot(pcs[it], v_aug[it[:2]][it[2]], preferred_element_type=F32)
                if has_prev(it):
                    acc = acc + jnp.dot(pps[it], v_aug[(it[0], it[1] - 1)][it[2]],
                                        preferred_element_type=F32)
                pv[it] = acc
            for rr in range(rper):
                for i in range(nper):
                    m_b = jnp.where(head0, ms[(rr, i, 0)], ms[(rr, i, 1)])
                    num_b = jnp.where(head0, pv[(rr, i, 0)], pv[(rr, i, 1)])
                    l_b = pltpu.roll(jnp.where(head0, pv[(rr, i, 1)], pv[(rr, i, 0)]), HEAD_DIM, axis=1)
                    if g == 0:
                        rows = pl.ds(base + i * blk, blk)
                        num_ref[rows, :] = num_b
                        m_ref[rows, :] = m_b
                        den_ref[rows, :] = l_b
                    else:
                        rows = pl.ds((n0 + i) * (blk * dil) + r0 + rr, blk, stride=dil)
                        m_old = m_ref[rows, :]
                        m_new = jnp.maximum(m_old, m_b)
                        w_old = jnp.exp2(m_old - m_new)
                        w_b = jnp.exp2(m_b - m_new)
                        num_ref[rows, :] = num_ref[rows, :] * w_old + num_b * w_b
                        den_ref[rows, :] = den_ref[rows, :] * w_old + l_b * w_b
                        m_ref[rows, :] = m_new
            return carry

        lax.fori_loop(0, dil * nb // (nper * rper), body, 0)

    o_ref[0] = _bf(num_ref[...] / den_ref[...])


def _attn(qkv, seq):
    bsz = qkv[0].shape[0]
    in_specs = []
    for _ in range(3):
        for _, dil in ATTN_GROUPS:
            in_specs.append(pl.BlockSpec((1, dil, seq // dil, LANES), lambda b, sp: (b, 0, 0, sp)))
    return pl.pallas_call(
        _attn_kernel,
        grid=(bsz, GROUP_WIDTH // LANES),
        in_specs=in_specs,
        out_specs=pl.BlockSpec((1, seq, LANES), lambda b, sp: (b, 0, sp)),
        out_shape=jax.ShapeDtypeStruct((bsz, seq, GROUP_WIDTH), BF16),
        scratch_shapes=[pltpu.VMEM((seq, LANES), F32)] * 3,
        compiler_params=pltpu.CompilerParams(
            dimension_semantics=("arbitrary", "arbitrary"), vmem_limit_bytes=VMEM_LIMIT),
        name="attn",
    )(*qkv)


def _merge_kernel(x_ref, shift_ref, scale_ref, gatem_ref, g_ref, wg_ref, y_ref, bonus_ref, gate_ref,
                  lnw_ref, lnb_ref, wa_ref, o_ref, wb_ref, wout_ref, gpost_ref, out_ref):
    inv_n = 1.0 / HEAD_DIM
    subs = range(x_ref.shape[1] // SUB_ROWS)
    rows = [slice(sb * SUB_ROWS, (sb + 1) * SUB_ROWS) for sb in subs]
    yb = [jnp.dot(o_ref[0, rows[sb], :], wb_ref[...], preferred_element_type=F32) for sb in subs]
    y = [y_ref[0, rows[sb], :] for sb in subs]
    mean = [_head_sum(y[sb]) * inv_n for sb in subs]
    h = [_bf(_norm_modulate(x_ref[0, rows[sb], :], g_ref[...], shift_ref[0], scale_ref[0])) for sb in subs]
    pg = [jnp.dot(h[sb], wg_ref[...], preferred_element_type=F32) for sb in subs]
    yc = [y[sb] - mean[sb] for sb in subs]
    var = [_head_sum(yc[sb] * yc[sb]) * inv_n for sb in subs]
    pre = [_bf((yc[sb] * lax.rsqrt(var[sb] + GN_EPS) * lnw_ref[...] + lnb_ref[...]
                + bonus_ref[0, rows[sb], :]) * gate_ref[0, rows[sb], :]) for sb in subs]
    ya = [jnp.dot(pre[sb], wa_ref[...], preferred_element_type=F32) for sb in subs]
    merged = [_bf(_sigmoid(pg[sb][:, :D_MODEL]) * ya[sb] + _sigmoid(pg[sb][:, D_MODEL:]) * yb[sb])
              for sb in subs]
    zz = [jnp.dot(merged[sb], wout_ref[...], preferred_element_type=F32) for sb in subs]
    for sb in subs:
        out_ref[0, rows[sb], :] = x_ref[0, rows[sb], :] + gatem_ref[0] * _rms(zz[sb], gpost_ref[...])


def _merge(x, mod3, g_pre, w_g, y, bonus, gate, lnx_w, lnx_b, w_a, o_att, w_b, w_out, g_post, tm):
    bsz, seq, d = x.shape
    rw = RWKV_WIDTH
    const = lambda shape: pl.BlockSpec(shape, lambda b, i: (0, 0), pipeline_mode=pl.Buffered(1))
    row = lambda n: pl.BlockSpec((1, tm, n), lambda b, i: (b, i, 0))
    modv = lambda j: pl.BlockSpec((1, 1, d), lambda b, i: (b, 0, j))
    return pl.pallas_call(
        _merge_kernel,
        grid=(bsz, seq // tm),
        in_specs=[row(d), modv(0), modv(1), modv(2), const((1, d)), const((d, 2 * d)),
                  row(rw), row(rw), row(rw), const((1, rw)), const((1, rw)), const((rw, d)),
                  row(GROUP_WIDTH), const((GROUP_WIDTH, d)), const((d, d)), const((1, d))],
        out_specs=row(d),
        out_shape=jax.ShapeDtypeStruct((bsz, seq, d), F32),
        compiler_params=pltpu.CompilerParams(
            dimension_semantics=("arbitrary", "arbitrary"), vmem_limit_bytes=VMEM_LIMIT),
        name="merge",
    )(x, mod3, mod3, mod3, g_pre, w_g, y, bonus, gate, lnx_w, lnx_b, w_a, o_att, w_b, w_out, g_post)


def _ffn_kernel(x_ref, shift_ref, scale_ref, gatef_ref, g_ref, win_ref, wout_ref, gpost_ref, out_ref):
    dff = wout_ref.shape[0]
    subs = range(x_ref.shape[1] // SUB_ROWS)
    rows = [slice(sb * SUB_ROWS, (sb + 1) * SUB_ROWS) for sb in subs]
    h = [_bf(_norm_modulate(x_ref[0, rows[sb], :], g_ref[...], shift_ref[0], scale_ref[0])) for sb in subs]
    u = [jnp.dot(h[sb], win_ref[:, :dff], preferred_element_type=F32) for sb in subs]
    gt = [jnp.dot(h[sb], win_ref[:, dff:], preferred_element_type=F32) for sb in subs]
    act = [_bf(u[sb] * _sigmoid(u[sb]) * gt[sb]) for sb in subs]
    f = [jnp.dot(act[sb], wout_ref[...], preferred_element_type=F32) for sb in subs]
    for sb in subs:
        out_ref[0, rows[sb], :] = x_ref[0, rows[sb], :] + gatef_ref[0] * _rms(f[sb], gpost_ref[...])


def _ffn(x, mod3, g_pre, w_in, w_out, g_post, tm):
    bsz, seq, d = x.shape
    dff = w_out.shape[0]
    const = lambda shape: pl.BlockSpec(shape, lambda b, i: (0, 0), pipeline_mode=pl.Buffered(1))
    row = pl.BlockSpec((1, tm, d), lambda b, i: (b, i, 0))
    modv = lambda j: pl.BlockSpec((1, 1, d), lambda b, i: (b, 0, j))
    return pl.pallas_call(
        _ffn_kernel,
        grid=(bsz, seq // tm),
        in_specs=[row, modv(3), modv(4), modv(5), const((1, d)), const((d, 2 * dff)),
                  const((dff, d)), const((1, d))],
        out_specs=row,
        out_shape=jax.ShapeDtypeStruct((bsz, seq, d), F32),
        compiler_params=pltpu.CompilerParams(
            dimension_semantics=("arbitrary", "arbitrary"), vmem_limit_bytes=VMEM_LIMIT),
        name="ffn",
    )(x, mod3, mod3, mod3, g_pre, w_in, w_out, g_post)


def _rope_tables(seq):
    half = HEAD_DIM // 2
    inv_freq = ROPE_THETA ** (-jnp.arange(half, dtype=F32) / half)
    ang = jnp.arange(seq, dtype=F32)[:, None] * inv_freq[None, :]
    cos, sin = jnp.cos(ang), jnp.sin(ang)
    reps = LANES // HEAD_DIM
    cos_t = jnp.tile(jnp.concatenate([cos, cos], axis=-1), (1, reps))
    sin_t = jnp.tile(jnp.concatenate([-sin, sin], axis=-1), (1, reps))
    return cos_t, sin_t


def _layer(x, c, w_ada, b_ada, g_pre_mix, g_post_mix, w_in, mu_shift, w0, w2, a0, a2, g2,
           k_k, k_a, r_k, lnx_w, lnx_b, w_a, w_b, w_out, g_pre_ffn, g_post_ffn, w_ffn_in, w_ffn_out):
    bsz, seq, d = x.shape
    rw = RWKV_WIDTH
    row1 = lambda t: t.reshape(1, -1)
    mod3 = _mod(c, w_ada, b_ada).reshape(bsz, 1, 6 * d)

    w_rwkv = _bf(w_in[:, :SHIFT_WIDTH])
    w_qkv = _bf(w_in[:, SHIFT_WIDTH:SHIFT_WIDTH + 3 * ATTN_WIDTH])
    w_gates = _bf(w_in[:, SHIFT_WIDTH + 3 * ATTN_WIDTH:])
    w_lora = jnp.zeros((LORA_WIDTH, 3 * rw), F32)
    w_lora = w_lora.at[:DECAY_LORA, :rw].set(w2)
    w_lora = w_lora.at[DECAY_LORA:DECAY_LORA + ICLR_LORA, rw:2 * rw].set(a2)
    w_lora = _bf(w_lora.at[DECAY_LORA + ICLR_LORA:, 2 * rw:].set(g2))

    prep = _rwkv_prep(x, mod3, row1(g_pre_mix), w_rwkv, row1(mu_shift), w_lora, row1(w0), row1(a0),
                      row1(k_k), row1(k_a), row1(r_k), tm=512)
    r, k, v, nkk, b, cum, bonus, gate = prep
    y = _rwkv_scan(r, k, v, nkk, b, cum, rows_per_step=8, chunks_per_step=1)

    cos_t, sin_t = _rope_tables(seq)
    qkv = _attn_proj(x, mod3, row1(g_pre_mix), w_qkv, cos_t, sin_t, tm=1024)
    o_att = _attn(qkv, seq)

    x1 = _merge(x, mod3, row1(g_pre_mix), w_gates, y, bonus, gate, row1(lnx_w), row1(lnx_b),
                _bf(w_a), o_att, _bf(w_b), _bf(w_out), row1(g_post_mix), tm=512)
    return _ffn(x1, mod3, row1(g_pre_ffn), _bf(w_ffn_in), _bf(w_ffn_out), row1(g_post_ffn), tm=512)


def kernel(x, c, w_ada, b_ada, g_pre_mix, g_post_mix, w_in, mu_shift, w0, w2, a0, a2, g2, k_k, k_a, r_k,
           lnx_w, lnx_b, w_a, w_b, w_out, g_pre_ffn, g_post_ffn, w_ffn_in, w_ffn_out):
    for i in range(w_in.shape[0]):
        x = _layer(x, c, w_ada[i], b_ada[i], g_pre_mix[i], g_post_mix[i], w_in[i], mu_shift[i], w0[i],
                   w2[i], a0[i], a2[i], g2[i], k_k[i], k_a[i], r_k[i], lnx_w[i], lnx_b[i], w_a[i], w_b[i],
                   w_out[i], g_pre_ffn[i], g_post_ffn[i], w_ffn_in[i], w_ffn_out[i])
    return x
```

```python
import functools
import math

import jax
import jax.numpy as jnp
from jax import lax
from jax.experimental import pallas as pl
from jax.experimental.pallas import tpu as pltpu

D_MODEL = 1024
HEAD_DIM = 64
RWKV_HEADS = 12
RWKV_WIDTH = RWKV_HEADS * HEAD_DIM
DECAY_LORA = 64
ICLR_LORA = 64
GATE_LORA = 128
LORA_WIDTH = DECAY_LORA + ICLR_LORA + GATE_LORA
ATTN_GROUPS = ((128, 1), (512, 4), (2048, 16))
HEADS_PER_GROUP = 4
GROUP_WIDTH = HEADS_PER_GROUP * HEAD_DIM
ATTN_WIDTH = len(ATTN_GROUPS) * GROUP_WIDTH
ATTN_BLOCK = 128
ROPE_THETA = 10000.0
SHIFT_WIDTH = 3 * RWKV_WIDTH + LORA_WIDTH
NORM_EPS = 1e-6
GN_EPS = 64e-5

CHUNK = 64
INV_BASE = 16
LANES = 128
NEG_BIG = -1e30
LOG2E = math.log2(math.e)
ATTN_BLOCKS_PER_ITER = 8
SUB_ROWS = 256
ATTN_PROJ_SUB_ROWS = 512
VMEM_LIMIT = 56 * 1024 * 1024

F32 = jnp.float32
BF16 = jnp.bfloat16


def _bf(x):
    return x.astype(BF16)


def _div(x, n):
    assert n & (n - 1) == 0
    return jnp.right_shift(x, n.bit_length() - 1)


def _rem(x, n):
    assert n & (n - 1) == 0
    return jnp.bitwise_and(x, n - 1)


def _sigmoid(x):
    return 1.0 / (1.0 + jnp.exp2(x * (-LOG2E)))


def _split_hi_lo(x):
    hi = _bf(x)
    lo = _bf(x - hi.astype(F32))
    return hi, lo


def _mm_exact_lhs(m_bf, x):
    hi, lo = _split_hi_lo(x)
    return (jnp.dot(m_bf, hi, preferred_element_type=F32)
            + jnp.dot(m_bf, lo, preferred_element_type=F32))


def _head_sum(x):
    tile = 2 * LANES
    ri = _div(lax.broadcasted_iota(jnp.int32, (tile, tile), 0), HEAD_DIM)
    ci = _div(lax.broadcasted_iota(jnp.int32, (tile, tile), 1), HEAD_DIM)
    ones_bd = jnp.where(ri == ci, 1.0, 0.0).astype(BF16)
    xb = _bf(x)
    parts = [jnp.dot(xb[:, j * tile:(j + 1) * tile], ones_bd, preferred_element_type=F32)
             for j in range(x.shape[-1] // tile)]
    return jnp.concatenate(parts, axis=-1)


def _norm_modulate(x, g, shift, scale):
    y = x * lax.rsqrt(jnp.mean(x * x, axis=-1, keepdims=True) + NORM_EPS)
    return y * (g * (1.0 + scale)) + shift


def _rms(x, g):
    return x * lax.rsqrt(jnp.mean(x * x, axis=-1, keepdims=True) + NORM_EPS) * g


def _mod_kernel(c_ref, w_ref, b_ref, o_ref):
    c = c_ref[...]
    hi, lo = _split_hi_lo(c * _sigmoid(c))
    w = _bf(w_ref[...])
    o_ref[...] = (jnp.dot(hi, w, preferred_element_type=F32)
                  + jnp.dot(lo, w, preferred_element_type=F32) + b_ref[...])


def _mod(c, w_ada, b_ada):
    bsz, d = c.shape
    n = w_ada.shape[1]
    tn = d
    return pl.pallas_call(
        _mod_kernel,
        grid=(n // tn,),
        in_specs=[pl.BlockSpec((bsz, d), lambda j: (0, 0)),
                  pl.BlockSpec((d, tn), lambda j: (0, j)),
                  pl.BlockSpec((1, tn), lambda j: (0, j))],
        out_specs=pl.BlockSpec((bsz, tn), lambda j: (0, j)),
        out_shape=jax.ShapeDtypeStruct((bsz, n), F32),
        name="mod",
    )(c, w_ada, b_ada.reshape(1, n))


def _rwkv_prep_kernel(x_ref, shift_ref, scale_ref, g_ref, w_ref, mu_ref, wl_ref, w0_ref, a0_ref,
                      kk_ref, ka_ref, rk_ref,
                      r_o, k_o, v_o, nkk_o, b_o, cum_o, bonus_o, gate_o,
                      carry_ref):
    i = pl.program_id(1)
    rw = RWKV_WIDTH
    sub = SUB_ROWS

    @pl.when(i == 0)
    def _():
        carry_ref[...] = jnp.zeros_like(carry_ref)

    ri = lax.broadcasted_iota(jnp.int32, (sub, sub), 0)
    ci = lax.broadcasted_iota(jnp.int32, (sub, sub), 1)
    same = _div(ri, CHUNK) == _div(ci, CHUNK)
    lower = jnp.where(same & (ci <= ri), 1.0, 0.0).astype(BF16)
    row0 = lax.broadcasted_iota(jnp.int32, (8, SHIFT_WIDTH), 0) == 0
    lane = lax.broadcasted_iota(jnp.int32, (sub, LORA_WIDTH), 1)

    subs = range(x_ref.shape[1] // sub)
    rows = [slice(sb * sub, (sb + 1) * sub) for sb in subs]
    h = [_bf(_norm_modulate(x_ref[0, rows[sb], :], g_ref[...], shift_ref[0], scale_ref[0])) for sb in subs]
    p = [jnp.dot(h[sb], w_ref[...], preferred_element_type=F32) for sb in subs]

    prev_row = carry_ref[7:8, :]
    ps = []
    for sb in subs:
        rolled = pltpu.roll(p[sb], 1, axis=0)
        shifted = jnp.concatenate([jnp.where(row0, prev_row, rolled[:8]), rolled[8:]], axis=0)
        last8 = p[sb][sub - 8:, :]
        prev_row = last8[7:8, :]
        ps.append(p[sb] + mu_ref[...] * (shifted - p[sb]))
    carry_ref[...] = last8

    z = []
    for sb in subs:
        lora_in = ps[sb][:, 3 * rw:]
        z.append(_bf(jnp.where(lane < DECAY_LORA, jnp.tanh(lora_in),
                               jnp.where(lane < DECAY_LORA + ICLR_LORA, lora_in, _sigmoid(lora_in)))))
    lo = [jnp.dot(z[sb], wl_ref[...], preferred_element_type=F32) for sb in subs]

    pr = [ps[sb][:, :rw] for sb in subs]
    pk = [ps[sb][:, rw:2 * rw] for sb in subs]
    pv = [ps[sb][:, 2 * rw:3 * rw] for sb in subs]
    kkr = [pk[sb] * kk_ref[...] for sb in subs]
    kk_sq = [_head_sum(kkr[sb] * kkr[sb]) for sb in subs]
    a = [_sigmoid(a0_ref[...] + lo[sb][:, rw:2 * rw]) for sb in subs]
    k = [pk[sb] * (1.0 + (a[sb] - 1.0) * ka_ref[...]) for sb in subs]
    rk_sum = [_head_sum(pr[sb] * k[sb] * rk_ref[...]) for sb in subs]
    lw = [(-math.exp(-0.5) * LOG2E) * _sigmoid(w0_ref[...] + lo[sb][:, :rw]) for sb in subs]
    cum = [_mm_exact_lhs(lower, lw[sb]) for sb in subs]

    for sb in subs:
        kkn = kkr[sb] * (1.0 / jnp.maximum(jnp.sqrt(kk_sq[sb]), 1e-12))
        gate_o[0, rows[sb], :] = lo[sb][:, 2 * rw:]
        bonus_o[0, rows[sb], :] = rk_sum[sb] * pv[sb]
        cum_o[0, rows[sb], :] = cum[sb]
        r_o[0, rows[sb], :] = _bf(pr[sb])
        k_o[0, rows[sb], :] = _bf(k[sb])
        v_o[0, rows[sb], :] = _bf(pv[sb])
        nkk_o[0, rows[sb], :] = _bf(-kkn)
        b_o[0, rows[sb], :] = _bf(kkn * a[sb])


def _rwkv_prep(x, mod3, g_pre, w_rwkv, mu, w_lora, w0, a0, k_k, k_a, r_k, tm):
    bsz, seq, d = x.shape
    rw = RWKV_WIDTH
    row_spec = pl.BlockSpec((1, tm, rw), lambda b, i: (b, i, 0))
    vec = lambda n: pl.BlockSpec((1, n), lambda b, i: (0, 0))
    big = jax.ShapeDtypeStruct((bsz, seq, rw), F32)
    half = jax.ShapeDtypeStruct((bsz, seq, rw), BF16)
    return pl.pallas_call(
        _rwkv_prep_kernel,
        grid=(bsz, seq // tm),
        in_specs=[pl.BlockSpec((1, tm, d), lambda b, i: (b, i, 0)),
                  pl.BlockSpec((1, 1, d), lambda b, i: (b, 0, 0)),
                  pl.BlockSpec((1, 1, d), lambda b, i: (b, 0, 1)),
                  vec(d),
                  pl.BlockSpec((d, SHIFT_WIDTH), lambda b, i: (0, 0), pipeline_mode=pl.Buffered(1)),
                  vec(SHIFT_WIDTH),
                  pl.BlockSpec((LORA_WIDTH, 3 * rw), lambda b, i: (0, 0), pipeline_mode=pl.Buffered(1)),
                  vec(rw), vec(rw), vec(rw), vec(rw), vec(rw)],
        out_specs=[row_spec] * 8,
        out_shape=[half] * 5 + [big] * 3,
        scratch_shapes=[pltpu.VMEM((8, SHIFT_WIDTH), F32)],
        compiler_params=pltpu.CompilerParams(
            dimension_semantics=("arbitrary", "arbitrary"), vmem_limit_bytes=VMEM_LIMIT),
        name="rwkv_prep",
    )(x, mod3, mod3, g_pre, w_rwkv, mu, w_lora, w0, a0, k_k, k_a, r_k)


def _pair_blockdiag(x):
    first = lax.broadcasted_iota(jnp.int32, x.shape, 1) < HEAD_DIM
    zero = jnp.zeros_like(x)
    return jnp.concatenate([jnp.where(first, x, zero), jnp.where(first, zero, x)], axis=0)


def _pmm(a, b):
    return jnp.dot(_bf(a), _pair_blockdiag(_bf(b)), preferred_element_type=F32)


def _pmm_nt(a, b):
    return lax.dot_general(_bf(a), _pair_blockdiag(_bf(b)), (((1,), (1,)), ((), ())),
                           preferred_element_type=F32)


def _pmm_tn(a, b):
    full = lax.dot_general(_bf(a), _bf(b), (((0,), (0,)), ((), ())), preferred_element_type=F32)
    first = lax.broadcasted_iota(jnp.int32, (HEAD_DIM, LANES), 1) < HEAD_DIM
    return jnp.where(first, full[:HEAD_DIM], full[HEAD_DIM:])


def _unit_lower_inverse(a_list, masks):
    eye, diag_mask, merge_masks = masks
    ps = [jnp.where(diag_mask, a, 0.0) for a in a_list]
    ts = [eye + p for p in ps]
    ps = [_pmm(p, p) for p in ps]
    m = 2
    while 2 * m < INV_BASE:
        both = [_pmm(jnp.concatenate([p, t], axis=0), p) for p, t in zip(ps, ts)]
        ps = [x[:CHUNK] for x in both]
        ts = [t + x[CHUNK:] for t, x in zip(ts, both)]
        m *= 2
    ts = [t + _pmm(t, p) for t, p in zip(ts, ps)]
    for mk in merge_masks:
        ws = [_pmm(jnp.where(mk, a, 0.0), t) for a, t in zip(a_list, ts)]
        ts = [t + _pmm(t, w) for t, w in zip(ts, ws)]
    return ts


def _rwkv_scan_kernel(r_ref, k_ref, v_ref, nkk_ref, b_ref, cum_ref, y_ref, st_ref):
    @pl.when(pl.program_id(1) == 0)
    def _():
        st_ref[...] = jnp.zeros_like(st_ref)

    ri = lax.broadcasted_iota(jnp.int32, (CHUNK, LANES), 0)
    ci = _rem(lax.broadcasted_iota(jnp.int32, (CHUNK, LANES), 1), HEAD_DIM)
    strict = ri > ci
    incl = ri >= ci
    ri2 = lax.broadcasted_iota(jnp.int32, (2 * CHUNK, LANES), 0)
    ci2 = _rem(lax.broadcasted_iota(jnp.int32, (2 * CHUNK, LANES), 1), HEAD_DIM)
    rloc = _rem(ri2, CHUNK)
    stacked = (rloc > ci2) | ((rloc == ci2) & (ri2 >= CHUNK))
    eye = jnp.where(ri == ci, 1.0, 0.0)
    diag_mask = _div(ri, INV_BASE) == _div(ci, INV_BASE)
    merge_masks = []
    bsize = INV_BASE
    while bsize < CHUNK:
        merge_masks.append((_div(ri, 2 * bsize) == _div(ci, 2 * bsize))
                           & (_div(ri, bsize) != _div(ci, bsize)))
        bsize *= 2
    masks = (eye, diag_mask, merge_masks)

    n_rows = r_ref.shape[0]
    n_chunks = r_ref.shape[1] // CHUNK
    n_pairs = RWKV_WIDTH // LANES
    seqs = [(b, j) for b in range(n_rows) for j in range(n_pairs)]
    items = [(c, s) for c in range(n_chunks) for s in seqs]

    def tile(ref, c, s):
        b, j = s
        return ref[b, c * CHUNK:(c + 1) * CHUNK, j * LANES:(j + 1) * LANES]

    first_row = ri == 0
    qs, bk, kd, bkg, gcs = {}, {}, {}, {}, {}
    for it in items:
        cum = tile(cum_ref, *it)
        cum_last = cum[CHUNK - 1:CHUNK, :]
        e_cum = jnp.exp2(cum)
        e_prev = jnp.exp2(jnp.where(first_row, 0.0, pltpu.roll(cum, 1, axis=0)))
        e_inv = jnp.exp2(-cum)
        e_rest = jnp.exp2(cum_last - cum)
        gcs[it] = jnp.exp2(cum_last)
        r, k = tile(r_ref, *it).astype(F32), tile(k_ref, *it).astype(F32)
        nkk, b = tile(nkk_ref, *it).astype(F32), tile(b_ref, *it).astype(F32)
        qs[it] = _bf(jnp.concatenate([nkk * e_prev, r * e_cum], axis=0))
        bk[it], kd[it] = _bf(b * e_inv), _bf(k * e_inv)
        bkg[it] = _bf(jnp.concatenate([b * e_rest, k * e_rest], axis=0))

    nt = lambda a, b: lax.dot_general(a, b, (((1,), (1,)), ((), ())), preferred_element_type=F32)
    abk = {it: nt(qs[it], jnp.concatenate([_pair_blockdiag(bk[it]), _pair_blockdiag(kd[it])], axis=0))
           for it in items}
    ab = {it: abk[it][:, :LANES] for it in items}
    ak = {it: jnp.where(stacked, abk[it][:, LANES:], 0.0) for it in items}
    akv = {it: _pmm(ak[it], tile(v_ref, *it)) for it in items}
    t_list = _unit_lower_inverse([jnp.where(strict, ab[it][:CHUNK], 0.0) for it in items], masks)
    t = dict(zip(items, t_list))

    state = {s: st_ref[s[0], s[1]] for s in seqs}
    for c in range(n_chunks):
        xs = {s: _pmm_nt(qs[(c, s)], state[s]) for s in seqs}
        u = {s: _pmm(t[(c, s)], xs[s][:CHUNK] + akv[(c, s)][:CHUNK]) for s in seqs}
        for s in seqs:
            b, j = s
            y = xs[s][CHUNK:] + akv[(c, s)][CHUNK:] + _pmm(
                jnp.where(incl, ab[(c, s)][CHUNK:], 0.0), u[s])
            y_ref[b, c * CHUNK:(c + 1) * CHUNK, j * LANES:(j + 1) * LANES] = y
        new_state = {}
        for s in seqs:
            uv = jnp.concatenate([_bf(u[s]), tile(v_ref, c, s)], axis=0)
            new_state[s] = state[s] * gcs[(c, s)] + _pmm_tn(uv, bkg[(c, s)])
        state = new_state
    for s in seqs:
        st_ref[s[0], s[1]] = state[s]


def _rwkv_scan(r, k, v, nkk, b, cum, rows_per_step, chunks_per_step):
    bsz, seq, rw = r.shape
    ts = chunks_per_step * CHUNK
    row_spec = pl.BlockSpec((rows_per_step, ts, rw), lambda b, c: (b, c, 0))
    return pl.pallas_call(
        _rwkv_scan_kernel,
        grid=(bsz // rows_per_step, seq // ts),
        in_specs=[row_spec] * 6,
        out_specs=row_spec,
        out_shape=jax.ShapeDtypeStruct((bsz, seq, rw), F32),
        scratch_shapes=[pltpu.VMEM((rows_per_step, rw // LANES, HEAD_DIM, LANES), F32)],
        compiler_params=pltpu.CompilerParams(dimension_semantics=("arbitrary", "arbitrary")),
        name="rwkv_scan",
    )(r, k, v, nkk, b, cum)


def _attn_proj_kernel(x_ref, shift_ref, scale_ref, g_ref, w_ref, cos_ref, sin_ref, *rest):
    outs, buf = rest[:-1], rest[-1]
    SUB_ROWS = ATTN_PROJ_SUB_ROWS
    aw = ATTN_WIDTH
    reps = aw // LANES
    lane = lax.broadcasted_iota(jnp.int32, (SUB_ROWS, aw), 1)
    first_half = _rem(lane, HEAD_DIM) < (HEAD_DIM // 2)
    tiles_per_group = GROUP_WIDTH // LANES

    subs = range(x_ref.shape[1] // SUB_ROWS)
    ps = []
    for sb in subs:
        rows = slice(sb * SUB_ROWS, (sb + 1) * SUB_ROWS)
        h = _norm_modulate(x_ref[0, rows, :], g_ref[...], shift_ref[0], scale_ref[0])
        ps.append(jnp.dot(_bf(h), w_ref[...], preferred_element_type=F32))
    for sb in subs:
        rows = slice(sb * SUB_ROWS, (sb + 1) * SUB_ROWS)
        p = ps[sb]
        cos_t, sin_t = cos_ref[rows, :], sin_ref[rows, :]

        def rope(t, scale):
            cos = jnp.concatenate([cos_t * scale] * reps, axis=-1)
            sin = jnp.concatenate([sin_t * scale] * reps, axis=-1)
            partner = jnp.where(first_half, pltpu.roll(t, aw - HEAD_DIM // 2, axis=1),
                                pltpu.roll(t, HEAD_DIM // 2, axis=1))
            return t * cos + partner * sin

        roped = (rope(p[:, :aw], HEAD_DIM ** -0.5 * LOG2E), rope(p[:, aw:2 * aw], 1.0), p[:, 2 * aw:])
        n = 0
        for which in range(3):
            for g, (_, dil) in enumerate(ATTN_GROUPS):
                o_ref = outs[n]
                n += 1
                out_rows = slice(sb * SUB_ROWS // dil, (sb + 1) * SUB_ROWS // dil)
                for ct in range(tiles_per_group):
                    c0 = g * GROUP_WIDTH + ct * LANES
                    val = roped[which][:, c0:c0 + LANES]
                    if dil == 1:
                        o_ref[0, 0, out_rows, ct * LANES:(ct + 1) * LANES] = _bf(val)
                        continue
                    slot = (which * len(ATTN_GROUPS) + g) * tiles_per_group + ct
                    buf[sb, slot] = val
                    for r in range(dil):
                        o_ref[0, r, out_rows, ct * LANES:(ct + 1) * LANES] = _bf(
                            buf[sb, slot, pl.ds(r, SUB_ROWS // dil, stride=dil), :])


def _attn_proj(x, mod3, g_pre, w_qkv, cos_t, sin_t, tm):
    bsz, seq, d = x.shape
    out_specs, out_shape = [], []
    for _ in range(3):
        for _, dil in ATTN_GROUPS:
            out_specs.append(pl.BlockSpec((1, dil, tm // dil, GROUP_WIDTH), lambda b, i: (b, 0, i, 0)))
            out_shape.append(jax.ShapeDtypeStruct((bsz, dil, seq // dil, GROUP_WIDTH), BF16))
    return pl.pallas_call(
        _attn_proj_kernel,
        grid=(bsz, seq // tm),
        in_specs=[pl.BlockSpec((1, tm, d), lambda b, i: (b, i, 0)),
                  pl.BlockSpec((1, 1, d), lambda b, i: (b, 0, 0)),
                  pl.BlockSpec((1, 1, d), lambda b, i: (b, 0, 1)),
                  pl.BlockSpec((1, d), lambda b, i: (0, 0)),
                  pl.BlockSpec((d, 3 * ATTN_WIDTH), lambda b, i: (0, 0), pipeline_mode=pl.Buffered(1)),
                  pl.BlockSpec((tm, LANES), lambda b, i: (i, 0)),
                  pl.BlockSpec((tm, LANES), lambda b, i: (i, 0))],
        out_specs=out_specs,
        out_shape=out_shape,
        scratch_shapes=[pltpu.VMEM((tm // ATTN_PROJ_SUB_ROWS, 3 * ATTN_WIDTH // LANES,
                                    ATTN_PROJ_SUB_ROWS, LANES), F32)],
        compiler_params=pltpu.CompilerParams(
            dimension_semantics=("arbitrary", "arbitrary"), vmem_limit_bytes=VMEM_LIMIT),
        name="attn_proj",
    )(x, mod3, mod3, g_pre, w_qkv, cos_t, sin_t)


def _attn_kernel(*refs):
    qkv = refs[:9]
    o_ref = refs[9]
    num_ref, m_ref, den_ref = refs[10:]
    blk = ATTN_BLOCK
    qi = lax.broadcasted_iota(jnp.int32, (blk, blk), 0)
    kj = lax.broadcasted_iota(jnp.int32, (blk, blk), 1)
    bias_cur = jnp.where(kj <= qi, 0.0, NEG_BIG)
    bias_prev = jnp.where(kj >= qi, 0.0, NEG_BIG)
    lane = lax.broadcasted_iota(jnp.int32, (blk, LANES), 1)
    head0 = lane < HEAD_DIM

    order = sorted(range(len(ATTN_GROUPS)), key=lambda g: -ATTN_GROUPS[g][1])
    assert ATTN_GROUPS[order[-1]][1] == 1
    for g in order:
        dil = ATTN_GROUPS[g][1]
        q_ref, k_ref, v_ref = qkv[g], qkv[3 + g], qkv[6 + g]
        nb = q_ref.shape[2] // blk

        nper = min(nb, ATTN_BLOCKS_PER_ITER)
        rper = ATTN_BLOCKS_PER_ITER // nper
        whole = nper == nb

        def body(idx, carry, q_ref=q_ref, k_ref=k_ref, v_ref=v_ref, dil=dil, nb=nb, g=g,
                 nper=nper, rper=rper, whole=whole):
            if whole:
                r0, n0 = idx * rper, 0
                base = 0
            else:
                r0 = _div(idx, nb // nper)
                n0 = _rem(idx, nb // nper) * nper
                base = pl.multiple_of(n0 * blk, blk)
            nt = lambda a, b: lax.dot_general(a, b, (((1,), (1,)), ((), ())), preferred_element_type=F32)
            one = jnp.ones((blk, LANES), BF16)
            zero = jnp.zeros((blk, LANES), BF16)
            q_msk, ks, v_aug = {}, {}, {}
            for rr in range(rper):
                r = r0 + rr
                first = 0 if whole else -1
                for i in range(first, nper):
                    if i < 0:
                        st = pl.multiple_of(jnp.maximum(n0 - 1, 0) * blk, blk)
                    else:
                        st = base + i * blk
                    ks[(rr, i)] = k_ref[0, r, pl.ds(st, blk), :]
                    v = v_ref[0, r, pl.ds(st, blk), :]
                    v_aug[(rr, i)] = (jnp.where(head0, v, one), jnp.where(head0, one, v))
                    if i >= 0:
                        q = q_ref[0, r, pl.ds(st, blk), :]
                        q_msk[(rr, i)] = (jnp.where(head0, q, zero), jnp.where(head0, zero, q))
            items = [(rr, i, e) for rr in range(rper) for i in range(nper) for e in range(2)]
            has_prev = lambda it: (it[0], it[1] - 1) in ks
            s_c = {it: nt(q_msk[it[:2]][it[2]], ks[it[:2]]) for it in items}
            s_p = {it: nt(q_msk[it[:2]][it[2]], ks[(it[0], it[1] - 1)]) for it in items if has_prev(it)}
            bias_first = bias_prev if whole else jnp.where(n0 > 0, bias_prev, NEG_BIG)
            ms, pcs, pps = {}, {}, {}
            for it in items:
                sc = s_c[it] + bias_cur
                if has_prev(it):
                    sp = s_p[it] + (bias_first if it[1] == 0 else bias_prev)
                    m = jnp.max(jnp.maximum(sc, sp), axis=-1, keepdims=True)
                    pps[it] = _bf(jnp.exp2(sp - m))
                else:
                    m = jnp.max(sc, axis=-1, keepdims=True)
                ms[it] = m
                pcs[it] = _bf(jnp.exp2(sc - m))
            pv = {}
            for it in items:
                acc = jnp.dot(pcs[it], v_aug[it[:2]][it[2]], preferred_element_type=F32)
                if has_prev(it):
                    acc = acc + jnp.dot(pps[it], v_aug[(it[0], it[1] - 1)][it[2]],
                                        preferred_element_type=F32)
                pv[it] = acc
            for rr in range(rper):
                for i in range(nper):
                    m_b = jnp.where(head0, ms[(rr, i, 0)], ms[(rr, i, 1)])
                    num_b = jnp.where(head0, pv[(rr, i, 0)], pv[(rr, i, 1)])
                    l_b = pltpu.roll(jnp.where(head0, pv[(rr, i, 1)], pv[(rr, i, 0)]), HEAD_DIM, axis=1)
                    if dil == 1:
                        rows = pl.ds(base + i * blk, blk)
                    else:
                        rows = pl.ds((n0 + i) * (blk * dil) + r0 + rr, blk, stride=dil)
                    if g == order[0]:
                        num_ref[rows, :] = num_b
                        m_ref[rows, :] = m_b
                        den_ref[rows, :] = l_b
                        continue
                    m_old = m_ref[rows, :]
                    m_new = jnp.maximum(m_old, m_b)
                    w_old = jnp.exp2(m_old - m_new)
                    w_b = jnp.exp2(m_b - m_new)
                    num = num_ref[rows, :] * w_old + num_b * w_b
                    den = den_ref[rows, :] * w_old + l_b * w_b
                    if g == order[-1]:
                        o_ref[0, rows, :] = _bf(num / den)
                    else:
                        num_ref[rows, :] = num
                        den_ref[rows, :] = den
                        m_ref[rows, :] = m_new
            return carry

        lax.fori_loop(0, dil * nb // (nper * rper), body, 0, unroll=2)


def _attn(qkv, seq):
    bsz = qkv[0].shape[0]
    in_specs = []
    for _ in range(3):
        for _, dil in ATTN_GROUPS:
            in_specs.append(pl.BlockSpec((1, dil, seq // dil, LANES), lambda b, sp: (b, 0, 0, sp)))
    return pl.pallas_call(
        _attn_kernel,
        grid=(bsz, GROUP_WIDTH // LANES),
        in_specs=in_specs,
        out_specs=pl.BlockSpec((1, seq, LANES), lambda b, sp: (b, 0, sp)),
        out_shape=jax.ShapeDtypeStruct((bsz, seq, GROUP_WIDTH), BF16),
        scratch_shapes=[pltpu.VMEM((seq, LANES), F32)] * 3,
        compiler_params=pltpu.CompilerParams(
            dimension_semantics=("arbitrary", "arbitrary"), vmem_limit_bytes=VMEM_LIMIT),
        name="attn",
    )(*qkv)


def _merge_kernel(x_ref, shift_ref, scale_ref, gatem_ref, g_ref, wg_ref, y_ref, bonus_ref, gate_ref,
                  lnw_ref, lnb_ref, wa_ref, o_ref, wb_ref, wout_ref, gpost_ref, out_ref):
    inv_n = 1.0 / HEAD_DIM
    subs = range(x_ref.shape[1] // SUB_ROWS)
    rows = [slice(sb * SUB_ROWS, (sb + 1) * SUB_ROWS) for sb in subs]
    yb = [jnp.dot(o_ref[0, rows[sb], :], wb_ref[...], preferred_element_type=F32) for sb in subs]
    y = [y_ref[0, rows[sb], :] for sb in subs]
    mean = [_head_sum(y[sb]) * inv_n for sb in subs]
    h = [_bf(_norm_modulate(x_ref[0, rows[sb], :], g_ref[...], shift_ref[0], scale_ref[0])) for sb in subs]
    pg = [jnp.dot(h[sb], wg_ref[...], preferred_element_type=F32) for sb in subs]
    yc = [y[sb] - mean[sb] for sb in subs]
    var = [_head_sum(yc[sb] * yc[sb]) * inv_n for sb in subs]
    pre = [_bf((yc[sb] * lax.rsqrt(var[sb] + GN_EPS) * lnw_ref[...] + lnb_ref[...]
                + bonus_ref[0, rows[sb], :]) * gate_ref[0, rows[sb], :]) for sb in subs]
    ya = [jnp.dot(pre[sb], wa_ref[...], preferred_element_type=F32) for sb in subs]
    merged = [_bf(_sigmoid(pg[sb][:, :D_MODEL]) * ya[sb] + _sigmoid(pg[sb][:, D_MODEL:]) * yb[sb])
              for sb in subs]
    zz = [jnp.dot(merged[sb], wout_ref[...], preferred_element_type=F32) for sb in subs]
    for sb in subs:
        out_ref[0, rows[sb], :] = x_ref[0, rows[sb], :] + gatem_ref[0] * _rms(zz[sb], gpost_ref[...])


def _merge(x, mod3, g_pre, w_g, y, bonus, gate, lnx_w, lnx_b, w_a, o_att, w_b, w_out, g_post, tm):
    bsz, seq, d = x.shape
    rw = RWKV_WIDTH
    const = lambda shape: pl.BlockSpec(shape, lambda b, i: (0, 0), pipeline_mode=pl.Buffered(1))
    row = lambda n: pl.BlockSpec((1, tm, n), lambda b, i: (b, i, 0))
    modv = lambda j: pl.BlockSpec((1, 1, d), lambda b, i: (b, 0, j))
    return pl.pallas_call(
        _merge_kernel,
        grid=(bsz, seq // tm),
        in_specs=[row(d), modv(0), modv(1), modv(2), const((1, d)), const((d, 2 * d)),
                  row(rw), row(rw), row(rw), const((1, rw)), const((1, rw)), const((rw, d)),
                  row(GROUP_WIDTH), const((GROUP_WIDTH, d)), const((d, d)), const((1, d))],
        out_specs=row(d),
        out_shape=jax.ShapeDtypeStruct((bsz, seq, d), F32),
        compiler_params=pltpu.CompilerParams(
            dimension_semantics=("arbitrary", "arbitrary"), vmem_limit_bytes=VMEM_LIMIT),
        name="merge",
    )(x, mod3, mod3, mod3, g_pre, w_g, y, bonus, gate, lnx_w, lnx_b, w_a, o_att, w_b, w_out, g_post)


def _ffn_kernel(x_ref, shift_ref, scale_ref, gatef_ref, g_ref, win_ref, wout_ref, gpost_ref, out_ref):
    dff = wout_ref.shape[0]
    subs = range(x_ref.shape[1] // SUB_ROWS)
    rows = [slice(sb * SUB_ROWS, (sb + 1) * SUB_ROWS) for sb in subs]
    h = [_bf(_norm_modulate(x_ref[0, rows[sb], :], g_ref[...], shift_ref[0], scale_ref[0])) for sb in subs]
    u = [jnp.dot(h[sb], win_ref[:, :dff], preferred_element_type=F32) for sb in subs]
    gt = [jnp.dot(h[sb], win_ref[:, dff:], preferred_element_type=F32) for sb in subs]
    act = [_bf(u[sb] * _sigmoid(u[sb]) * gt[sb]) for sb in subs]
    f = [jnp.dot(act[sb], wout_ref[...], preferred_element_type=F32) for sb in subs]
    for sb in subs:
        out_ref[0, rows[sb], :] = x_ref[0, rows[sb], :] + gatef_ref[0] * _rms(f[sb], gpost_ref[...])


def _ffn(x, mod3, g_pre, w_in, w_out, g_post, tm):
    bsz, seq, d = x.shape
    dff = w_out.shape[0]
    const = lambda shape: pl.BlockSpec(shape, lambda b, i: (0, 0), pipeline_mode=pl.Buffered(1))
    row = pl.BlockSpec((1, tm, d), lambda b, i: (b, i, 0))
    modv = lambda j: pl.BlockSpec((1, 1, d), lambda b, i: (b, 0, j))
    return pl.pallas_call(
        _ffn_kernel,
        grid=(bsz, seq // tm),
        in_specs=[row, modv(3), modv(4), modv(5), const((1, d)), const((d, 2 * dff)),
                  const((dff, d)), const((1, d))],
        out_specs=row,
        out_shape=jax.ShapeDtypeStruct((bsz, seq, d), F32),
        compiler_params=pltpu.CompilerParams(
            dimension_semantics=("arbitrary", "arbitrary"), vmem_limit_bytes=VMEM_LIMIT),
        name="ffn",
    )(x, mod3, mod3, mod3, g_pre, w_in, w_out, g_post)


def _rope_tables(seq):
    half = HEAD_DIM // 2
    inv_freq = ROPE_THETA ** (-jnp.arange(half, dtype=F32) / half)
    ang = jnp.arange(seq, dtype=F32)[:, None] * inv_freq[None, :]
    cos, sin = jnp.cos(ang), jnp.sin(ang)
    reps = LANES // HEAD_DIM
    cos_t = jnp.tile(jnp.concatenate([cos, cos], axis=-1), (1, reps))
    sin_t = jnp.tile(jnp.concatenate([-sin, sin], axis=-1), (1, reps))
    return cos_t, sin_t


def _layer(x, c, w_ada, b_ada, g_pre_mix, g_post_mix, w_in, mu_shift, w0, w2, a0, a2, g2,
           k_k, k_a, r_k, lnx_w, lnx_b, w_a, w_b, w_out, g_pre_ffn, g_post_ffn, w_ffn_in, w_ffn_out):
    bsz, seq, d = x.shape
    rw = RWKV_WIDTH
    row1 = lambda t: t.reshape(1, -1)
    mod3 = _mod(c, w_ada, b_ada).reshape(bsz, 1, 6 * d)

    w_rwkv = _bf(w_in[:, :SHIFT_WIDTH])
    w_qkv = _bf(w_in[:, SHIFT_WIDTH:SHIFT_WIDTH + 3 * ATTN_WIDTH])
    w_gates = _bf(w_in[:, SHIFT_WIDTH + 3 * ATTN_WIDTH:])
    w_lora = jnp.zeros((LORA_WIDTH, 3 * rw), F32)
    w_lora = w_lora.at[:DECAY_LORA, :rw].set(w2)
    w_lora = w_lora.at[DECAY_LORA:DECAY_LORA + ICLR_LORA, rw:2 * rw].set(a2)
    w_lora = _bf(w_lora.at[DECAY_LORA + ICLR_LORA:, 2 * rw:].set(g2))

    prep = _rwkv_prep(x, mod3, row1(g_pre_mix), w_rwkv, row1(mu_shift), w_lora, row1(w0), row1(a0),
                      row1(k_k), row1(k_a), row1(r_k), tm=512)
    r, k, v, nkk, b, cum, bonus, gate = prep
    y = _rwkv_scan(r, k, v, nkk, b, cum, rows_per_step=8, chunks_per_step=1)

    cos_t, sin_t = _rope_tables(seq)
    qkv = _attn_proj(x, mod3, row1(g_pre_mix), w_qkv, cos_t, sin_t, tm=1024)
    o_att = _attn(qkv, seq)

    x1 = _merge(x, mod3, row1(g_pre_mix), w_gates, y, bonus, gate, row1(lnx_w), row1(lnx_b),
                _bf(w_a), o_att, _bf(w_b), _bf(w_out), row1(g_post_mix), tm=512)
    return _ffn(x1, mod3, row1(g_pre_ffn), _bf(w_ffn_in), _bf(w_ffn_out), row1(g_post_ffn), tm=512)


def kernel(x, c, w_ada, b_ada, g_pre_mix, g_post_mix, w_in, mu_shift, w0, w2, a0, a2, g2, k_k, k_a, r_k,
           lnx_w, lnx_b, w_a, w_b, w_out, g_pre_ffn, g_post_ffn, w_ffn_in, w_ffn_out):
    for i in range(w_in.shape[0]):
        x = _layer(x, c, w_ada[i], b_ada[i], g_pre_mix[i], g_post_mix[i], w_in[i], mu_shift[i], w0[i],
                   w2[i], a0[i], a2[i], g2[i], k_k[i], k_a[i], r_k[i], lnx_w[i], lnx_b[i], w_a[i], w_b[i],
                   w_out[i], g_pre_ffn[i], g_post_ffn[i], w_ffn_in[i], w_ffn_out[i])
    return x
```

```python
import functools
import math

import jax
import jax.numpy as jnp
from jax import lax
from jax.experimental import pallas as pl
from jax.experimental.pallas import tpu as pltpu

D_MODEL = 1024
HEAD_DIM = 64
RWKV_HEADS = 12
RWKV_WIDTH = RWKV_HEADS * HEAD_DIM
DECAY_LORA = 64
ICLR_LORA = 64
GATE_LORA = 128
LORA_WIDTH = DECAY_LORA + ICLR_LORA + GATE_LORA
ATTN_GROUPS = ((128, 1), (512, 4), (2048, 16))
HEADS_PER_GROUP = 4
GROUP_WIDTH = HEADS_PER_GROUP * HEAD_DIM
ATTN_WIDTH = len(ATTN_GROUPS) * GROUP_WIDTH
ATTN_BLOCK = 128
ROPE_THETA = 10000.0
SHIFT_WIDTH = 3 * RWKV_WIDTH + LORA_WIDTH
NORM_EPS = 1e-6
GN_EPS = 64e-5

CHUNK = 64
INV_BASE = 16
LANES = 128
NEG_BIG = -1e30
LOG2E = math.log2(math.e)
ATTN_BLOCKS_PER_ITER = 8
SUB_ROWS = 256
ATTN_PROJ_SUB_ROWS = 512
VMEM_LIMIT = 56 * 1024 * 1024

F32 = jnp.float32
BF16 = jnp.bfloat16


def _bf(x):
    return x.astype(BF16)


def _div(x, n):
    assert n & (n - 1) == 0
    return jnp.right_shift(x, n.bit_length() - 1)


def _rem(x, n):
    assert n & (n - 1) == 0
    return jnp.bitwise_and(x, n - 1)


def _sigmoid(x):
    return 1.0 / (1.0 + jnp.exp2(x * (-LOG2E)))


def _split_hi_lo(x):
    hi = _bf(x)
    lo = _bf(x - hi.astype(F32))
    return hi, lo


def _mm_exact_lhs(m_bf, x):
    hi, lo = _split_hi_lo(x)
    return (jnp.dot(m_bf, hi, preferred_element_type=F32)
            + jnp.dot(m_bf, lo, preferred_element_type=F32))


def _head_sum(x):
    tile = 2 * LANES
    ri = _div(lax.broadcasted_iota(jnp.int32, (tile, tile), 0), HEAD_DIM)
    ci = _div(lax.broadcasted_iota(jnp.int32, (tile, tile), 1), HEAD_DIM)
    ones_bd = jnp.where(ri == ci, 1.0, 0.0).astype(BF16)
    xb = _bf(x)
    parts = [jnp.dot(xb[:, j * tile:(j + 1) * tile], ones_bd, preferred_element_type=F32)
             for j in range(x.shape[-1] // tile)]
    return jnp.concatenate(parts, axis=-1)


def _norm_modulate(x, g, shift, scale):
    y = x * lax.rsqrt(jnp.mean(x * x, axis=-1, keepdims=True) + NORM_EPS)
    return y * (g * (1.0 + scale)) + shift


def _rms(x, g):
    return x * lax.rsqrt(jnp.mean(x * x, axis=-1, keepdims=True) + NORM_EPS) * g


def _mod_kernel(c_ref, w_ref, b_ref, o_ref):
    c = c_ref[...]
    hi, lo = _split_hi_lo(c * _sigmoid(c))
    w = _bf(w_ref[...])
    o_ref[...] = (jnp.dot(hi, w, preferred_element_type=F32)
                  + jnp.dot(lo, w, preferred_element_type=F32) + b_ref[...])


def _mod(c, w_ada, b_ada):
    bsz, d = c.shape
    n = w_ada.shape[1]
    tn = d
    return pl.pallas_call(
        _mod_kernel,
        grid=(n // tn,),
        in_specs=[pl.BlockSpec((bsz, d), lambda j: (0, 0)),
                  pl.BlockSpec((d, tn), lambda j: (0, j)),
                  pl.BlockSpec((1, tn), lambda j: (0, j))],
        out_specs=pl.BlockSpec((bsz, tn), lambda j: (0, j)),
        out_shape=jax.ShapeDtypeStruct((bsz, n), F32),
        name="mod",
    )(c, w_ada, b_ada.reshape(1, n))


def _rwkv_prep_kernel(x_ref, shift_ref, scale_ref, g_ref, w_ref, mu_ref, wl_ref, w0_ref, a0_ref,
                      kk_ref, ka_ref, rk_ref,
                      r_o, k_o, v_o, nkk_o, b_o, cum_o, bonus_o, gate_o,
                      carry_ref):
    i = pl.program_id(1)
    rw = RWKV_WIDTH
    sub = SUB_ROWS

    @pl.when(i == 0)
    def _():
        carry_ref[...] = jnp.zeros_like(carry_ref)

    ri = lax.broadcasted_iota(jnp.int32, (sub, sub), 0)
    ci = lax.broadcasted_iota(jnp.int32, (sub, sub), 1)
    same = _div(ri, CHUNK) == _div(ci, CHUNK)
    lower = jnp.where(same & (ci <= ri), 1.0, 0.0).astype(BF16)
    row0 = lax.broadcasted_iota(jnp.int32, (8, SHIFT_WIDTH), 0) == 0
    lane = lax.broadcasted_iota(jnp.int32, (sub, LORA_WIDTH), 1)

    subs = range(x_ref.shape[1] // sub)
    rows = [slice(sb * sub, (sb + 1) * sub) for sb in subs]
    h = [_bf(_norm_modulate(x_ref[0, rows[sb], :], g_ref[...], shift_ref[0], scale_ref[0])) for sb in subs]
    p = [jnp.dot(h[sb], w_ref[...], preferred_element_type=F32) for sb in subs]

    prev_row = carry_ref[7:8, :]
    ps = []
    for sb in subs:
        rolled = pltpu.roll(p[sb], 1, axis=0)
        shifted = jnp.concatenate([jnp.where(row0, prev_row, rolled[:8]), rolled[8:]], axis=0)
        last8 = p[sb][sub - 8:, :]
        prev_row = last8[7:8, :]
        ps.append(p[sb] + mu_ref[...] * (shifted - p[sb]))
    carry_ref[...] = last8

    z = []
    for sb in subs:
        lora_in = ps[sb][:, 3 * rw:]
        z.append(_bf(jnp.where(lane < DECAY_LORA, jnp.tanh(lora_in),
                               jnp.where(lane < DECAY_LORA + ICLR_LORA, lora_in, _sigmoid(lora_in)))))
    lo = [jnp.dot(z[sb], wl_ref[...], preferred_element_type=F32) for sb in subs]

    pr = [ps[sb][:, :rw] for sb in subs]
    pk = [ps[sb][:, rw:2 * rw] for sb in subs]
    pv = [ps[sb][:, 2 * rw:3 * rw] for sb in subs]
    kkr = [pk[sb] * kk_ref[...] for sb in subs]
    kk_sq = [_head_sum(kkr[sb] * kkr[sb]) for sb in subs]
    a = [_sigmoid(a0_ref[...] + lo[sb][:, rw:2 * rw]) for sb in subs]
    k = [pk[sb] * (1.0 + (a[sb] - 1.0) * ka_ref[...]) for sb in subs]
    rk_sum = [_head_sum(pr[sb] * k[sb] * rk_ref[...]) for sb in subs]
    lw = [(-math.exp(-0.5) * LOG2E) * _sigmoid(w0_ref[...] + lo[sb][:, :rw]) for sb in subs]
    cum = [_mm_exact_lhs(lower, lw[sb]) for sb in subs]

    for sb in subs:
        kkn = kkr[sb] * (1.0 / jnp.maximum(jnp.sqrt(kk_sq[sb]), 1e-12))
        gate_o[0, rows[sb], :] = lo[sb][:, 2 * rw:]
        bonus_o[0, rows[sb], :] = rk_sum[sb] * pv[sb]
        cum_o[0, rows[sb], :] = cum[sb]
        r_o[0, rows[sb], :] = _bf(pr[sb])
        k_o[0, rows[sb], :] = _bf(k[sb])
        v_o[0, rows[sb], :] = _bf(pv[sb])
        nkk_o[0, rows[sb], :] = _bf(-kkn)
        b_o[0, rows[sb], :] = _bf(kkn * a[sb])


def _rwkv_prep(x, mod3, g_pre, w_rwkv, mu, w_lora, w0, a0, k_k, k_a, r_k, tm):
    bsz, seq, d = x.shape
    rw = RWKV_WIDTH
    row_spec = pl.BlockSpec((1, tm, rw), lambda b, i: (b, i, 0))
    vec = lambda n: pl.BlockSpec((1, n), lambda b, i: (0, 0))
    big = jax.ShapeDtypeStruct((bsz, seq, rw), F32)
    half = jax.ShapeDtypeStruct((bsz, seq, rw), BF16)
    return pl.pallas_call(
        _rwkv_prep_kernel,
        grid=(bsz, seq // tm),
        in_specs=[pl.BlockSpec((1, tm, d), lambda b, i: (b, i, 0)),
                  pl.BlockSpec((1, 1, d), lambda b, i: (b, 0, 0)),
                  pl.BlockSpec((1, 1, d), lambda b, i: (b, 0, 1)),
                  vec(d),
                  pl.BlockSpec((d, SHIFT_WIDTH), lambda b, i: (0, 0), pipeline_mode=pl.Buffered(1)),
                  vec(SHIFT_WIDTH),
                  pl.BlockSpec((LORA_WIDTH, 3 * rw), lambda b, i: (0, 0), pipeline_mode=pl.Buffered(1)),
                  vec(rw), vec(rw), vec(rw), vec(rw), vec(rw)],
        out_specs=[row_spec] * 8,
        out_shape=[half] * 5 + [big] * 3,
        scratch_shapes=[pltpu.VMEM((8, SHIFT_WIDTH), F32)],
        compiler_params=pltpu.CompilerParams(
            dimension_semantics=("arbitrary", "arbitrary"), vmem_limit_bytes=VMEM_LIMIT),
        name="rwkv_prep",
    )(x, mod3, mod3, g_pre, w_rwkv, mu, w_lora, w0, a0, k_k, k_a, r_k)


def _pair_blockdiag(x):
    first = lax.broadcasted_iota(jnp.int32, x.shape, 1) < HEAD_DIM
    zero = jnp.zeros_like(x)
    return jnp.concatenate([jnp.where(first, x, zero), jnp.where(first, zero, x)], axis=0)


def _pmm(a, b):
    return jnp.dot(_bf(a), _pair_blockdiag(_bf(b)), preferred_element_type=F32)


def _pmm_nt(a, b):
    return lax.dot_general(_bf(a), _pair_blockdiag(_bf(b)), (((1,), (1,)), ((), ())),
                           preferred_element_type=F32)


def _pmm_tn(a, b):
    full = lax.dot_general(_bf(a), _bf(b), (((0,), (0,)), ((), ())), preferred_element_type=F32)
    first = lax.broadcasted_iota(jnp.int32, (HEAD_DIM, LANES), 1) < HEAD_DIM
    return jnp.where(first, full[:HEAD_DIM], full[HEAD_DIM:])


def _unit_lower_inverse(a_list, masks):
    eye, diag_mask, merge_masks = masks
    ps = [jnp.where(diag_mask, a, 0.0) for a in a_list]
    ts = [eye + p for p in ps]
    ps = [_pmm(p, p) for p in ps]
    m = 2
    while 2 * m < INV_BASE:
        both = [_pmm(jnp.concatenate([p, t], axis=0), p) for p, t in zip(ps, ts)]
        ps = [x[:CHUNK] for x in both]
        ts = [t + x[CHUNK:] for t, x in zip(ts, both)]
        m *= 2
    ts = [t + _pmm(t, p) for t, p in zip(ts, ps)]
    for mk in merge_masks:
        ws = [_pmm(jnp.where(mk, a, 0.0), t) for a, t in zip(a_list, ts)]
        ts = [t + _pmm(t, w) for t, w in zip(ts, ws)]
    return ts


def _rwkv_scan_kernel(r_ref, k_ref, v_ref, nkk_ref, b_ref, cum_ref, y_ref, st_ref):
    @pl.when(pl.program_id(1) == 0)
    def _():
        st_ref[...] = jnp.zeros_like(st_ref)

    ri = lax.broadcasted_iota(jnp.int32, (CHUNK, LANES), 0)
    ci = _rem(lax.broadcasted_iota(jnp.int32, (CHUNK, LANES), 1), HEAD_DIM)
    strict = ri > ci
    incl = ri >= ci
    ri2 = lax.broadcasted_iota(jnp.int32, (2 * CHUNK, LANES), 0)
    ci2 = _rem(lax.broadcasted_iota(jnp.int32, (2 * CHUNK, LANES), 1), HEAD_DIM)
    rloc = _rem(ri2, CHUNK)
    stacked = (rloc > ci2) | ((rloc == ci2) & (ri2 >= CHUNK))
    eye = jnp.where(ri == ci, 1.0, 0.0)
    diag_mask = _div(ri, INV_BASE) == _div(ci, INV_BASE)
    merge_masks = []
    bsize = INV_BASE
    while bsize < CHUNK:
        merge_masks.append((_div(ri, 2 * bsize) == _div(ci, 2 * bsize))
                           & (_div(ri, bsize) != _div(ci, bsize)))
        bsize *= 2
    masks = (eye, diag_mask, merge_masks)

    n_rows = r_ref.shape[0]
    n_chunks = r_ref.shape[1] // CHUNK
    n_pairs = RWKV_WIDTH // LANES
    seqs = [(b, j) for b in range(n_rows) for j in range(n_pairs)]
    items = [(c, s) for c in range(n_chunks) for s in seqs]

    def tile(ref, c, s):
        b, j = s
        return ref[b, c * CHUNK:(c + 1) * CHUNK, j * LANES:(j + 1) * LANES]

    first_row = ri == 0
    qs, bk, kd, bkg, gcs = {}, {}, {}, {}, {}
    for it in items:
        cum = tile(cum_ref, *it)
        cum_last = cum[CHUNK - 1:CHUNK, :]
        e_cum = jnp.exp2(cum)
        e_prev = jnp.exp2(jnp.where(first_row, 0.0, pltpu.roll(cum, 1, axis=0)))
        e_inv = jnp.exp2(-cum)
        e_rest = jnp.exp2(cum_last - cum)
        gcs[it] = jnp.exp2(cum_last)
        r, k = tile(r_ref, *it).astype(F32), tile(k_ref, *it).astype(F32)
        nkk, b = tile(nkk_ref, *it).astype(F32), tile(b_ref, *it).astype(F32)
        qs[it] = _bf(jnp.concatenate([nkk * e_prev, r * e_cum], axis=0))
        bk[it], kd[it] = _bf(b * e_inv), _bf(k * e_inv)
        bkg[it] = _bf(jnp.concatenate([b * e_rest, k * e_rest], axis=0))

    nt = lambda a, b: lax.dot_general(a, b, (((1,), (1,)), ((), ())), preferred_element_type=F32)
    abk = {it: nt(qs[it], jnp.concatenate([_pair_blockdiag(bk[it]), _pair_blockdiag(kd[it])], axis=0))
           for it in items}
    ab = {it: abk[it][:, :LANES] for it in items}
    ak = {it: jnp.where(stacked, abk[it][:, LANES:], 0.0) for it in items}
    akv = {it: _pmm(ak[it], tile(v_ref, *it)) for it in items}
    t_list = _unit_lower_inverse([jnp.where(strict, ab[it][:CHUNK], 0.0) for it in items], masks)
    t = dict(zip(items, t_list))

    state = {s: st_ref[s[0], s[1]] for s in seqs}
    for c in range(n_chunks):
        xs = {s: _pmm_nt(qs[(c, s)], state[s]) for s in seqs}
        u = {s: _pmm(t[(c, s)], xs[s][:CHUNK] + akv[(c, s)][:CHUNK]) for s in seqs}
        for s in seqs:
            b, j = s
            y = xs[s][CHUNK:] + akv[(c, s)][CHUNK:] + _pmm(
                jnp.where(incl, ab[(c, s)][CHUNK:], 0.0), u[s])
            y_ref[b, c * CHUNK:(c + 1) * CHUNK, j * LANES:(j + 1) * LANES] = y
        new_state = {}
        for s in seqs:
            uv = jnp.concatenate([_bf(u[s]), tile(v_ref, c, s)], axis=0)
            new_state[s] = state[s] * gcs[(c, s)] + _pmm_tn(uv, bkg[(c, s)])
        state = new_state
    for s in seqs:
        st_ref[s[0], s[1]] = state[s]


def _rwkv_scan(r, k, v, nkk, b, cum, rows_per_step, chunks_per_step):
    bsz, seq, rw = r.shape
    ts = chunks_per_step * CHUNK
    row_spec = pl.BlockSpec((rows_per_step, ts, rw), lambda b, c: (b, c, 0))
    return pl.pallas_call(
        _rwkv_scan_kernel,
        grid=(bsz // rows_per_step, seq // ts),
        in_specs=[row_spec] * 6,
        out_specs=row_spec,
        out_shape=jax.ShapeDtypeStruct((bsz, seq, rw), F32),
        scratch_shapes=[pltpu.VMEM((rows_per_step, rw // LANES, HEAD_DIM, LANES), F32)],
        compiler_params=pltpu.CompilerParams(dimension_semantics=("arbitrary", "arbitrary")),
        name="rwkv_scan",
    )(r, k, v, nkk, b, cum)


def _attn_proj_kernel(x_ref, shift_ref, scale_ref, g_ref, w_ref, cos_ref, sin_ref, *rest):
    outs, buf = rest[:-1], rest[-1]
    SUB_ROWS = ATTN_PROJ_SUB_ROWS
    aw = ATTN_WIDTH
    reps = aw // LANES
    lane = lax.broadcasted_iota(jnp.int32, (SUB_ROWS, aw), 1)
    first_half = _rem(lane, HEAD_DIM) < (HEAD_DIM // 2)
    tiles_per_group = GROUP_WIDTH // LANES

    subs = range(x_ref.shape[1] // SUB_ROWS)
    ps = []
    for sb in subs:
        rows = slice(sb * SUB_ROWS, (sb + 1) * SUB_ROWS)
        h = _norm_modulate(x_ref[0, rows, :], g_ref[...], shift_ref[0], scale_ref[0])
        ps.append(jnp.dot(_bf(h), w_ref[...], preferred_element_type=F32))
    for sb in subs:
        rows = slice(sb * SUB_ROWS, (sb + 1) * SUB_ROWS)
        p = ps[sb]
        cos_t, sin_t = cos_ref[rows, :], sin_ref[rows, :]

        def rope(t, scale):
            cos = jnp.concatenate([cos_t * scale] * reps, axis=-1)
            sin = jnp.concatenate([sin_t * scale] * reps, axis=-1)
            partner = jnp.where(first_half, pltpu.roll(t, aw - HEAD_DIM // 2, axis=1),
                                pltpu.roll(t, HEAD_DIM // 2, axis=1))
            return t * cos + partner * sin

        roped = (rope(p[:, :aw], HEAD_DIM ** -0.5 * LOG2E), rope(p[:, aw:2 * aw], 1.0), p[:, 2 * aw:])
        n = 0
        for which in range(3):
            for g, (_, dil) in enumerate(ATTN_GROUPS):
                o_ref = outs[n]
                n += 1
                out_rows = slice(sb * SUB_ROWS // dil, (sb + 1) * SUB_ROWS // dil)
                for ct in range(tiles_per_group):
                    c0 = g * GROUP_WIDTH + ct * LANES
                    val = roped[which][:, c0:c0 + LANES]
                    if dil == 1:
                        o_ref[0, 0, out_rows, ct * LANES:(ct + 1) * LANES] = _bf(val)
                        continue
                    slot = (which * len(ATTN_GROUPS) + g) * tiles_per_group + ct
                    buf[sb, slot] = val
                    for r in range(dil):
                        o_ref[0, r, out_rows, ct * LANES:(ct + 1) * LANES] = _bf(
                            buf[sb, slot, pl.ds(r, SUB_ROWS // dil, stride=dil), :])


def _attn_proj(x, mod3, g_pre, w_qkv, cos_t, sin_t, tm):
    bsz, seq, d = x.shape
    out_specs, out_shape = [], []
    for _ in range(3):
        for _, dil in ATTN_GROUPS:
            out_specs.append(pl.BlockSpec((1, dil, tm // dil, GROUP_WIDTH), lambda b, i: (b, 0, i, 0)))
            out_shape.append(jax.ShapeDtypeStruct((bsz, dil, seq // dil, GROUP_WIDTH), BF16))
    return pl.pallas_call(
        _attn_proj_kernel,
        grid=(bsz, seq // tm),
        in_specs=[pl.BlockSpec((1, tm, d), lambda b, i: (b, i, 0)),
                  pl.BlockSpec((1, 1, d), lambda b, i: (b, 0, 0)),
                  pl.BlockSpec((1, 1, d), lambda b, i: (b, 0, 1)),
                  pl.BlockSpec((1, d), lambda b, i: (0, 0)),
                  pl.BlockSpec((d, 3 * ATTN_WIDTH), lambda b, i: (0, 0), pipeline_mode=pl.Buffered(1)),
                  pl.BlockSpec((tm, LANES), lambda b, i: (i, 0)),
                  pl.BlockSpec((tm, LANES), lambda b, i: (i, 0))],
        out_specs=out_specs,
        out_shape=out_shape,
        scratch_shapes=[pltpu.VMEM((tm // ATTN_PROJ_SUB_ROWS, 3 * ATTN_WIDTH // LANES,
                                    ATTN_PROJ_SUB_ROWS, LANES), F32)],
        compiler_params=pltpu.CompilerParams(
            dimension_semantics=("arbitrary", "arbitrary"), vmem_limit_bytes=VMEM_LIMIT),
        name="attn_proj",
    )(x, mod3, mod3, g_pre, w_qkv, cos_t, sin_t)


def _attn_kernel(*refs):
    qkv = refs[:9]
    o_ref = refs[9]
    num_ref, m_ref, den_ref = refs[10:]
    blk = ATTN_BLOCK
    qi = lax.broadcasted_iota(jnp.int32, (blk, blk), 0)
    kj = lax.broadcasted_iota(jnp.int32, (blk, blk), 1)
    bias_cur = jnp.where(kj <= qi, 0.0, NEG_BIG)
    bias_prev = jnp.where(kj >= qi, 0.0, NEG_BIG)
    lane = lax.broadcasted_iota(jnp.int32, (blk, LANES), 1)
    head0 = lane < HEAD_DIM

    order = sorted(range(len(ATTN_GROUPS)), key=lambda g: -ATTN_GROUPS[g][1])
    assert ATTN_GROUPS[order[-1]][1] == 1
    for g in order:
        dil = ATTN_GROUPS[g][1]
        q_ref, k_ref, v_ref = qkv[g], qkv[3 + g], qkv[6 + g]
        nb = q_ref.shape[2] // blk

        nper = min(nb, ATTN_BLOCKS_PER_ITER)
        rper = ATTN_BLOCKS_PER_ITER // nper
        whole = nper == nb

        def body(idx, carry, q_ref=q_ref, k_ref=k_ref, v_ref=v_ref, dil=dil, nb=nb, g=g,
                 nper=nper, rper=rper, whole=whole):
            if whole:
                r0, n0 = idx * rper, 0
                base = 0
            else:
                r0 = _div(idx, nb // nper)
                n0 = _rem(idx, nb // nper) * nper
                base = pl.multiple_of(n0 * blk, blk)
            nt = lambda a, b: lax.dot_general(a, b, (((1,), (1,)), ((), ())), preferred_element_type=F32)
            one = jnp.ones((blk, LANES), BF16)
            zero = jnp.zeros((blk, LANES), BF16)
            q_msk, ks, v_aug = {}, {}, {}
            for rr in range(rper):
                r = r0 + rr
                first = 0 if whole else -1
                for i in range(first, nper):
                    if i < 0:
                        st = pl.multiple_of(jnp.maximum(n0 - 1, 0) * blk, blk)
                    else:
                        st = base + i * blk
                    ks[(rr, i)] = k_ref[0, r, pl.ds(st, blk), :]
                    v = v_ref[0, r, pl.ds(st, blk), :]
                    v_aug[(rr, i)] = (jnp.where(head0, v, one), jnp.where(head0, one, v))
                    if i >= 0:
                        q = q_ref[0, r, pl.ds(st, blk), :]
                        q_msk[(rr, i)] = (jnp.where(head0, q, zero), jnp.where(head0, zero, q))
            items = [(rr, i, e) for rr in range(rper) for i in range(nper) for e in range(2)]
            has_prev = lambda it: (it[0], it[1] - 1) in ks
            prev_of = lambda it: (it[0], it[1] - 1)
            k_win = {it[:2]: jnp.concatenate([ks[prev_of(it)], ks[it[:2]]], axis=0)
                     for it in items if has_prev(it)}
            bias_win = jnp.concatenate([bias_prev, bias_cur], axis=1)
            bias_first = bias_win if whole else jnp.where(n0 > 0, bias_win, jnp.concatenate(
                [jnp.full_like(bias_prev, NEG_BIG), bias_cur], axis=1))
            s, ms, ps, pv = {}, {}, {}, {}

            def scores(it):
                s[it] = nt(q_msk[it[:2]][it[2]], k_win[it[:2]] if has_prev(it) else ks[it[:2]])

            def softmax(it):
                if has_prev(it):
                    sb = s.pop(it) + (bias_first if it[1] == 0 else bias_win)
                else:
                    sb = s.pop(it) + bias_cur
                m = jnp.max(sb, axis=-1, keepdims=True)
                ms[it] = m
                ps[it] = _bf(jnp.exp2(sb - m))

            def weighted_values(it):
                if has_prev(it):
                    v_win = jnp.concatenate([v_aug[prev_of(it)][it[2]], v_aug[it[:2]][it[2]]], axis=0)
                else:
                    v_win = v_aug[it[:2]][it[2]]
                pv[it] = jnp.dot(ps.pop(it), v_win, preferred_element_type=F32)

            for it in items:
                scores(it)
            for it in items:
                softmax(it)
            for it in items:
                weighted_values(it)
            for rr in range(rper):
                for i in range(nper):
                    m_b = jnp.where(head0, ms[(rr, i, 0)], ms[(rr, i, 1)])
                    num_b = jnp.where(head0, pv[(rr, i, 0)], pv[(rr, i, 1)])
                    l_b = pltpu.roll(jnp.where(head0, pv[(rr, i, 1)], pv[(rr, i, 0)]), HEAD_DIM, axis=1)
                    if dil == 1:
                        rows = pl.ds(base + i * blk, blk)
                    else:
                        rows = pl.ds((n0 + i) * (blk * dil) + r0 + rr, blk, stride=dil)
                    if g == order[0]:
                        num_ref[rows, :] = num_b
                        m_ref[rows, :] = m_b
                        den_ref[rows, :] = l_b
                        continue
                    m_old = m_ref[rows, :]
                    m_new = jnp.maximum(m_old, m_b)
                    w_old = jnp.exp2(m_old - m_new)
                    w_b = jnp.exp2(m_b - m_new)
                    num = num_ref[rows, :] * w_old + num_b * w_b
                    den = den_ref[rows, :] * w_old + l_b * w_b
                    if g == order[-1]:
                        o_ref[0, rows, :] = _bf(num / den)
                    else:
                        num_ref[rows, :] = num
                        den_ref[rows, :] = den
                        m_ref[rows, :] = m_new
            return carry

        lax.fori_loop(0, dil * nb // (nper * rper), body, 0, unroll=2)


def _attn(qkv, seq):
    bsz = qkv[0].shape[0]
    in_specs = []
    for _ in range(3):
        for _, dil in ATTN_GROUPS:
            in_specs.append(pl.BlockSpec((1, dil, seq // dil, LANES), lambda b, sp: (b, 0, 0, sp)))
    return pl.pallas_call(
        _attn_kernel,
        grid=(bsz, GROUP_WIDTH // LANES),
        in_specs=in_specs,
        out_specs=pl.BlockSpec((1, seq, LANES), lambda b, sp: (b, 0, sp)),
        out_shape=jax.ShapeDtypeStruct((bsz, seq, GROUP_WIDTH), BF16),
        scratch_shapes=[pltpu.VMEM((seq, LANES), F32)] * 3,
        compiler_params=pltpu.CompilerParams(
            dimension_semantics=("arbitrary", "arbitrary"), vmem_limit_bytes=VMEM_LIMIT),
        name="attn",
    )(*qkv)


def _merge_kernel(x_ref, shift_ref, scale_ref, gatem_ref, g_ref, wg_ref, y_ref, bonus_ref, gate_ref,
                  lnw_ref, lnb_ref, wa_ref, o_ref, wb_ref, wout_ref, gpost_ref, out_ref):
    inv_n = 1.0 / HEAD_DIM
    subs = range(x_ref.shape[1] // SUB_ROWS)
    rows = [slice(sb * SUB_ROWS, (sb + 1) * SUB_ROWS) for sb in subs]
    yb = [jnp.dot(o_ref[0, rows[sb], :], wb_ref[...], preferred_element_type=F32) for sb in subs]
    y = [y_ref[0, rows[sb], :] for sb in subs]
    mean = [_head_sum(y[sb]) * inv_n for sb in subs]
    h = [_bf(_norm_modulate(x_ref[0, rows[sb], :], g_ref[...], shift_ref[0], scale_ref[0])) for sb in subs]
    pg = [jnp.dot(h[sb], wg_ref[...], preferred_element_type=F32) for sb in subs]
    yc = [y[sb] - mean[sb] for sb in subs]
    var = [_head_sum(yc[sb] * yc[sb]) * inv_n for sb in subs]
    pre = [_bf((yc[sb] * lax.rsqrt(var[sb] + GN_EPS) * lnw_ref[...] + lnb_ref[...]
                + bonus_ref[0, rows[sb], :]) * gate_ref[0, rows[sb], :]) for sb in subs]
    ya = [jnp.dot(pre[sb], wa_ref[...], preferred_element_type=F32) for sb in subs]
    merged = [_bf(_sigmoid(pg[sb][:, :D_MODEL]) * ya[sb] + _sigmoid(pg[sb][:, D_MODEL:]) * yb[sb])
              for sb in subs]
    zz = [jnp.dot(merged[sb], wout_ref[...], preferred_element_type=F32) for sb in subs]
    for sb in subs:
        out_ref[0, rows[sb], :] = x_ref[0, rows[sb], :] + gatem_ref[0] * _rms(zz[sb], gpost_ref[...])


def _merge(x, mod3, g_pre, w_g, y, bonus, gate, lnx_w, lnx_b, w_a, o_att, w_b, w_out, g_post, tm):
    bsz, seq, d = x.shape
    rw = RWKV_WIDTH
    const = lambda shape: pl.BlockSpec(shape, lambda b, i: (0, 0), pipeline_mode=pl.Buffered(1))
    row = lambda n: pl.BlockSpec((1, tm, n), lambda b, i: (b, i, 0))
    modv = lambda j: pl.BlockSpec((1, 1, d), lambda b, i: (b, 0, j))
    return pl.pallas_call(
        _merge_kernel,
        grid=(bsz, seq // tm),
        in_specs=[row(d), modv(0), modv(1), modv(2), const((1, d)), const((d, 2 * d)),
                  row(rw), row(rw), row(rw), const((1, rw)), const((1, rw)), const((rw, d)),
                  row(GROUP_WIDTH), const((GROUP_WIDTH, d)), const((d, d)), const((1, d))],
        out_specs=row(d),
        out_shape=jax.ShapeDtypeStruct((bsz, seq, d), F32),
        compiler_params=pltpu.CompilerParams(
            dimension_semantics=("arbitrary", "arbitrary"), vmem_limit_bytes=VMEM_LIMIT),
        name="merge",
    )(x, mod3, mod3, mod3, g_pre, w_g, y, bonus, gate, lnx_w, lnx_b, w_a, o_att, w_b, w_out, g_post)


def _ffn_kernel(x_ref, shift_ref, scale_ref, gatef_ref, g_ref, win_ref, wout_ref, gpost_ref, out_ref):
    dff = wout_ref.shape[0]
    subs = range(x_ref.shape[1] // SUB_ROWS)
    rows = [slice(sb * SUB_ROWS, (sb + 1) * SUB_ROWS) for sb in subs]
    h = [_bf(_norm_modulate(x_ref[0, rows[sb], :], g_ref[...], shift_ref[0], scale_ref[0])) for sb in subs]
    u = [jnp.dot(h[sb], win_ref[:, :dff], preferred_element_type=F32) for sb in subs]
    gt = [jnp.dot(h[sb], win_ref[:, dff:], preferred_element_type=F32) for sb in subs]
    act = [_bf(u[sb] * _sigmoid(u[sb]) * gt[sb]) for sb in subs]
    f = [jnp.dot(act[sb], wout_ref[...], preferred_element_type=F32) for sb in subs]
    for sb in subs:
        out_ref[0, rows[sb], :] = x_ref[0, rows[sb], :] + gatef_ref[0] * _rms(f[sb], gpost_ref[...])


def _ffn(x, mod3, g_pre, w_in, w_out, g_post, tm):
    bsz, seq, d = x.shape
    dff = w_out.shape[0]
    const = lambda shape: pl.BlockSpec(shape, lambda b, i: (0, 0), pipeline_mode=pl.Buffered(1))
    row = pl.BlockSpec((1, tm, d), lambda b, i: (b, i, 0))
    modv = lambda j: pl.BlockSpec((1, 1, d), lambda b, i: (b, 0, j))
    return pl.pallas_call(
        _ffn_kernel,
        grid=(bsz, seq // tm),
        in_specs=[row, modv(3), modv(4), modv(5), const((1, d)), const((d, 2 * dff)),
                  const((dff, d)), const((1, d))],
        out_specs=row,
        out_shape=jax.ShapeDtypeStruct((bsz, seq, d), F32),
        compiler_params=pltpu.CompilerParams(
            dimension_semantics=("arbitrary", "arbitrary"), vmem_limit_bytes=VMEM_LIMIT),
        name="ffn",
    )(x, mod3, mod3, mod3, g_pre, w_in, w_out, g_post)


def _rope_tables(seq):
    half = HEAD_DIM // 2
    inv_freq = ROPE_THETA ** (-jnp.arange(half, dtype=F32) / half)
    ang = jnp.arange(seq, dtype=F32)[:, None] * inv_freq[None, :]
    cos, sin = jnp.cos(ang), jnp.sin(ang)
    reps = LANES // HEAD_DIM
    cos_t = jnp.tile(jnp.concatenate([cos, cos], axis=-1), (1, reps))
    sin_t = jnp.tile(jnp.concatenate([-sin, sin], axis=-1), (1, reps))
    return cos_t, sin_t


def _layer(x, c, w_ada, b_ada, g_pre_mix, g_post_mix, w_in, mu_shift, w0, w2, a0, a2, g2,
           k_k, k_a, r_k, lnx_w, lnx_b, w_a, w_b, w_out, g_pre_ffn, g_post_ffn, w_ffn_in, w_ffn_out):
    bsz, seq, d = x.shape
    rw = RWKV_WIDTH
    row1 = lambda t: t.reshape(1, -1)
    mod3 = _mod(c, w_ada, b_ada).reshape(bsz, 1, 6 * d)

    w_rwkv = _bf(w_in[:, :SHIFT_WIDTH])
    w_qkv = _bf(w_in[:, SHIFT_WIDTH:SHIFT_WIDTH + 3 * ATTN_WIDTH])
    w_gates = _bf(w_in[:, SHIFT_WIDTH + 3 * ATTN_WIDTH:])
    w_lora = jnp.zeros((LORA_WIDTH, 3 * rw), F32)
    w_lora = w_lora.at[:DECAY_LORA, :rw].set(w2)
    w_lora = w_lora.at[DECAY_LORA:DECAY_LORA + ICLR_LORA, rw:2 * rw].set(a2)
    w_lora = _bf(w_lora.at[DECAY_LORA + ICLR_LORA:, 2 * rw:].set(g2))

    prep = _rwkv_prep(x, mod3, row1(g_pre_mix), w_rwkv, row1(mu_shift), w_lora, row1(w0), row1(a0),
                      row1(k_k), row1(k_a), row1(r_k), tm=512)
    r, k, v, nkk, b, cum, bonus, gate = prep
    y = _rwkv_scan(r, k, v, nkk, b, cum, rows_per_step=8, chunks_per_step=1)

    cos_t, sin_t = _rope_tables(seq)
    qkv = _attn_proj(x, mod3, row1(g_pre_mix), w_qkv, cos_t, sin_t, tm=1024)
    o_att = _attn(qkv, seq)

    x1 = _merge(x, mod3, row1(g_pre_mix), w_gates, y, bonus, gate, row1(lnx_w), row1(lnx_b),
                _bf(w_a), o_att, _bf(w_b), _bf(w_out), row1(g_post_mix), tm=512)
    return _ffn(x1, mod3, row1(g_pre_ffn), _bf(w_ffn_in), _bf(w_ffn_out), row1(g_post_ffn), tm=512)


def kernel(x, c, w_ada, b_ada, g_pre_mix, g_post_mix, w_in, mu_shift, w0, w2, a0, a2, g2, k_k, k_a, r_k,
           lnx_w, lnx_b, w_a, w_b, w_out, g_pre_ffn, g_post_ffn, w_ffn_in, w_ffn_out):
    for i in range(w_in.shape[0]):
        x = _layer(x, c, w_ada[i], b_ada[i], g_pre_mix[i], g_post_mix[i], w_in[i], mu_shift[i], w0[i],
                   w2[i], a0[i], a2[i], g2[i], k_k[i], k_a[i], r_k[i], lnx_w[i], lnx_b[i], w_a[i], w_b[i],
                   w_out[i], g_pre_ffn[i], g_post_ffn[i], w_ffn_in[i], w_ffn_out[i])
    return x
```

```python
import functools
import math

import jax
import jax.numpy as jnp
from jax import lax
from jax.experimental import pallas as pl
from jax.experimental.pallas import tpu as pltpu

D_MODEL = 1024
HEAD_DIM = 64
RWKV_HEADS = 12
RWKV_WIDTH = RWKV_HEADS * HEAD_DIM
DECAY_LORA = 64
ICLR_LORA = 64
GATE_LORA = 128
LORA_WIDTH = DECAY_LORA + ICLR_LORA + GATE_LORA
ATTN_GROUPS = ((128, 1), (512, 4), (2048, 16))
HEADS_PER_GROUP = 4
GROUP_WIDTH = HEADS_PER_GROUP * HEAD_DIM
ATTN_WIDTH = len(ATTN_GROUPS) * GROUP_WIDTH
ATTN_BLOCK = 128
ROPE_THETA = 10000.0
SHIFT_WIDTH = 3 * RWKV_WIDTH + LORA_WIDTH
NORM_EPS = 1e-6
GN_EPS = 64e-5

CHUNK = 64
INV_BASE = 16
LANES = 128
NEG_BIG = -1e30
LOG2E = math.log2(math.e)
ATTN_BLOCKS_PER_ITER = 8
SUB_ROWS = 256
ATTN_PROJ_SUB_ROWS = 512
VMEM_LIMIT = 56 * 1024 * 1024

F32 = jnp.float32
BF16 = jnp.bfloat16


def _bf(x):
    return x.astype(BF16)


def _div(x, n):
    assert n & (n - 1) == 0
    return jnp.right_shift(x, n.bit_length() - 1)


def _rem(x, n):
    assert n & (n - 1) == 0
    return jnp.bitwise_and(x, n - 1)


def _sigmoid(x):
    return 1.0 / (1.0 + jnp.exp2(x * (-LOG2E)))


def _split_hi_lo(x):
    hi = _bf(x)
    lo = _bf(x - hi.astype(F32))
    return hi, lo


def _mm_exact_lhs(m_bf, x):
    hi, lo = _split_hi_lo(x)
    return (jnp.dot(m_bf, hi, preferred_element_type=F32)
            + jnp.dot(m_bf, lo, preferred_element_type=F32))


def _head_sum(x):
    tile = 2 * LANES
    ri = _div(lax.broadcasted_iota(jnp.int32, (tile, tile), 0), HEAD_DIM)
    ci = _div(lax.broadcasted_iota(jnp.int32, (tile, tile), 1), HEAD_DIM)
    ones_bd = jnp.where(ri == ci, 1.0, 0.0).astype(BF16)
    xb = _bf(x)
    parts = [jnp.dot(xb[:, j * tile:(j + 1) * tile], ones_bd, preferred_element_type=F32)
             for j in range(x.shape[-1] // tile)]
    return jnp.concatenate(parts, axis=-1)


def _norm_modulate(x, g, shift, scale):
    y = x * lax.rsqrt(jnp.mean(x * x, axis=-1, keepdims=True) + NORM_EPS)
    return y * (g * (1.0 + scale)) + shift


def _rms(x, g):
    return x * lax.rsqrt(jnp.mean(x * x, axis=-1, keepdims=True) + NORM_EPS) * g


def _mod_kernel(c_ref, w_ref, b_ref, o_ref):
    c = c_ref[...]
    hi, lo = _split_hi_lo(c * _sigmoid(c))
    w = _bf(w_ref[...])
    o_ref[...] = (jnp.dot(hi, w, preferred_element_type=F32)
                  + jnp.dot(lo, w, preferred_element_type=F32) + b_ref[...])


def _mod(c, w_ada, b_ada):
    bsz, d = c.shape
    n = w_ada.shape[1]
    tn = d
    return pl.pallas_call(
        _mod_kernel,
        grid=(n // tn,),
        in_specs=[pl.BlockSpec((bsz, d), lambda j: (0, 0)),
                  pl.BlockSpec((d, tn), lambda j: (0, j)),
                  pl.BlockSpec((1, tn), lambda j: (0, j))],
        out_specs=pl.BlockSpec((bsz, tn), lambda j: (0, j)),
        out_shape=jax.ShapeDtypeStruct((bsz, n), F32),
        name="mod",
    )(c, w_ada, b_ada.reshape(1, n))


def _rwkv_prep_kernel(x_ref, shift_ref, scale_ref, g_ref, w_ref, mu_ref, wl_ref, w0_ref, a0_ref,
                      kk_ref, ka_ref, rk_ref,
                      r_o, k_o, v_o, nkk_o, b_o, cum_o, bonus_o, gate_o,
                      carry_ref):
    i = pl.program_id(1)
    rw = RWKV_WIDTH
    sub = SUB_ROWS

    @pl.when(i == 0)
    def _():
        carry_ref[...] = jnp.zeros_like(carry_ref)

    ri = lax.broadcasted_iota(jnp.int32, (sub, sub), 0)
    ci = lax.broadcasted_iota(jnp.int32, (sub, sub), 1)
    same = _div(ri, CHUNK) == _div(ci, CHUNK)
    lower = jnp.where(same & (ci <= ri), 1.0, 0.0).astype(BF16)
    row0 = lax.broadcasted_iota(jnp.int32, (8, SHIFT_WIDTH), 0) == 0
    assert DECAY_LORA + ICLR_LORA == LANES
    lane = lax.broadcasted_iota(jnp.int32, (sub, LANES), 1)

    subs = range(x_ref.shape[1] // sub)
    rows = [slice(sb * sub, (sb + 1) * sub) for sb in subs]
    h = [_bf(_norm_modulate(x_ref[0, rows[sb], :], g_ref[...], shift_ref[0], scale_ref[0])) for sb in subs]
    p = [jnp.dot(h[sb], w_ref[...], preferred_element_type=F32) for sb in subs]

    prev_row = carry_ref[7:8, :]
    ps = []
    for sb in subs:
        rolled = pltpu.roll(p[sb], 1, axis=0)
        shifted = jnp.concatenate([jnp.where(row0, prev_row, rolled[:8]), rolled[8:]], axis=0)
        last8 = p[sb][sub - 8:, :]
        prev_row = last8[7:8, :]
        ps.append(p[sb] + mu_ref[...] * (shifted - p[sb]))
    carry_ref[...] = last8

    z = []
    for sb in subs:
        wa_in = ps[sb][:, 3 * rw:3 * rw + LANES]
        g_in = ps[sb][:, 3 * rw + LANES:]
        z.append(_bf(jnp.concatenate([jnp.where(lane < DECAY_LORA, jnp.tanh(wa_in), wa_in),
                                      _sigmoid(g_in)], axis=1)))
    lo = [jnp.dot(z[sb], wl_ref[...], preferred_element_type=F32) for sb in subs]

    pr = [ps[sb][:, :rw] for sb in subs]
    pk = [ps[sb][:, rw:2 * rw] for sb in subs]
    pv = [ps[sb][:, 2 * rw:3 * rw] for sb in subs]
    kkr = [pk[sb] * kk_ref[...] for sb in subs]
    kk_sq = [_head_sum(kkr[sb] * kkr[sb]) for sb in subs]
    a = [_sigmoid(a0_ref[...] + lo[sb][:, rw:2 * rw]) for sb in subs]
    one_minus_ka = 1.0 - ka_ref[...]
    k = [pk[sb] * (one_minus_ka + a[sb] * ka_ref[...]) for sb in subs]
    rk_sum = [_head_sum(pr[sb] * k[sb] * rk_ref[...]) for sb in subs]
    lw = [(-math.exp(-0.5) * LOG2E) * _sigmoid(w0_ref[...] + lo[sb][:, :rw]) for sb in subs]
    cum = [_mm_exact_lhs(lower, lw[sb]) for sb in subs]

    for sb in subs:
        kkn = kkr[sb] * lax.rsqrt(jnp.maximum(kk_sq[sb], 1e-24))
        gate_o[0, rows[sb], :] = lo[sb][:, 2 * rw:]
        bonus_o[0, rows[sb], :] = rk_sum[sb] * pv[sb]
        cum_o[0, rows[sb], :] = cum[sb]
        r_o[0, rows[sb], :] = _bf(pr[sb])
        k_o[0, rows[sb], :] = _bf(k[sb])
        v_o[0, rows[sb], :] = _bf(pv[sb])
        nkk_o[0, rows[sb], :] = _bf(-kkn)
        b_o[0, rows[sb], :] = _bf(kkn * a[sb])


def _rwkv_prep(x, mod3, g_pre, w_rwkv, mu, w_lora, w0, a0, k_k, k_a, r_k, tm):
    bsz, seq, d = x.shape
    rw = RWKV_WIDTH
    row_spec = pl.BlockSpec((1, tm, rw), lambda b, i: (b, i, 0))
    vec = lambda n: pl.BlockSpec((1, n), lambda b, i: (0, 0))
    big = jax.ShapeDtypeStruct((bsz, seq, rw), F32)
    half = jax.ShapeDtypeStruct((bsz, seq, rw), BF16)
    return pl.pallas_call(
        _rwkv_prep_kernel,
        grid=(bsz, seq // tm),
        in_specs=[pl.BlockSpec((1, tm, d), lambda b, i: (b, i, 0)),
                  pl.BlockSpec((1, 1, d), lambda b, i: (b, 0, 0)),
                  pl.BlockSpec((1, 1, d), lambda b, i: (b, 0, 1)),
                  vec(d),
                  pl.BlockSpec((d, SHIFT_WIDTH), lambda b, i: (0, 0), pipeline_mode=pl.Buffered(1)),
                  vec(SHIFT_WIDTH),
                  pl.BlockSpec((LORA_WIDTH, 3 * rw), lambda b, i: (0, 0), pipeline_mode=pl.Buffered(1)),
                  vec(rw), vec(rw), vec(rw), vec(rw), vec(rw)],
        out_specs=[row_spec] * 8,
        out_shape=[half] * 5 + [big] * 3,
        scratch_shapes=[pltpu.VMEM((8, SHIFT_WIDTH), F32)],
        compiler_params=pltpu.CompilerParams(
            dimension_semantics=("arbitrary", "arbitrary"), vmem_limit_bytes=VMEM_LIMIT),
        name="rwkv_prep",
    )(x, mod3, mod3, g_pre, w_rwkv, mu, w_lora, w0, a0, k_k, k_a, r_k)


def _pair_blockdiag(x):
    first = lax.broadcasted_iota(jnp.int32, x.shape, 1) < HEAD_DIM
    zero = jnp.zeros_like(x)
    return jnp.concatenate([jnp.where(first, x, zero), jnp.where(first, zero, x)], axis=0)


def _pmm(a, b):
    return jnp.dot(_bf(a), _pair_blockdiag(_bf(b)), preferred_element_type=F32)


def _pmm_nt(a, b):
    return lax.dot_general(_bf(a), _pair_blockdiag(_bf(b)), (((1,), (1,)), ((), ())),
                           preferred_element_type=F32)


def _pmm_tn(a, b):
    full = lax.dot_general(_bf(a), _bf(b), (((0,), (0,)), ((), ())), preferred_element_type=F32)
    first = lax.broadcasted_iota(jnp.int32, (HEAD_DIM, LANES), 1) < HEAD_DIM
    return jnp.where(first, full[:HEAD_DIM], full[HEAD_DIM:])


def _unit_lower_inverse(a_list, masks):
    eye, diag_mask, merge_masks = masks
    ps = [jnp.where(diag_mask, a, 0.0) for a in a_list]
    ts = [eye + p for p in ps]
    ps = [_pmm(p, p) for p in ps]
    m = 2
    while 2 * m < INV_BASE:
        both = [_pmm(jnp.concatenate([p, t], axis=0), p) for p, t in zip(ps, ts)]
        ps = [x[:CHUNK] for x in both]
        ts = [t + x[CHUNK:] for t, x in zip(ts, both)]
        m *= 2
    ts = [t + _pmm(t, p) for t, p in zip(ts, ps)]
    for mk in merge_masks:
        ws = [_pmm(jnp.where(mk, a, 0.0), t) for a, t in zip(a_list, ts)]
        ts = [t + _pmm(t, w) for t, w in zip(ts, ws)]
    return ts


def _rwkv_scan_kernel(r_ref, k_ref, v_ref, nkk_ref, b_ref, cum_ref, y_ref, st_ref):
    @pl.when(pl.program_id(1) == 0)
    def _():
        st_ref[...] = jnp.zeros_like(st_ref)

    ri = lax.broadcasted_iota(jnp.int32, (CHUNK, LANES), 0)
    ci = _rem(lax.broadcasted_iota(jnp.int32, (CHUNK, LANES), 1), HEAD_DIM)
    strict = ri > ci
    incl = ri >= ci
    ri2 = lax.broadcasted_iota(jnp.int32, (2 * CHUNK, LANES), 0)
    ci2 = _rem(lax.broadcasted_iota(jnp.int32, (2 * CHUNK, LANES), 1), HEAD_DIM)
    rloc = _rem(ri2, CHUNK)
    stacked = (rloc > ci2) | ((rloc == ci2) & (ri2 >= CHUNK))
    eye = jnp.where(ri == ci, 1.0, 0.0)
    diag_mask = _div(ri, INV_BASE) == _div(ci, INV_BASE)
    merge_masks = []
    bsize = INV_BASE
    while bsize < CHUNK:
        merge_masks.append((_div(ri, 2 * bsize) == _div(ci, 2 * bsize))
                           & (_div(ri, bsize) != _div(ci, bsize)))
        bsize *= 2
    masks = (eye, diag_mask, merge_masks)

    n_rows = r_ref.shape[0]
    n_chunks = r_ref.shape[1] // CHUNK
    n_pairs = RWKV_WIDTH // LANES
    seqs = [(b, j) for b in range(n_rows) for j in range(n_pairs)]
    items = [(c, s) for c in range(n_chunks) for s in seqs]

    def tile(ref, c, s):
        b, j = s
        return ref[b, c * CHUNK:(c + 1) * CHUNK, j * LANES:(j + 1) * LANES]

    first_row = ri == 0
    qs, bk, kd, bkg, gcs = {}, {}, {}, {}, {}
    for it in items:
        cum = tile(cum_ref, *it)
        cum_last = cum[CHUNK - 1:CHUNK, :]
        e_cum = jnp.exp2(cum)
        e_prev = jnp.exp2(jnp.where(first_row, 0.0, pltpu.roll(cum, 1, axis=0)))
        e_inv = jnp.exp2(-cum)
        e_rest = jnp.exp2(cum_last - cum)
        gcs[it] = jnp.exp2(cum_last)
        r, k = tile(r_ref, *it).astype(F32), tile(k_ref, *it).astype(F32)
        nkk, b = tile(nkk_ref, *it).astype(F32), tile(b_ref, *it).astype(F32)
        qs[it] = _bf(jnp.concatenate([nkk * e_prev, r * e_cum], axis=0))
        bk[it], kd[it] = _bf(b * e_inv), _bf(k * e_inv)
        bkg[it] = _bf(jnp.concatenate([b * e_rest, k * e_rest], axis=0))

    nt = lambda a, b: lax.dot_general(a, b, (((1,), (1,)), ((), ())), preferred_element_type=F32)
    abk = {it: nt(qs[it], jnp.concatenate([_pair_blockdiag(bk[it]), _pair_blockdiag(kd[it])], axis=0))
           for it in items}
    ab = {it: abk[it][:, :LANES] for it in items}
    ak = {it: jnp.where(stacked, abk[it][:, LANES:], 0.0) for it in items}
    akv = {it: _pmm(ak[it], tile(v_ref, *it)) for it in items}
    t_list = _unit_lower_inverse([jnp.where(strict, ab[it][:CHUNK], 0.0) for it in items], masks)
    t = dict(zip(items, t_list))

    state = {s: st_ref[s[0], s[1]] for s in seqs}
    for c in range(n_chunks):
        xs = {s: _pmm_nt(qs[(c, s)], state[s]) for s in seqs}
        u = {s: _pmm(t[(c, s)], xs[s][:CHUNK] + akv[(c, s)][:CHUNK]) for s in seqs}
        for s in seqs:
            b, j = s
            y = xs[s][CHUNK:] + akv[(c, s)][CHUNK:] + _pmm(
                jnp.where(incl, ab[(c, s)][CHUNK:], 0.0), u[s])
            y_ref[b, c * CHUNK:(c + 1) * CHUNK, j * LANES:(j + 1) * LANES] = y
        new_state = {}
        for s in seqs:
            uv = jnp.concatenate([_bf(u[s]), tile(v_ref, c, s)], axis=0)
            new_state[s] = state[s] * gcs[(c, s)] + _pmm_tn(uv, bkg[(c, s)])
        state = new_state
    for s in seqs:
        st_ref[s[0], s[1]] = state[s]


def _rwkv_scan(r, k, v, nkk, b, cum, rows_per_step, chunks_per_step):
    bsz, seq, rw = r.shape
    ts = chunks_per_step * CHUNK
    row_spec = pl.BlockSpec((rows_per_step, ts, rw), lambda b, c: (b, c, 0))
    return pl.pallas_call(
        _rwkv_scan_kernel,
        grid=(bsz // rows_per_step, seq // ts),
        in_specs=[row_spec] * 6,
        out_specs=row_spec,
        out_shape=jax.ShapeDtypeStruct((bsz, seq, rw), F32),
        scratch_shapes=[pltpu.VMEM((rows_per_step, rw // LANES, HEAD_DIM, LANES), F32)],
        compiler_params=pltpu.CompilerParams(dimension_semantics=("arbitrary", "arbitrary")),
        name="rwkv_scan",
    )(r, k, v, nkk, b, cum)


def _attn_proj_kernel(x_ref, shift_ref, scale_ref, g_ref, w_ref, cos_ref, sin_ref, *rest):
    outs, buf = rest[:-1], rest[-1]
    SUB_ROWS = ATTN_PROJ_SUB_ROWS
    aw = ATTN_WIDTH
    reps = aw // LANES
    lane = lax.broadcasted_iota(jnp.int32, (SUB_ROWS, aw), 1)
    first_half = _rem(lane, HEAD_DIM) < (HEAD_DIM // 2)
    tiles_per_group = GROUP_WIDTH // LANES

    subs = range(x_ref.shape[1] // SUB_ROWS)
    ps = []
    for sb in subs:
        rows = slice(sb * SUB_ROWS, (sb + 1) * SUB_ROWS)
        h = _norm_modulate(x_ref[0, rows, :], g_ref[...], shift_ref[0], scale_ref[0])
        ps.append(jnp.dot(_bf(h), w_ref[...], preferred_element_type=F32))
    for sb in subs:
        rows = slice(sb * SUB_ROWS, (sb + 1) * SUB_ROWS)
        p = ps[sb]
        cos_t, sin_t = cos_ref[rows, :], sin_ref[rows, :]

        def rope(t, scale):
            cos = jnp.concatenate([cos_t * scale] * reps, axis=-1)
            sin = jnp.concatenate([sin_t * scale] * reps, axis=-1)
            partner = jnp.where(first_half, pltpu.roll(t, aw - HEAD_DIM // 2, axis=1),
                                pltpu.roll(t, HEAD_DIM // 2, axis=1))
            return t * cos + partner * sin

        roped = (rope(p[:, :aw], HEAD_DIM ** -0.5 * LOG2E), rope(p[:, aw:2 * aw], 1.0), p[:, 2 * aw:])
        n = 0
        for which in range(3):
            for g, (_, dil) in enumerate(ATTN_GROUPS):
                o_ref = outs[n]
                n += 1
                out_rows = slice(sb * SUB_ROWS // dil, (sb + 1) * SUB_ROWS // dil)
                for ct in range(tiles_per_group):
                    c0 = g * GROUP_WIDTH + ct * LANES
                    val = roped[which][:, c0:c0 + LANES]
                    if dil == 1:
                        o_ref[0, 0, out_rows, ct * LANES:(ct + 1) * LANES] = _bf(val)
                        continue
                    slot = (which * len(ATTN_GROUPS) + g) * tiles_per_group + ct
                    buf[sb, slot] = val
                    for r in range(dil):
                        o_ref[0, r, out_rows, ct * LANES:(ct + 1) * LANES] = _bf(
                            buf[sb, slot, pl.ds(r, SUB_ROWS // dil, stride=dil), :])


def _attn_proj(x, mod3, g_pre, w_qkv, cos_t, sin_t, tm):
    bsz, seq, d = x.shape
    out_specs, out_shape = [], []
    for _ in range(3):
        for _, dil in ATTN_GROUPS:
            out_specs.append(pl.BlockSpec((1, dil, tm // dil, GROUP_WIDTH), lambda b, i: (b, 0, i, 0)))
            out_shape.append(jax.ShapeDtypeStruct((bsz, dil, seq // dil, GROUP_WIDTH), BF16))
    return pl.pallas_call(
        _attn_proj_kernel,
        grid=(bsz, seq // tm),
        in_specs=[pl.BlockSpec((1, tm, d), lambda b, i: (b, i, 0)),
                  pl.BlockSpec((1, 1, d), lambda b, i: (b, 0, 0)),
                  pl.BlockSpec((1, 1, d), lambda b, i: (b, 0, 1)),
                  pl.BlockSpec((1, d), lambda b, i: (0, 0)),
                  pl.BlockSpec((d, 3 * ATTN_WIDTH), lambda b, i: (0, 0), pipeline_mode=pl.Buffered(1)),
                  pl.BlockSpec((tm, LANES), lambda b, i: (i, 0)),
                  pl.BlockSpec((tm, LANES), lambda b, i: (i, 0))],
        out_specs=out_specs,
        out_shape=out_shape,
        scratch_shapes=[pltpu.VMEM((tm // ATTN_PROJ_SUB_ROWS, 3 * ATTN_WIDTH // LANES,
                                    ATTN_PROJ_SUB_ROWS, LANES), F32)],
        compiler_params=pltpu.CompilerParams(
            dimension_semantics=("arbitrary", "arbitrary"), vmem_limit_bytes=VMEM_LIMIT),
        name="attn_proj",
    )(x, mod3, mod3, g_pre, w_qkv, cos_t, sin_t)


def _attn_kernel(*refs):
    qkv = refs[:9]
    o_ref = refs[9]
    num_ref, m_ref, den_ref = refs[10:]
    blk = ATTN_BLOCK
    qi = lax.broadcasted_iota(jnp.int32, (blk, blk), 0)
    kj = lax.broadcasted_iota(jnp.int32, (blk, blk), 1)
    bias_cur = jnp.where(kj <= qi, 0.0, NEG_BIG)
    bias_prev = jnp.where(kj >= qi, 0.0, NEG_BIG)
    lane = lax.broadcasted_iota(jnp.int32, (blk, LANES), 1)
    head0 = lane < HEAD_DIM

    order = sorted(range(len(ATTN_GROUPS)), key=lambda g: -ATTN_GROUPS[g][1])
    assert ATTN_GROUPS[order[-1]][1] == 1
    for g in order:
        dil = ATTN_GROUPS[g][1]
        q_ref, k_ref, v_ref = qkv[g], qkv[3 + g], qkv[6 + g]
        nb = q_ref.shape[2] // blk

        nper = min(nb, ATTN_BLOCKS_PER_ITER)
        rper = ATTN_BLOCKS_PER_ITER // nper
        whole = nper == nb

        def body(idx, carry, q_ref=q_ref, k_ref=k_ref, v_ref=v_ref, dil=dil, nb=nb, g=g,
                 nper=nper, rper=rper, whole=whole):
            if whole:
                r0, n0 = idx * rper, 0
                base = 0
            else:
                r0 = _div(idx, nb // nper)
                n0 = _rem(idx, nb // nper) * nper
                base = pl.multiple_of(n0 * blk, blk)
            nt = lambda a, b: lax.dot_general(a, b, (((1,), (1,)), ((), ())), preferred_element_type=F32)
            one = jnp.ones((blk, LANES), BF16)
            zero = jnp.zeros((blk, LANES), BF16)
            q_msk, ks, v_aug = {}, {}, {}
            for rr in range(rper):
                r = r0 + rr
                first = 0 if whole else -1
                for i in range(first, nper):
                    if i < 0:
                        st = pl.multiple_of(jnp.maximum(n0 - 1, 0) * blk, blk)
                    else:
                        st = base + i * blk
                    ks[(rr, i)] = k_ref[0, r, pl.ds(st, blk), :]
                    v = v_ref[0, r, pl.ds(st, blk), :]
                    v_aug[(rr, i)] = (jnp.where(head0, v, one), jnp.where(head0, one, v))
                    if i >= 0:
                        q = q_ref[0, r, pl.ds(st, blk), :]
                        q_msk[(rr, i)] = (jnp.where(head0, q, zero), jnp.where(head0, zero, q))
            items = [(rr, i, e) for rr in range(rper) for i in range(nper) for e in range(2)]
            has_prev = lambda it: (it[0], it[1] - 1) in ks
            prev_of = lambda it: (it[0], it[1] - 1)
            k_win = {it[:2]: jnp.concatenate([ks[prev_of(it)], ks[it[:2]]], axis=0)
                     for it in items if has_prev(it)}
            bias_win = jnp.concatenate([bias_prev, bias_cur], axis=1)
            bias_first = bias_win if whole else jnp.where(n0 > 0, bias_win, jnp.concatenate(
                [jnp.full_like(bias_prev, NEG_BIG), bias_cur], axis=1))
            s, ms, ps, pv = {}, {}, {}, {}

            def scores(it):
                s[it] = nt(q_msk[it[:2]][it[2]], k_win[it[:2]] if has_prev(it) else ks[it[:2]])

            def softmax(it):
                if has_prev(it):
                    sb = s.pop(it) + (bias_first if it[1] == 0 else bias_win)
                else:
                    sb = s.pop(it) + bias_cur
                m = jnp.max(sb, axis=-1, keepdims=True)
                ms[it] = m
                ps[it] = _bf(jnp.exp2(sb - m))

            def weighted_values(it):
                if has_prev(it):
                    v_win = jnp.concatenate([v_aug[prev_of(it)][it[2]], v_aug[it[:2]][it[2]]], axis=0)
                else:
                    v_win = v_aug[it[:2]][it[2]]
                pv[it] = jnp.dot(ps.pop(it), v_win, preferred_element_type=F32)

            for it in items:
                scores(it)
            for it in items:
                softmax(it)
            for it in items:
                weighted_values(it)
            for rr in range(rper):
                for i in range(nper):
                    m_b = jnp.where(head0, ms[(rr, i, 0)], ms[(rr, i, 1)])
                    num_b = jnp.where(head0, pv[(rr, i, 0)], pv[(rr, i, 1)])
                    l_b = pltpu.roll(jnp.where(head0, pv[(rr, i, 1)], pv[(rr, i, 0)]), HEAD_DIM, axis=1)
                    if dil == 1:
                        rows = pl.ds(base + i * blk, blk)
                    else:
                        rows = pl.ds((n0 + i) * (blk * dil) + r0 + rr, blk, stride=dil)
                    if g == order[0]:
                        num_ref[rows, :] = num_b
                        m_ref[rows, :] = m_b
                        den_ref[rows, :] = l_b
                        continue
                    m_old = m_ref[rows, :]
                    m_new = jnp.maximum(m_old, m_b)
                    w_old = jnp.exp2(m_old - m_new)
                    w_b = jnp.exp2(m_b - m_new)
                    num = num_ref[rows, :] * w_old + num_b * w_b
                    den = den_ref[rows, :] * w_old + l_b * w_b
                    if g == order[-1]:
                        o_ref[0, rows, :] = _bf(num / den)
                    else:
                        num_ref[rows, :] = num
                        den_ref[rows, :] = den
                        m_ref[rows, :] = m_new
            return carry

        lax.fori_loop(0, dil * nb // (nper * rper), body, 0, unroll=2)


def _attn(qkv, seq):
    bsz = qkv[0].shape[0]
    in_specs = []
    for _ in range(3):
        for _, dil in ATTN_GROUPS:
            in_specs.append(pl.BlockSpec((1, dil, seq // dil, LANES), lambda b, sp: (b, 0, 0, sp)))
    return pl.pallas_call(
        _attn_kernel,
        grid=(bsz, GROUP_WIDTH // LANES),
        in_specs=in_specs,
        out_specs=pl.BlockSpec((1, seq, LANES), lambda b, sp: (b, 0, sp)),
        out_shape=jax.ShapeDtypeStruct((bsz, seq, GROUP_WIDTH), BF16),
        scratch_shapes=[pltpu.VMEM((seq, LANES), F32)] * 3,
        compiler_params=pltpu.CompilerParams(
            dimension_semantics=("arbitrary", "arbitrary"), vmem_limit_bytes=VMEM_LIMIT),
        name="attn",
    )(*qkv)


def _merge_kernel(x_ref, shift_ref, scale_ref, gatem_ref, g_ref, wg_ref, y_ref, bonus_ref, gate_ref,
                  lnw_ref, lnb_ref, wa_ref, o_ref, wb_ref, wout_ref, gpost_ref, out_ref):
    inv_n = 1.0 / HEAD_DIM
    subs = range(x_ref.shape[1] // SUB_ROWS)
    rows = [slice(sb * SUB_ROWS, (sb + 1) * SUB_ROWS) for sb in subs]
    yb = [jnp.dot(o_ref[0, rows[sb], :], wb_ref[...], preferred_element_type=F32) for sb in subs]
    y = [y_ref[0, rows[sb], :] for sb in subs]
    mean = [_head_sum(y[sb]) * inv_n for sb in subs]
    h = [_bf(_norm_modulate(x_ref[0, rows[sb], :], g_ref[...], shift_ref[0], scale_ref[0])) for sb in subs]
    pg = [jnp.dot(h[sb], wg_ref[...], preferred_element_type=F32) for sb in subs]
    yc = [y[sb] - mean[sb] for sb in subs]
    var = [_head_sum(yc[sb] * yc[sb]) * inv_n for sb in subs]
    pre = [_bf((yc[sb] * lax.rsqrt(var[sb] + GN_EPS) * lnw_ref[...] + lnb_ref[...]
                + bonus_ref[0, rows[sb], :]) * gate_ref[0, rows[sb], :]) for sb in subs]
    ya = [jnp.dot(pre[sb], wa_ref[...], preferred_element_type=F32) for sb in subs]
    merged = [_bf(_sigmoid(pg[sb][:, :D_MODEL]) * ya[sb] + _sigmoid(pg[sb][:, D_MODEL:]) * yb[sb])
              for sb in subs]
    zz = [jnp.dot(merged[sb], wout_ref[...], preferred_element_type=F32) for sb in subs]
    for sb in subs:
        out_ref[0, rows[sb], :] = x_ref[0, rows[sb], :] + gatem_ref[0] * _rms(zz[sb], gpost_ref[...])


def _merge(x, mod3, g_pre, w_g, y, bonus, gate, lnx_w, lnx_b, w_a, o_att, w_b, w_out, g_post, tm):
    bsz, seq, d = x.shape
    rw = RWKV_WIDTH
    const = lambda shape: pl.BlockSpec(shape, lambda b, i: (0, 0), pipeline_mode=pl.Buffered(1))
    row = lambda n: pl.BlockSpec((1, tm, n), lambda b, i: (b, i, 0))
    modv = lambda j: pl.BlockSpec((1, 1, d), lambda b, i: (b, 0, j))
    return pl.pallas_call(
        _merge_kernel,
        grid=(bsz, seq // tm),
        in_specs=[row(d), modv(0), modv(1), modv(2), const((1, d)), const((d, 2 * d)),
                  row(rw), row(rw), row(rw), const((1, rw)), const((1, rw)), const((rw, d)),
                  row(GROUP_WIDTH), const((GROUP_WIDTH, d)), const((d, d)), const((1, d))],
        out_specs=row(d),
        out_shape=jax.ShapeDtypeStruct((bsz, seq, d), F32),
        compiler_params=pltpu.CompilerParams(
            dimension_semantics=("arbitrary", "arbitrary"), vmem_limit_bytes=VMEM_LIMIT),
        name="merge",
    )(x, mod3, mod3, mod3, g_pre, w_g, y, bonus, gate, lnx_w, lnx_b, w_a, o_att, w_b, w_out, g_post)


def _ffn_kernel(x_ref, shift_ref, scale_ref, gatef_ref, g_ref, win_ref, wout_ref, gpost_ref, out_ref):
    dff = wout_ref.shape[0]
    subs = range(x_ref.shape[1] // SUB_ROWS)
    rows = [slice(sb * SUB_ROWS, (sb + 1) * SUB_ROWS) for sb in subs]
    h = [_bf(_norm_modulate(x_ref[0, rows[sb], :], g_ref[...], shift_ref[0], scale_ref[0])) for sb in subs]
    u = [jnp.dot(h[sb], win_ref[:, :dff], preferred_element_type=F32) for sb in subs]
    gt = [jnp.dot(h[sb], win_ref[:, dff:], preferred_element_type=F32) for sb in subs]
    act = [_bf(u[sb] * _sigmoid(u[sb]) * gt[sb]) for sb in subs]
    f = [jnp.dot(act[sb], wout_ref[...], preferred_element_type=F32) for sb in subs]
    for sb in subs:
        out_ref[0, rows[sb], :] = x_ref[0, rows[sb], :] + gatef_ref[0] * _rms(f[sb], gpost_ref[...])


def _ffn(x, mod3, g_pre, w_in, w_out, g_post, tm):
    bsz, seq, d = x.shape
    dff = w_out.shape[0]
    const = lambda shape: pl.BlockSpec(shape, lambda b, i: (0, 0), pipeline_mode=pl.Buffered(1))
    row = pl.BlockSpec((1, tm, d), lambda b, i: (b, i, 0))
    modv = lambda j: pl.BlockSpec((1, 1, d), lambda b, i: (b, 0, j))
    return pl.pallas_call(
        _ffn_kernel,
        grid=(bsz, seq // tm),
        in_specs=[row, modv(3), modv(4), modv(5), const((1, d)), const((d, 2 * dff)),
                  const((dff, d)), const((1, d))],
        out_specs=row,
        out_shape=jax.ShapeDtypeStruct((bsz, seq, d), F32),
        compiler_params=pltpu.CompilerParams(
            dimension_semantics=("arbitrary", "arbitrary"), vmem_limit_bytes=VMEM_LIMIT),
        name="ffn",
    )(x, mod3, mod3, mod3, g_pre, w_in, w_out, g_post)


def _rope_tables(seq):
    half = HEAD_DIM // 2
    inv_freq = ROPE_THETA ** (-jnp.arange(half, dtype=F32) / half)
    ang = jnp.arange(seq, dtype=F32)[:, None] * inv_freq[None, :]
    cos, sin = jnp.cos(ang), jnp.sin(ang)
    reps = LANES // HEAD_DIM
    cos_t = jnp.tile(jnp.concatenate([cos, cos], axis=-1), (1, reps))
    sin_t = jnp.tile(jnp.concatenate([-sin, sin], axis=-1), (1, reps))
    return cos_t, sin_t


def _layer(x, c, w_ada, b_ada, g_pre_mix, g_post_mix, w_in, mu_shift, w0, w2, a0, a2, g2,
           k_k, k_a, r_k, lnx_w, lnx_b, w_a, w_b, w_out, g_pre_ffn, g_post_ffn, w_ffn_in, w_ffn_out):
    bsz, seq, d = x.shape
    rw = RWKV_WIDTH
    row1 = lambda t: t.reshape(1, -1)
    mod3 = _mod(c, w_ada, b_ada).reshape(bsz, 1, 6 * d)

    w_rwkv = _bf(w_in[:, :SHIFT_WIDTH])
    w_qkv = _bf(w_in[:, SHIFT_WIDTH:SHIFT_WIDTH + 3 * ATTN_WIDTH])
    w_gates = _bf(w_in[:, SHIFT_WIDTH + 3 * ATTN_WIDTH:])
    w_lora = jnp.zeros((LORA_WIDTH, 3 * rw), F32)
    w_lora = w_lora.at[:DECAY_LORA, :rw].set(w2)
    w_lora = w_lora.at[DECAY_LORA:DECAY_LORA + ICLR_LORA, rw:2 * rw].set(a2)
    w_lora = _bf(w_lora.at[DECAY_LORA + ICLR_LORA:, 2 * rw:].set(g2))

    prep = _rwkv_prep(x, mod3, row1(g_pre_mix), w_rwkv, row1(mu_shift), w_lora, row1(w0), row1(a0),
                      row1(k_k), row1(k_a), row1(r_k), tm=512)
    r, k, v, nkk, b, cum, bonus, gate = prep
    y = _rwkv_scan(r, k, v, nkk, b, cum, rows_per_step=8, chunks_per_step=1)

    cos_t, sin_t = _rope_tables(seq)
    qkv = _attn_proj(x, mod3, row1(g_pre_mix), w_qkv, cos_t, sin_t, tm=1024)
    o_att = _attn(qkv, seq)

    x1 = _merge(x, mod3, row1(g_pre_mix), w_gates, y, bonus, gate, row1(lnx_w), row1(lnx_b),
                _bf(w_a), o_att, _bf(w_b), _bf(w_out), row1(g_post_mix), tm=512)
    return _ffn(x1, mod3, row1(g_pre_ffn), _bf(w_ffn_in), _bf(w_ffn_out), row1(g_post_ffn), tm=512)


def kernel(x, c, w_ada, b_ada, g_pre_mix, g_post_mix, w_in, mu_shift, w0, w2, a0, a2, g2, k_k, k_a, r_k,
           lnx_w, lnx_b, w_a, w_b, w_out, g_pre_ffn, g_post_ffn, w_ffn_in, w_ffn_out):
    for i in range(w_in.shape[0]):
        x = _layer(x, c, w_ada[i], b_ada[i], g_pre_mix[i], g_post_mix[i], w_in[i], mu_shift[i], w0[i],
                   w2[i], a0[i], a2[i], g2[i], k_k[i], k_a[i], r_k[i], lnx_w[i], lnx_b[i], w_a[i], w_b[i],
                   w_out[i], g_pre_ffn[i], g_post_ffn[i], w_ffn_in[i], w_ffn_out[i])
    return x
```

```python
import functools
import math

import jax
import jax.numpy as jnp
from jax import lax
from jax.experimental import pallas as pl
from jax.experimental.pallas import tpu as pltpu

D_MODEL = 1024
HEAD_DIM = 64
RWKV_HEADS = 12
RWKV_WIDTH = RWKV_HEADS * HEAD_DIM
DECAY_LORA = 64
ICLR_LORA = 64
GATE_LORA = 128
LORA_WIDTH = DECAY_LORA + ICLR_LORA + GATE_LORA
ATTN_GROUPS = ((128, 1), (512, 4), (2048, 16))
HEADS_PER_GROUP = 4
GROUP_WIDTH = HEADS_PER_GROUP * HEAD_DIM
ATTN_WIDTH = len(ATTN_GROUPS) * GROUP_WIDTH
ATTN_BLOCK = 128
ROPE_THETA = 10000.0
SHIFT_WIDTH = 3 * RWKV_WIDTH + LORA_WIDTH
NORM_EPS = 1e-6
GN_EPS = 64e-5

CHUNK = 64
INV_BASE = 16
LANES = 128
NEG_BIG = -1e30
LOG2E = math.log2(math.e)
SCAN_OPERAND_GROUPS = 4
ATTN_BLOCKS_PER_ITER = 8
SUB_ROWS = 256
ATTN_PROJ_SUB_ROWS = 512
VMEM_LIMIT = 56 * 1024 * 1024

F32 = jnp.float32
BF16 = jnp.bfloat16


def _bf(x):
    return x.astype(BF16)


def _div(x, n):
    assert n & (n - 1) == 0
    return jnp.right_shift(x, n.bit_length() - 1)


def _rem(x, n):
    assert n & (n - 1) == 0
    return jnp.bitwise_and(x, n - 1)


def _sigmoid(x):
    return 1.0 / (1.0 + jnp.exp2(x * (-LOG2E)))


def _split_hi_lo(x):
    hi = _bf(x)
    lo = _bf(x - hi.astype(F32))
    return hi, lo


def _mm_exact_lhs(m_bf, x):
    hi, lo = _split_hi_lo(x)
    return (jnp.dot(m_bf, hi, preferred_element_type=F32)
            + jnp.dot(m_bf, lo, preferred_element_type=F32))


def _head_sum(x):
    tile = 2 * LANES
    ri = _div(lax.broadcasted_iota(jnp.int32, (tile, tile), 0), HEAD_DIM)
    ci = _div(lax.broadcasted_iota(jnp.int32, (tile, tile), 1), HEAD_DIM)
    ones_bd = jnp.where(ri == ci, 1.0, 0.0).astype(BF16)
    xb = _bf(x)
    parts = [jnp.dot(xb[:, j * tile:(j + 1) * tile], ones_bd, preferred_element_type=F32)
             for j in range(x.shape[-1] // tile)]
    return jnp.concatenate(parts, axis=-1)


def _norm_modulate(x, g, shift, scale):
    y = x * lax.rsqrt(jnp.mean(x * x, axis=-1, keepdims=True) + NORM_EPS)
    return y * (g * (1.0 + scale)) + shift


def _rms(x, g):
    return x * lax.rsqrt(jnp.mean(x * x, axis=-1, keepdims=True) + NORM_EPS) * g


def _mod_kernel(c_ref, w_ref, b_ref, o_ref):
    c = c_ref[...]
    hi, lo = _split_hi_lo(c * _sigmoid(c))
    w = _bf(w_ref[...])
    o_ref[...] = (jnp.dot(hi, w, preferred_element_type=F32)
                  + jnp.dot(lo, w, preferred_element_type=F32) + b_ref[...])


def _mod(c, w_ada, b_ada):
    bsz, d = c.shape
    n = w_ada.shape[1]
    tn = d
    return pl.pallas_call(
        _mod_kernel,
        grid=(n // tn,),
        in_specs=[pl.BlockSpec((bsz, d), lambda j: (0, 0)),
                  pl.BlockSpec((d, tn), lambda j: (0, j)),
                  pl.BlockSpec((1, tn), lambda j: (0, j))],
        out_specs=pl.BlockSpec((bsz, tn), lambda j: (0, j)),
        out_shape=jax.ShapeDtypeStruct((bsz, n), F32),
        name="mod",
    )(c, w_ada, b_ada.reshape(1, n))


def _rwkv_prep_kernel(x_ref, shift_ref, scale_ref, g_ref, w_ref, mu_ref, wl_ref, w0_ref, a0_ref,
                      kk_ref, ka_ref, rk_ref,
                      r_o, k_o, v_o, nkk_o, b_o, cum_o, bonus_o, gate_o,
                      carry_ref):
    i = pl.program_id(1)
    rw = RWKV_WIDTH
    sub = SUB_ROWS

    @pl.when(i == 0)
    def _():
        carry_ref[...] = jnp.zeros_like(carry_ref)

    ri = lax.broadcasted_iota(jnp.int32, (sub, sub), 0)
    ci = lax.broadcasted_iota(jnp.int32, (sub, sub), 1)
    same = _div(ri, CHUNK) == _div(ci, CHUNK)
    lower = jnp.where(same & (ci <= ri), 1.0, 0.0).astype(BF16)
    row0 = lax.broadcasted_iota(jnp.int32, (8, SHIFT_WIDTH), 0) == 0
    assert DECAY_LORA + ICLR_LORA == LANES
    lane = lax.broadcasted_iota(jnp.int32, (sub, LANES), 1)

    subs = range(x_ref.shape[1] // sub)
    rows = [slice(sb * sub, (sb + 1) * sub) for sb in subs]
    h = [_bf(_norm_modulate(x_ref[0, rows[sb], :], g_ref[...], shift_ref[0], scale_ref[0])) for sb in subs]
    p = [jnp.dot(h[sb], w_ref[...], preferred_element_type=F32) for sb in subs]

    prev_row = carry_ref[7:8, :]
    ps = []
    for sb in subs:
        rolled = pltpu.roll(p[sb], 1, axis=0)
        shifted = jnp.concatenate([jnp.where(row0, prev_row, rolled[:8]), rolled[8:]], axis=0)
        last8 = p[sb][sub - 8:, :]
        prev_row = last8[7:8, :]
        ps.append(p[sb] + mu_ref[...] * (shifted - p[sb]))
    carry_ref[...] = last8

    z = []
    for sb in subs:
        wa_in = ps[sb][:, 3 * rw:3 * rw + LANES]
        g_in = ps[sb][:, 3 * rw + LANES:]
        z.append(_bf(jnp.concatenate([jnp.where(lane < DECAY_LORA, jnp.tanh(wa_in), wa_in),
                                      _sigmoid(g_in)], axis=1)))
    lo = [jnp.dot(z[sb], wl_ref[...], preferred_element_type=F32) for sb in subs]

    pr = [ps[sb][:, :rw] for sb in subs]
    pk = [ps[sb][:, rw:2 * rw] for sb in subs]
    pv = [ps[sb][:, 2 * rw:3 * rw] for sb in subs]
    kkr = [pk[sb] * kk_ref[...] for sb in subs]
    kk_sq = [_head_sum(kkr[sb] * kkr[sb]) for sb in subs]
    a = [_sigmoid(a0_ref[...] + lo[sb][:, rw:2 * rw]) for sb in subs]
    one_minus_ka = 1.0 - ka_ref[...]
    k = [pk[sb] * (one_minus_ka + a[sb] * ka_ref[...]) for sb in subs]
    rk_sum = [_head_sum(pr[sb] * k[sb] * rk_ref[...]) for sb in subs]
    lw = [(-math.exp(-0.5) * LOG2E) * _sigmoid(w0_ref[...] + lo[sb][:, :rw]) for sb in subs]
    cum = [_mm_exact_lhs(lower, lw[sb]) for sb in subs]

    for sb in subs:
        kkn = kkr[sb] * lax.rsqrt(jnp.maximum(kk_sq[sb], 1e-24))
        gate_o[0, rows[sb], :] = lo[sb][:, 2 * rw:]
        bonus_o[0, rows[sb], :] = rk_sum[sb] * pv[sb]
        cum_o[0, rows[sb], :] = cum[sb]
        r_o[0, rows[sb], :] = _bf(pr[sb])
        k_o[0, rows[sb], :] = _bf(k[sb])
        v_o[0, rows[sb], :] = _bf(pv[sb])
        nkk_o[0, rows[sb], :] = _bf(-kkn)
        b_o[0, rows[sb], :] = _bf(kkn * a[sb])


def _rwkv_prep(x, mod3, g_pre, w_rwkv, mu, w_lora, w0, a0, k_k, k_a, r_k, tm):
    bsz, seq, d = x.shape
    rw = RWKV_WIDTH
    row_spec = pl.BlockSpec((1, tm, rw), lambda b, i: (b, i, 0))
    vec = lambda n: pl.BlockSpec((1, n), lambda b, i: (0, 0))
    big = jax.ShapeDtypeStruct((bsz, seq, rw), F32)
    half = jax.ShapeDtypeStruct((bsz, seq, rw), BF16)
    return pl.pallas_call(
        _rwkv_prep_kernel,
        grid=(bsz, seq // tm),
        in_specs=[pl.BlockSpec((1, tm, d), lambda b, i: (b, i, 0)),
                  pl.BlockSpec((1, 1, d), lambda b, i: (b, 0, 0)),
                  pl.BlockSpec((1, 1, d), lambda b, i: (b, 0, 1)),
                  vec(d),
                  pl.BlockSpec((d, SHIFT_WIDTH), lambda b, i: (0, 0), pipeline_mode=pl.Buffered(1)),
                  vec(SHIFT_WIDTH),
                  pl.BlockSpec((LORA_WIDTH, 3 * rw), lambda b, i: (0, 0), pipeline_mode=pl.Buffered(1)),
                  vec(rw), vec(rw), vec(rw), vec(rw), vec(rw)],
        out_specs=[row_spec] * 8,
        out_shape=[half] * 5 + [big] * 3,
        scratch_shapes=[pltpu.VMEM((8, SHIFT_WIDTH), F32)],
        compiler_params=pltpu.CompilerParams(
            dimension_semantics=("arbitrary", "arbitrary"), vmem_limit_bytes=VMEM_LIMIT),
        name="rwkv_prep",
    )(x, mod3, mod3, g_pre, w_rwkv, mu, w_lora, w0, a0, k_k, k_a, r_k)


def _pair_blockdiag(x):
    first = lax.broadcasted_iota(jnp.int32, x.shape, 1) < HEAD_DIM
    zero = jnp.zeros_like(x)
    return jnp.concatenate([jnp.where(first, x, zero), jnp.where(first, zero, x)], axis=0)


def _pmm(a, b):
    return jnp.dot(_bf(a), _pair_blockdiag(_bf(b)), preferred_element_type=F32)


def _pmm_nt(a, b):
    return lax.dot_general(_bf(a), _pair_blockdiag(_bf(b)), (((1,), (1,)), ((), ())),
                           preferred_element_type=F32)


def _pmm_tn(a, b):
    full = lax.dot_general(_bf(a), _bf(b), (((0,), (0,)), ((), ())), preferred_element_type=F32)
    first = lax.broadcasted_iota(jnp.int32, (HEAD_DIM, LANES), 1) < HEAD_DIM
    return jnp.where(first, full[:HEAD_DIM], full[HEAD_DIM:])


def _unit_lower_inverse(a_list, masks):
    eye, diag_mask, merge_masks = masks
    ps = [jnp.where(diag_mask, a, 0.0) for a in a_list]
    ts = [eye + p for p in ps]
    ps = [_pmm(p, p) for p in ps]
    m = 2
    while 2 * m < INV_BASE:
        both = [_pmm(jnp.concatenate([p, t], axis=0), p) for p, t in zip(ps, ts)]
        ps = [x[:CHUNK] for x in both]
        ts = [t + x[CHUNK:] for t, x in zip(ts, both)]
        m *= 2
    ts = [t + _pmm(t, p) for t, p in zip(ts, ps)]
    for mk in merge_masks:
        ws = [_pmm(jnp.where(mk, a, 0.0), t) for a, t in zip(a_list, ts)]
        ts = [t + _pmm(t, w) for t, w in zip(ts, ws)]
    return ts


def _rwkv_scan_kernel(r_ref, k_ref, v_ref, nkk_ref, b_ref, cum_ref, y_ref, st_ref):
    @pl.when(pl.program_id(1) == 0)
    def _():
        st_ref[...] = jnp.zeros_like(st_ref)

    ri = lax.broadcasted_iota(jnp.int32, (CHUNK, LANES), 0)
    ci = _rem(lax.broadcasted_iota(jnp.int32, (CHUNK, LANES), 1), HEAD_DIM)
    strict = ri > ci
    incl = ri >= ci
    ri2 = lax.broadcasted_iota(jnp.int32, (2 * CHUNK, LANES), 0)
    ci2 = _rem(lax.broadcasted_iota(jnp.int32, (2 * CHUNK, LANES), 1), HEAD_DIM)
    rloc = _rem(ri2, CHUNK)
    stacked = (rloc > ci2) | ((rloc == ci2) & (ri2 >= CHUNK))
    eye = jnp.where(ri == ci, 1.0, 0.0)
    diag_mask = _div(ri, INV_BASE) == _div(ci, INV_BASE)
    merge_masks = []
    bsize = INV_BASE
    while bsize < CHUNK:
        merge_masks.append((_div(ri, 2 * bsize) == _div(ci, 2 * bsize))
                           & (_div(ri, bsize) != _div(ci, bsize)))
        bsize *= 2
    masks = (eye, diag_mask, merge_masks)

    n_rows = r_ref.shape[0]
    n_chunks = r_ref.shape[1] // CHUNK
    n_pairs = RWKV_WIDTH // LANES
    seqs = [(b, j) for b in range(n_rows) for j in range(n_pairs)]
    items = [(c, s) for c in range(n_chunks) for s in seqs]

    def tile(ref, c, s):
        b, j = s
        return ref[b, c * CHUNK:(c + 1) * CHUNK, j * LANES:(j + 1) * LANES]

    first_row = ri == 0
    nt = lambda a, b: lax.dot_general(a, b, (((1,), (1,)), ((), ())), preferred_element_type=F32)
    qs, bkg, gcs, ab, akv = {}, {}, {}, {}, {}
    n_groups = min(SCAN_OPERAND_GROUPS, len(items))
    for grp in range(n_groups):
        group = items[grp::n_groups]
        bk, kd = {}, {}
        for it in group:
            cum = tile(cum_ref, *it)
            cum_last = cum[CHUNK - 1:CHUNK, :]
            e_cum = jnp.exp2(cum)
            e_prev = jnp.exp2(jnp.where(first_row, 0.0, pltpu.roll(cum, 1, axis=0)))
            e_inv = jnp.exp2(-cum)
            e_rest = jnp.exp2(cum_last - cum)
            gcs[it] = jnp.exp2(cum_last)
            r, k = tile(r_ref, *it).astype(F32), tile(k_ref, *it).astype(F32)
            nkk, b = tile(nkk_ref, *it).astype(F32), tile(b_ref, *it).astype(F32)
            qs[it] = _bf(jnp.concatenate([nkk * e_prev, r * e_cum], axis=0))
            bk[it], kd[it] = _bf(b * e_inv), _bf(k * e_inv)
            bkg[it] = _bf(jnp.concatenate([b * e_rest, k * e_rest], axis=0))
        abk = {it: nt(qs[it], jnp.concatenate([_pair_blockdiag(bk[it]), _pair_blockdiag(kd[it])], axis=0))
               for it in group}
        for it in group:
            ab[it] = abk[it][:, :LANES]
        ak = {it: jnp.where(stacked, abk[it][:, LANES:], 0.0) for it in group}
        for it in group:
            akv[it] = _pmm(ak[it], tile(v_ref, *it))

    t_list = _unit_lower_inverse([jnp.where(strict, ab[it][:CHUNK], 0.0) for it in items], masks)
    t = dict(zip(items, t_list))

    state = {s: st_ref[s[0], s[1]] for s in seqs}
    for c in range(n_chunks):
        xs = {s: _pmm_nt(qs[(c, s)], state[s]) for s in seqs}
        u = {s: _pmm(t[(c, s)], xs[s][:CHUNK] + akv[(c, s)][:CHUNK]) for s in seqs}
        for s in seqs:
            b, j = s
            y = xs[s][CHUNK:] + akv[(c, s)][CHUNK:] + _pmm(
                jnp.where(incl, ab[(c, s)][CHUNK:], 0.0), u[s])
            y_ref[b, c * CHUNK:(c + 1) * CHUNK, j * LANES:(j + 1) * LANES] = y
        new_state = {}
        for s in seqs:
            uv = jnp.concatenate([_bf(u[s]), tile(v_ref, c, s)], axis=0)
            new_state[s] = state[s] * gcs[(c, s)] + _pmm_tn(uv, bkg[(c, s)])
        state = new_state
    for s in seqs:
        st_ref[s[0], s[1]] = state[s]


def _rwkv_scan(r, k, v, nkk, b, cum, rows_per_step, chunks_per_step):
    bsz, seq, rw = r.shape
    ts = chunks_per_step * CHUNK
    row_spec = pl.BlockSpec((rows_per_step, ts, rw), lambda b, c: (b, c, 0))
    return pl.pallas_call(
        _rwkv_scan_kernel,
        grid=(bsz // rows_per_step, seq // ts),
        in_specs=[row_spec] * 6,
        out_specs=row_spec,
        out_shape=jax.ShapeDtypeStruct((bsz, seq, rw), F32),
        scratch_shapes=[pltpu.VMEM((rows_per_step, rw // LANES, HEAD_DIM, LANES), F32)],
        compiler_params=pltpu.CompilerParams(dimension_semantics=("arbitrary", "arbitrary")),
        name="rwkv_scan",
    )(r, k, v, nkk, b, cum)


def _attn_proj_kernel(x_ref, shift_ref, scale_ref, g_ref, w_ref, cos_ref, sin_ref, *rest):
    outs, buf = rest[:-1], rest[-1]
    SUB_ROWS = ATTN_PROJ_SUB_ROWS
    aw = ATTN_WIDTH
    reps = aw // LANES
    lane = lax.broadcasted_iota(jnp.int32, (SUB_ROWS, aw), 1)
    first_half = _rem(lane, HEAD_DIM) < (HEAD_DIM // 2)
    tiles_per_group = GROUP_WIDTH // LANES

    subs = range(x_ref.shape[1] // SUB_ROWS)
    ps = []
    for sb in subs:
        rows = slice(sb * SUB_ROWS, (sb + 1) * SUB_ROWS)
        h = _norm_modulate(x_ref[0, rows, :], g_ref[...], shift_ref[0], scale_ref[0])
        ps.append(jnp.dot(_bf(h), w_ref[...], preferred_element_type=F32))
    for sb in subs:
        rows = slice(sb * SUB_ROWS, (sb + 1) * SUB_ROWS)
        p = ps[sb]
        cos_t, sin_t = cos_ref[rows, :], sin_ref[rows, :]

        def rope(t, scale):
            cos = jnp.concatenate([cos_t * scale] * reps, axis=-1)
            sin = jnp.concatenate([sin_t * scale] * reps, axis=-1)
            partner = jnp.where(first_half, pltpu.roll(t, aw - HEAD_DIM // 2, axis=1),
                                pltpu.roll(t, HEAD_DIM // 2, axis=1))
            return t * cos + partner * sin

        roped = (rope(p[:, :aw], HEAD_DIM ** -0.5 * LOG2E), rope(p[:, aw:2 * aw], 1.0), p[:, 2 * aw:])
        n = 0
        for which in range(3):
            for g, (_, dil) in enumerate(ATTN_GROUPS):
                o_ref = outs[n]
                n += 1
                out_rows = slice(sb * SUB_ROWS // dil, (sb + 1) * SUB_ROWS // dil)
                for ct in range(tiles_per_group):
                    c0 = g * GROUP_WIDTH + ct * LANES
                    val = roped[which][:, c0:c0 + LANES]
                    if dil == 1:
                        o_ref[0, 0, out_rows, ct * LANES:(ct + 1) * LANES] = _bf(val)
                        continue
                    slot = (which * len(ATTN_GROUPS) + g) * tiles_per_group + ct
                    buf[sb, slot] = val
                    for r in range(dil):
                        o_ref[0, r, out_rows, ct * LANES:(ct + 1) * LANES] = _bf(
                            buf[sb, slot, pl.ds(r, SUB_ROWS // dil, stride=dil), :])


def _attn_proj(x, mod3, g_pre, w_qkv, cos_t, sin_t, tm):
    bsz, seq, d = x.shape
    out_specs, out_shape = [], []
    for _ in range(3):
        for _, dil in ATTN_GROUPS:
            out_specs.append(pl.BlockSpec((1, dil, tm // dil, GROUP_WIDTH), lambda b, i: (b, 0, i, 0)))
            out_shape.append(jax.ShapeDtypeStruct((bsz, dil, seq // dil, GROUP_WIDTH), BF16))
    return pl.pallas_call(
        _attn_proj_kernel,
        grid=(bsz, seq // tm),
        in_specs=[pl.BlockSpec((1, tm, d), lambda b, i: (b, i, 0)),
                  pl.BlockSpec((1, 1, d), lambda b, i: (b, 0, 0)),
                  pl.BlockSpec((1, 1, d), lambda b, i: (b, 0, 1)),
                  pl.BlockSpec((1, d), lambda b, i: (0, 0)),
                  pl.BlockSpec((d, 3 * ATTN_WIDTH), lambda b, i: (0, 0), pipeline_mode=pl.Buffered(1)),
                  pl.BlockSpec((tm, LANES), lambda b, i: (i, 0)),
                  pl.BlockSpec((tm, LANES), lambda b, i: (i, 0))],
        out_specs=out_specs,
        out_shape=out_shape,
        scratch_shapes=[pltpu.VMEM((tm // ATTN_PROJ_SUB_ROWS, 3 * ATTN_WIDTH // LANES,
                                    ATTN_PROJ_SUB_ROWS, LANES), F32)],
        compiler_params=pltpu.CompilerParams(
            dimension_semantics=("arbitrary", "arbitrary"), vmem_limit_bytes=VMEM_LIMIT),
        name="attn_proj",
    )(x, mod3, mod3, g_pre, w_qkv, cos_t, sin_t)


def _attn_kernel(*refs):
    qkv = refs[:9]
    o_ref = refs[9]
    num_ref, m_ref, den_ref = refs[10:]
    blk = ATTN_BLOCK
    qi = lax.broadcasted_iota(jnp.int32, (blk, blk), 0)
    kj = lax.broadcasted_iota(jnp.int32, (blk, blk), 1)
    bias_cur = jnp.where(kj <= qi, 0.0, NEG_BIG)
    bias_prev = jnp.where(kj >= qi, 0.0, NEG_BIG)
    lane = lax.broadcasted_iota(jnp.int32, (blk, LANES), 1)
    head0 = lane < HEAD_DIM

    order = sorted(range(len(ATTN_GROUPS)), key=lambda g: -ATTN_GROUPS[g][1])
    assert ATTN_GROUPS[order[-1]][1] == 1
    for g in order:
        dil = ATTN_GROUPS[g][1]
        q_ref, k_ref, v_ref = qkv[g], qkv[3 + g], qkv[6 + g]
        nb = q_ref.shape[2] // blk

        nper = min(nb, ATTN_BLOCKS_PER_ITER)
        rper = ATTN_BLOCKS_PER_ITER // nper
        whole = nper == nb

        def body(idx, carry, q_ref=q_ref, k_ref=k_ref, v_ref=v_ref, dil=dil, nb=nb, g=g,
                 nper=nper, rper=rper, whole=whole):
            if whole:
                r0, n0 = idx * rper, 0
                base = 0
            else:
                r0 = _div(idx, nb // nper)
                n0 = _rem(idx, nb // nper) * nper
                base = pl.multiple_of(n0 * blk, blk)
            nt = lambda a, b: lax.dot_general(a, b, (((1,), (1,)), ((), ())), preferred_element_type=F32)
            one = jnp.ones((blk, LANES), BF16)
            zero = jnp.zeros((blk, LANES), BF16)
            q_msk, ks, v_aug = {}, {}, {}
            for rr in range(rper):
                r = r0 + rr
                first = 0 if whole else -1
                for i in range(first, nper):
                    if i < 0:
                        st = pl.multiple_of(jnp.maximum(n0 - 1, 0) * blk, blk)
                    else:
                        st = base + i * blk
                    ks[(rr, i)] = k_ref[0, r, pl.ds(st, blk), :]
                    v = v_ref[0, r, pl.ds(st, blk), :]
                    v_aug[(rr, i)] = (jnp.where(head0, v, one), jnp.where(head0, one, v))
                    if i >= 0:
                        q = q_ref[0, r, pl.ds(st, blk), :]
                        q_msk[(rr, i)] = (jnp.where(head0, q, zero), jnp.where(head0, zero, q))
            items = [(rr, i, e) for rr in range(rper) for i in range(nper) for e in range(2)]
            has_prev = lambda it: (it[0], it[1] - 1) in ks
            prev_of = lambda it: (it[0], it[1] - 1)
            k_win = {it[:2]: jnp.concatenate([ks[prev_of(it)], ks[it[:2]]], axis=0)
                     for it in items if has_prev(it)}
            bias_win = jnp.concatenate([bias_prev, bias_cur], axis=1)
            bias_first = bias_win if whole else jnp.where(n0 > 0, bias_win, jnp.concatenate(
                [jnp.full_like(bias_prev, NEG_BIG), bias_cur], axis=1))
            s, ms, ps, pv = {}, {}, {}, {}

            def scores(it):
                s[it] = nt(q_msk[it[:2]][it[2]], k_win[it[:2]] if has_prev(it) else ks[it[:2]])

            def softmax(it):
                if has_prev(it):
                    sb = s.pop(it) + (bias_first if it[1] == 0 else bias_win)
                else:
                    sb = s.pop(it) + bias_cur
                m = jnp.max(sb, axis=-1, keepdims=True)
                ms[it] = m
                ps[it] = _bf(jnp.exp2(sb - m))

            def weighted_values(it):
                if has_prev(it):
                    v_win = jnp.concatenate([v_aug[prev_of(it)][it[2]], v_aug[it[:2]][it[2]]], axis=0)
                else:
                    v_win = v_aug[it[:2]][it[2]]
                pv[it] = jnp.dot(ps.pop(it), v_win, preferred_element_type=F32)

            for it in items:
                scores(it)
            for it in items:
                softmax(it)
            for it in items:
                weighted_values(it)
            for rr in range(rper):
                for i in range(nper):
                    m_b = jnp.where(head0, ms[(rr, i, 0)], ms[(rr, i, 1)])
                    num_b = jnp.where(head0, pv[(rr, i, 0)], pv[(rr, i, 1)])
                    l_b = pltpu.roll(jnp.where(head0, pv[(rr, i, 1)], pv[(rr, i, 0)]), HEAD_DIM, axis=1)
                    if dil == 1:
                        rows = pl.ds(base + i * blk, blk)
                    else:
                        rows = pl.ds((n0 + i) * (blk * dil) + r0 + rr, blk, stride=dil)
                    if g == order[0]:
                        num_ref[rows, :] = num_b
                        m_ref[rows, :] = m_b
                        den_ref[rows, :] = l_b
                        continue
                    m_old = m_ref[rows, :]
                    m_new = jnp.maximum(m_old, m_b)
                    w_old = jnp.exp2(m_old - m_new)
                    w_b = jnp.exp2(m_b - m_new)
                    num = num_ref[rows, :] * w_old + num_b * w_b
                    den = den_ref[rows, :] * w_old + l_b * w_b
                    if g == order[-1]:
                        o_ref[0, rows, :] = _bf(num / den)
                    else:
                        num_ref[rows, :] = num
                        den_ref[rows, :] = den
                        m_ref[rows, :] = m_new
            return carry

        lax.fori_loop(0, dil * nb // (nper * rper), body, 0, unroll=2)


def _attn(qkv, seq):
    bsz = qkv[0].shape[0]
    in_specs = []
    for _ in range(3):
        for _, dil in ATTN_GROUPS:
            in_specs.append(pl.BlockSpec((1, dil, seq // dil, LANES), lambda b, sp: (b, 0, 0, sp)))
    return pl.pallas_call(
        _attn_kernel,
        grid=(bsz, GROUP_WIDTH // LANES),
        in_specs=in_specs,
        out_specs=pl.BlockSpec((1, seq, LANES), lambda b, sp: (b, 0, sp)),
        out_shape=jax.ShapeDtypeStruct((bsz, seq, GROUP_WIDTH), BF16),
        scratch_shapes=[pltpu.VMEM((seq, LANES), F32)] * 3,
        compiler_params=pltpu.CompilerParams(
            dimension_semantics=("arbitrary", "arbitrary"), vmem_limit_bytes=VMEM_LIMIT),
        name="attn",
    )(*qkv)


def _merge_kernel(x_ref, shift_ref, scale_ref, gatem_ref, g_ref, wg_ref, y_ref, bonus_ref, gate_ref,
                  lnw_ref, lnb_ref, wa_ref, o_ref, wb_ref, wout_ref, gpost_ref, out_ref):
    inv_n = 1.0 / HEAD_DIM
    subs = range(x_ref.shape[1] // SUB_ROWS)
    rows = [slice(sb * SUB_ROWS, (sb + 1) * SUB_ROWS) for sb in subs]
    yb = [jnp.dot(o_ref[0, rows[sb], :], wb_ref[...], preferred_element_type=F32) for sb in subs]
    y = [y_ref[0, rows[sb], :] for sb in subs]
    mean = [_head_sum(y[sb]) * inv_n for sb in subs]
    h = [_bf(_norm_modulate(x_ref[0, rows[sb], :], g_ref[...], shift_ref[0], scale_ref[0])) for sb in subs]
    pg = [jnp.dot(h[sb], wg_ref[...], preferred_element_type=F32) for sb in subs]
    yc = [y[sb] - mean[sb] for sb in subs]
    var = [_head_sum(yc[sb] * yc[sb]) * inv_n for sb in subs]
    pre = [_bf((yc[sb] * lax.rsqrt(var[sb] + GN_EPS) * lnw_ref[...] + lnb_ref[...]
                + bonus_ref[0, rows[sb], :]) * gate_ref[0, rows[sb], :]) for sb in subs]
    ya = [jnp.dot(pre[sb], wa_ref[...], preferred_element_type=F32) for sb in subs]
    merged = [_bf(_sigmoid(pg[sb][:, :D_MODEL]) * ya[sb] + _sigmoid(pg[sb][:, D_MODEL:]) * yb[sb])
              for sb in subs]
    zz = [jnp.dot(merged[sb], wout_ref[...], preferred_element_type=F32) for sb in subs]
    for sb in subs:
        out_ref[0, rows[sb], :] = x_ref[0, rows[sb], :] + gatem_ref[0] * _rms(zz[sb], gpost_ref[...])


def _merge(x, mod3, g_pre, w_g, y, bonus, gate, lnx_w, lnx_b, w_a, o_att, w_b, w_out, g_post, tm):
    bsz, seq, d = x.shape
    rw = RWKV_WIDTH
    const = lambda shape: pl.BlockSpec(shape, lambda b, i: (0, 0), pipeline_mode=pl.Buffered(1))
    row = lambda n: pl.BlockSpec((1, tm, n), lambda b, i: (b, i, 0))
    modv = lambda j: pl.BlockSpec((1, 1, d), lambda b, i: (b, 0, j))
    return pl.pallas_call(
        _merge_kernel,
        grid=(bsz, seq // tm),
        in_specs=[row(d), modv(0), modv(1), modv(2), const((1, d)), const((d, 2 * d)),
                  row(rw), row(rw), row(rw), const((1, rw)), const((1, rw)), const((rw, d)),
                  row(GROUP_WIDTH), const((GROUP_WIDTH, d)), const((d, d)), const((1, d))],
        out_specs=row(d),
        out_shape=jax.ShapeDtypeStruct((bsz, seq, d), F32),
        compiler_params=pltpu.CompilerParams(
            dimension_semantics=("arbitrary", "arbitrary"), vmem_limit_bytes=VMEM_LIMIT),
        name="merge",
    )(x, mod3, mod3, mod3, g_pre, w_g, y, bonus, gate, lnx_w, lnx_b, w_a, o_att, w_b, w_out, g_post)


def _ffn_kernel(x_ref, shift_ref, scale_ref, gatef_ref, g_ref, win_ref, wout_ref, gpost_ref, out_ref):
    dff = wout_ref.shape[0]
    subs = range(x_ref.shape[1] // SUB_ROWS)
    rows = [slice(sb * SUB_ROWS, (sb + 1) * SUB_ROWS) for sb in subs]
    h = [_bf(_norm_modulate(x_ref[0, rows[sb], :], g_ref[...], shift_ref[0], scale_ref[0])) for sb in subs]
    u = [jnp.dot(h[sb], win_ref[:, :dff], preferred_element_type=F32) for sb in subs]
    gt = [jnp.dot(h[sb], win_ref[:, dff:], preferred_element_type=F32) for sb in subs]
    act = [_bf(u[sb] * _sigmoid(u[sb]) * gt[sb]) for sb in subs]
    f = [jnp.dot(act[sb], wout_ref[...], preferred_element_type=F32) for sb in subs]
    for sb in subs:
        out_ref[0, rows[sb], :] = x_ref[0, rows[sb], :] + gatef_ref[0] * _rms(f[sb], gpost_ref[...])


def _ffn(x, mod3, g_pre, w_in, w_out, g_post, tm):
    bsz, seq, d = x.shape
    dff = w_out.shape[0]
    const = lambda shape: pl.BlockSpec(shape, lambda b, i: (0, 0), pipeline_mode=pl.Buffered(1))
    row = pl.BlockSpec((1, tm, d), lambda b, i: (b, i, 0))
    modv = lambda j: pl.BlockSpec((1, 1, d), lambda b, i: (b, 0, j))
    return pl.pallas_call(
        _ffn_kernel,
        grid=(bsz, seq // tm),
        in_specs=[row, modv(3), modv(4), modv(5), const((1, d)), const((d, 2 * dff)),
                  const((dff, d)), const((1, d))],
        out_specs=row,
        out_shape=jax.ShapeDtypeStruct((bsz, seq, d), F32),
        compiler_params=pltpu.CompilerParams(
            dimension_semantics=("arbitrary", "arbitrary"), vmem_limit_bytes=VMEM_LIMIT),
        name="ffn",
    )(x, mod3, mod3, mod3, g_pre, w_in, w_out, g_post)


def _rope_tables(seq):
    half = HEAD_DIM // 2
    inv_freq = ROPE_THETA ** (-jnp.arange(half, dtype=F32) / half)
    ang = jnp.arange(seq, dtype=F32)[:, None] * inv_freq[None, :]
    cos, sin = jnp.cos(ang), jnp.sin(ang)
    reps = LANES // HEAD_DIM
    cos_t = jnp.tile(jnp.concatenate([cos, cos], axis=-1), (1, reps))
    sin_t = jnp.tile(jnp.concatenate([-sin, sin], axis=-1), (1, reps))
    return cos_t, sin_t


def _layer(x, c, w_ada, b_ada, g_pre_mix, g_post_mix, w_in, mu_shift, w0, w2, a0, a2, g2,
           k_k, k_a, r_k, lnx_w, lnx_b, w_a, w_b, w_out, g_pre_ffn, g_post_ffn, w_ffn_in, w_ffn_out):
    bsz, seq, d = x.shape
    rw = RWKV_WIDTH
    row1 = lambda t: t.reshape(1, -1)
    mod3 = _mod(c, w_ada, b_ada).reshape(bsz, 1, 6 * d)

    w_rwkv = _bf(w_in[:, :SHIFT_WIDTH])
    w_qkv = _bf(w_in[:, SHIFT_WIDTH:SHIFT_WIDTH + 3 * ATTN_WIDTH])
    w_gates = _bf(w_in[:, SHIFT_WIDTH + 3 * ATTN_WIDTH:])
    w_lora = jnp.zeros((LORA_WIDTH, 3 * rw), F32)
    w_lora = w_lora.at[:DECAY_LORA, :rw].set(w2)
    w_lora = w_lora.at[DECAY_LORA:DECAY_LORA + ICLR_LORA, rw:2 * rw].set(a2)
    w_lora = _bf(w_lora.at[DECAY_LORA + ICLR_LORA:, 2 * rw:].set(g2))

    prep = _rwkv_prep(x, mod3, row1(g_pre_mix), w_rwkv, row1(mu_shift), w_lora, row1(w0), row1(a0),
                      row1(k_k), row1(k_a), row1(r_k), tm=512)
    r, k, v, nkk, b, cum, bonus, gate = prep
    y = _rwkv_scan(r, k, v, nkk, b, cum, rows_per_step=8, chunks_per_step=1)

    cos_t, sin_t = _rope_tables(seq)
    qkv = _attn_proj(x, mod3, row1(g_pre_mix), w_qkv, cos_t, sin_t, tm=1024)
    o_att = _attn(qkv, seq)

    x1 = _merge(x, mod3, row1(g_pre_mix), w_gates, y, bonus, gate, row1(lnx_w), row1(lnx_b),
                _bf(w_a), o_att, _bf(w_b), _bf(w_out), row1(g_post_mix), tm=512)
    return _ffn(x1, mod3, row1(g_pre_ffn), _bf(w_ffn_in), _bf(w_ffn_out), row1(g_post_ffn), tm=512)


def kernel(x, c, w_ada, b_ada, g_pre_mix, g_post_mix, w_in, mu_shift, w0, w2, a0, a2, g2, k_k, k_a, r_k,
           lnx_w, lnx_b, w_a, w_b, w_out, g_pre_ffn, g_post_ffn, w_ffn_in, w_ffn_out):
    for i in range(w_in.shape[0]):
        x = _layer(x, c, w_ada[i], b_ada[i], g_pre_mix[i], g_post_mix[i], w_in[i], mu_shift[i], w0[i],
                   w2[i], a0[i], a2[i], g2[i], k_k[i], k_a[i], r_k[i], lnx_w[i], lnx_b[i], w_a[i], w_b[i],
                   w_out[i], g_pre_ffn[i], g_post_ffn[i], w_ffn_in[i], w_ffn_out[i])
    return x
```

```python
import functools
import math

import jax
import jax.numpy as jnp
from jax import lax
from jax.experimental import pallas as pl
from jax.experimental.pallas import tpu as pltpu

D_MODEL = 1024
HEAD_DIM = 64
RWKV_HEADS = 12
RWKV_WIDTH = RWKV_HEADS * HEAD_DIM
DECAY_LORA = 64
ICLR_LORA = 64
GATE_LORA = 128
LORA_WIDTH = DECAY_LORA + ICLR_LORA + GATE_LORA
ATTN_GROUPS = ((128, 1), (512, 4), (2048, 16))
HEADS_PER_GROUP = 4
GROUP_WIDTH = HEADS_PER_GROUP * HEAD_DIM
ATTN_WIDTH = len(ATTN_GROUPS) * GROUP_WIDTH
ATTN_BLOCK = 128
ROPE_THETA = 10000.0
SHIFT_WIDTH = 3 * RWKV_WIDTH + LORA_WIDTH
NORM_EPS = 1e-6
GN_EPS = 64e-5

CHUNK = 64
INV_BASE = 16
LANES = 128
NEG_BIG = -1e30
LOG2E = math.log2(math.e)
SCAN_OPERAND_GROUPS = 4
ATTN_BLOCKS_PER_ITER = 8
SUB_ROWS = 256
ATTN_PROJ_SUB_ROWS = 512
VMEM_LIMIT = 56 * 1024 * 1024

F32 = jnp.float32
BF16 = jnp.bfloat16


def _bf(x):
    return x.astype(BF16)


def _div(x, n):
    assert n & (n - 1) == 0
    return jnp.right_shift(x, n.bit_length() - 1)


def _rem(x, n):
    assert n & (n - 1) == 0
    return jnp.bitwise_and(x, n - 1)


def _sigmoid(x):
    return 1.0 / (1.0 + jnp.exp2(x * (-LOG2E)))


def _split_hi_lo(x):
    hi = _bf(x)
    lo = _bf(x - hi.astype(F32))
    return hi, lo


def _mm_exact_lhs(m_bf, x):
    hi, lo = _split_hi_lo(x)
    return (jnp.dot(m_bf, hi, preferred_element_type=F32)
            + jnp.dot(m_bf, lo, preferred_element_type=F32))


def _head_sum(x):
    tile = 2 * LANES
    ri = _div(lax.broadcasted_iota(jnp.int32, (tile, tile), 0), HEAD_DIM)
    ci = _div(lax.broadcasted_iota(jnp.int32, (tile, tile), 1), HEAD_DIM)
    ones_bd = jnp.where(ri == ci, 1.0, 0.0).astype(BF16)
    xb = _bf(x)
    parts = [jnp.dot(xb[:, j * tile:(j + 1) * tile], ones_bd, preferred_element_type=F32)
             for j in range(x.shape[-1] // tile)]
    return jnp.concatenate(parts, axis=-1)


def _norm_modulate(x, g, shift, scale):
    y = x * lax.rsqrt(jnp.mean(x * x, axis=-1, keepdims=True) + NORM_EPS)
    return y * (g * (1.0 + scale)) + shift


def _rms(x, g):
    return x * lax.rsqrt(jnp.mean(x * x, axis=-1, keepdims=True) + NORM_EPS) * g


def _mod_kernel(c_ref, w_ref, b_ref, o_ref):
    c = c_ref[...]
    hi, lo = _split_hi_lo(c * _sigmoid(c))
    w = _bf(w_ref[...])
    o_ref[...] = (jnp.dot(hi, w, preferred_element_type=F32)
                  + jnp.dot(lo, w, preferred_element_type=F32) + b_ref[...])


def _mod(c, w_ada, b_ada):
    bsz, d = c.shape
    n = w_ada.shape[1]
    tn = d
    return pl.pallas_call(
        _mod_kernel,
        grid=(n // tn,),
        in_specs=[pl.BlockSpec((bsz, d), lambda j: (0, 0)),
                  pl.BlockSpec((d, tn), lambda j: (0, j)),
                  pl.BlockSpec((1, tn), lambda j: (0, j))],
        out_specs=pl.BlockSpec((bsz, tn), lambda j: (0, j)),
        out_shape=jax.ShapeDtypeStruct((bsz, n), F32),
        name="mod",
    )(c, w_ada, b_ada.reshape(1, n))


def _rwkv_prep_kernel(x_ref, shift_ref, scale_ref, g_ref, w_ref, mu_ref, wl_ref, w0_ref, a0_ref,
                      kk_ref, ka_ref, rk_ref,
                      r_o, k_o, v_o, nkk_o, b_o, cum_o, bonus_o, gate_o,
                      carry_ref):
    i = pl.program_id(1)
    rw = RWKV_WIDTH
    sub = SUB_ROWS

    @pl.when(i == 0)
    def _():
        carry_ref[...] = jnp.zeros_like(carry_ref)

    ri = lax.broadcasted_iota(jnp.int32, (sub, sub), 0)
    ci = lax.broadcasted_iota(jnp.int32, (sub, sub), 1)
    same = _div(ri, CHUNK) == _div(ci, CHUNK)
    lower = jnp.where(same & (ci <= ri), 1.0, 0.0).astype(BF16)
    row0 = lax.broadcasted_iota(jnp.int32, (8, SHIFT_WIDTH), 0) == 0
    assert DECAY_LORA + ICLR_LORA == LANES
    lane = lax.broadcasted_iota(jnp.int32, (sub, LANES), 1)

    subs = range(x_ref.shape[1] // sub)
    rows = [slice(sb * sub, (sb + 1) * sub) for sb in subs]
    h = [_bf(_norm_modulate(x_ref[0, rows[sb], :], g_ref[...], shift_ref[0], scale_ref[0])) for sb in subs]
    p = [jnp.dot(h[sb], w_ref[...], preferred_element_type=F32) for sb in subs]

    prev_row = carry_ref[7:8, :]
    ps = []
    for sb in subs:
        rolled = pltpu.roll(p[sb], 1, axis=0)
        shifted = jnp.concatenate([jnp.where(row0, prev_row, rolled[:8]), rolled[8:]], axis=0)
        last8 = p[sb][sub - 8:, :]
        prev_row = last8[7:8, :]
        ps.append(p[sb] + mu_ref[...] * (shifted - p[sb]))
    carry_ref[...] = last8

    z = []
    for sb in subs:
        wa_in = ps[sb][:, 3 * rw:3 * rw + LANES]
        g_in = ps[sb][:, 3 * rw + LANES:]
        z.append(_bf(jnp.concatenate([jnp.where(lane < DECAY_LORA, jnp.tanh(wa_in), wa_in),
                                      _sigmoid(g_in)], axis=1)))
    lo = [jnp.dot(z[sb], wl_ref[...], preferred_element_type=F32) for sb in subs]

    pr = [ps[sb][:, :rw] for sb in subs]
    pk = [ps[sb][:, rw:2 * rw] for sb in subs]
    pv = [ps[sb][:, 2 * rw:3 * rw] for sb in subs]
    kkr = [pk[sb] * kk_ref[...] for sb in subs]
    kk_sq = [_head_sum(kkr[sb] * kkr[sb]) for sb in subs]
    a = [_sigmoid(a0_ref[...] + lo[sb][:, rw:2 * rw]) for sb in subs]
    one_minus_ka = 1.0 - ka_ref[...]
    k = [pk[sb] * (one_minus_ka + a[sb] * ka_ref[...]) for sb in subs]
    rk_sum = [_head_sum(pr[sb] * k[sb] * rk_ref[...]) for sb in subs]
    lw = [(-math.exp(-0.5) * LOG2E) * _sigmoid(w0_ref[...] + lo[sb][:, :rw]) for sb in subs]
    cum = [_mm_exact_lhs(lower, lw[sb]) for sb in subs]

    for sb in subs:
        kkn = kkr[sb] * lax.rsqrt(jnp.maximum(kk_sq[sb], 1e-24))
        gate_o[0, rows[sb], :] = lo[sb][:, 2 * rw:]
        bonus_o[0, rows[sb], :] = rk_sum[sb] * pv[sb]
        cum_o[0, rows[sb], :] = cum[sb]
        r_o[0, rows[sb], :] = _bf(pr[sb])
        k_o[0, rows[sb], :] = _bf(k[sb])
        v_o[0, rows[sb], :] = _bf(pv[sb])
        nkk_o[0, rows[sb], :] = _bf(-kkn)
        b_o[0, rows[sb], :] = _bf(kkn * a[sb])


def _rwkv_prep(x, mod3, g_pre, w_rwkv, mu, w_lora, w0, a0, k_k, k_a, r_k, tm):
    bsz, seq, d = x.shape
    rw = RWKV_WIDTH
    row_spec = pl.BlockSpec((1, tm, rw), lambda b, i: (b, i, 0))
    vec = lambda n: pl.BlockSpec((1, n), lambda b, i: (0, 0))
    big = jax.ShapeDtypeStruct((bsz, seq, rw), F32)
    half = jax.ShapeDtypeStruct((bsz, seq, rw), BF16)
    return pl.pallas_call(
        _rwkv_prep_kernel,
        grid=(bsz, seq // tm),
        in_specs=[pl.BlockSpec((1, tm, d), lambda b, i: (b, i, 0)),
                  pl.BlockSpec((1, 1, d), lambda b, i: (b, 0, 0)),
                  pl.BlockSpec((1, 1, d), lambda b, i: (b, 0, 1)),
                  vec(d),
                  pl.BlockSpec((d, SHIFT_WIDTH), lambda b, i: (0, 0), pipeline_mode=pl.Buffered(1)),
                  vec(SHIFT_WIDTH),
                  pl.BlockSpec((LORA_WIDTH, 3 * rw), lambda b, i: (0, 0), pipeline_mode=pl.Buffered(1)),
                  vec(rw), vec(rw), vec(rw), vec(rw), vec(rw)],
        out_specs=[row_spec] * 8,
        out_shape=[half] * 5 + [big] * 3,
        scratch_shapes=[pltpu.VMEM((8, SHIFT_WIDTH), F32)],
        compiler_params=pltpu.CompilerParams(
            dimension_semantics=("arbitrary", "arbitrary"), vmem_limit_bytes=VMEM_LIMIT),
        name="rwkv_prep",
    )(x, mod3, mod3, g_pre, w_rwkv, mu, w_lora, w0, a0, k_k, k_a, r_k)


def _pair_blockdiag(x):
    first = lax.broadcasted_iota(jnp.int32, x.shape, 1) < HEAD_DIM
    zero = jnp.zeros_like(x)
    return jnp.concatenate([jnp.where(first, x, zero), jnp.where(first, zero, x)], axis=0)


def _pmm(a, b):
    return jnp.dot(_bf(a), _pair_blockdiag(_bf(b)), preferred_element_type=F32)


def _pmm_nt(a, b):
    return lax.dot_general(_bf(a), _pair_blockdiag(_bf(b)), (((1,), (1,)), ((), ())),
                           preferred_element_type=F32)


def _pmm_tn(a, b):
    full = lax.dot_general(_bf(a), _bf(b), (((0,), (0,)), ((), ())), preferred_element_type=F32)
    first = lax.broadcasted_iota(jnp.int32, (HEAD_DIM, LANES), 1) < HEAD_DIM
    return jnp.where(first, full[:HEAD_DIM], full[HEAD_DIM:])


def _unit_lower_inverse(a_list, masks):
    eye, diag_mask, merge_masks = masks
    ps = [jnp.where(diag_mask, a, 0.0) for a in a_list]
    ts = [eye + p for p in ps]
    ps = [_pmm(p, p) for p in ps]
    m = 2
    while 2 * m < INV_BASE:
        both = [_pmm(jnp.concatenate([p, t], axis=0), p) for p, t in zip(ps, ts)]
        ps = [x[:CHUNK] for x in both]
        ts = [t + x[CHUNK:] for t, x in zip(ts, both)]
        m *= 2
    ts = [t + _pmm(t, p) for t, p in zip(ts, ps)]
    for mk in merge_masks:
        ws = [_pmm(jnp.where(mk, a, 0.0), t) for a, t in zip(a_list, ts)]
        ts = [t + _pmm(t, w) for t, w in zip(ts, ws)]
    return ts


def _rwkv_scan_kernel(r_ref, k_ref, v_ref, nkk_ref, b_ref, cum_ref, y_ref, st_ref):
    @pl.when(pl.program_id(1) == 0)
    def _():
        st_ref[...] = jnp.zeros_like(st_ref)

    ri = lax.broadcasted_iota(jnp.int32, (CHUNK, LANES), 0)
    ci = _rem(lax.broadcasted_iota(jnp.int32, (CHUNK, LANES), 1), HEAD_DIM)
    strict = ri > ci
    incl = ri >= ci
    ri2 = lax.broadcasted_iota(jnp.int32, (2 * CHUNK, LANES), 0)
    ci2 = _rem(lax.broadcasted_iota(jnp.int32, (2 * CHUNK, LANES), 1), HEAD_DIM)
    rloc = _rem(ri2, CHUNK)
    stacked = (rloc > ci2) | ((rloc == ci2) & (ri2 >= CHUNK))
    eye = jnp.where(ri == ci, 1.0, 0.0)
    diag_mask = _div(ri, INV_BASE) == _div(ci, INV_BASE)
    merge_masks = []
    bsize = INV_BASE
    while bsize < CHUNK:
        merge_masks.append((_div(ri, 2 * bsize) == _div(ci, 2 * bsize))
                           & (_div(ri, bsize) != _div(ci, bsize)))
        bsize *= 2
    masks = (eye, diag_mask, merge_masks)

    n_rows = r_ref.shape[0]
    n_chunks = r_ref.shape[1] // CHUNK
    n_pairs = RWKV_WIDTH // LANES
    seqs = [(b, j) for b in range(n_rows) for j in range(n_pairs)]
    items = [(c, s) for c in range(n_chunks) for s in seqs]

    def tile(ref, c, s):
        b, j = s
        return ref[b, c * CHUNK:(c + 1) * CHUNK, j * LANES:(j + 1) * LANES]

    first_row = ri == 0
    nt = lambda a, b: lax.dot_general(a, b, (((1,), (1,)), ((), ())), preferred_element_type=F32)
    qs, bkg, gcs, ab, akv = {}, {}, {}, {}, {}
    n_groups = min(SCAN_OPERAND_GROUPS, len(items))
    for grp in range(n_groups):
        group = items[grp::n_groups]
        bk, kd = {}, {}
        for it in group:
            cum = tile(cum_ref, *it)
            cum_last = cum[CHUNK - 1:CHUNK, :]
            e_cum = jnp.exp2(cum)
            e_prev = jnp.exp2(jnp.where(first_row, 0.0, pltpu.roll(cum, 1, axis=0)))
            e_inv = jnp.exp2(-cum)
            e_rest = jnp.exp2(cum_last - cum)
            gcs[it] = jnp.exp2(cum_last)
            r, k = tile(r_ref, *it).astype(F32), tile(k_ref, *it).astype(F32)
            nkk, b = tile(nkk_ref, *it).astype(F32), tile(b_ref, *it).astype(F32)
            qs[it] = _bf(jnp.concatenate([nkk * e_prev, r * e_cum], axis=0))
            bk[it], kd[it] = _bf(b * e_inv), _bf(k * e_inv)
            bkg[it] = _bf(jnp.concatenate([b * e_rest, k * e_rest], axis=0))
        abk = {it: nt(qs[it], jnp.concatenate([_pair_blockdiag(bk[it]), _pair_blockdiag(kd[it])], axis=0))
               for it in group}
        for it in group:
            ab[it] = abk[it][:, :LANES]
        ak = {it: jnp.where(stacked, abk[it][:, LANES:], 0.0) for it in group}
        for it in group:
            akv[it] = _pmm(ak[it], tile(v_ref, *it))

    t_list = _unit_lower_inverse([jnp.where(strict, ab[it][:CHUNK], 0.0) for it in items], masks)
    t = dict(zip(items, t_list))

    state = {s: st_ref[s[0], s[1]] for s in seqs}
    for c in range(n_chunks):
        xs = {s: _pmm_nt(qs[(c, s)], state[s]) for s in seqs}
        u = {s: _pmm(t[(c, s)], xs[s][:CHUNK] + akv[(c, s)][:CHUNK]) for s in seqs}
        for s in seqs:
            b, j = s
            y = xs[s][CHUNK:] + akv[(c, s)][CHUNK:] + _pmm(
                jnp.where(incl, ab[(c, s)][CHUNK:], 0.0), u[s])
            y_ref[b, c * CHUNK:(c + 1) * CHUNK, j * LANES:(j + 1) * LANES] = y
        new_state = {}
        for s in seqs:
            uv = jnp.concatenate([_bf(u[s]), tile(v_ref, c, s)], axis=0)
            new_state[s] = state[s] * gcs[(c, s)] + _pmm_tn(uv, bkg[(c, s)])
        state = new_state
    for s in seqs:
        st_ref[s[0], s[1]] = state[s]


def _rwkv_scan(r, k, v, nkk, b, cum, rows_per_step, chunks_per_step):
    bsz, seq, rw = r.shape
    ts = chunks_per_step * CHUNK
    row_spec = pl.BlockSpec((rows_per_step, ts, rw), lambda b, c: (b, c, 0))
    return pl.pallas_call(
        _rwkv_scan_kernel,
        grid=(bsz // rows_per_step, seq // ts),
        in_specs=[row_spec] * 6,
        out_specs=row_spec,
        out_shape=jax.ShapeDtypeStruct((bsz, seq, rw), F32),
        scratch_shapes=[pltpu.VMEM((rows_per_step, rw // LANES, HEAD_DIM, LANES), F32)],
        compiler_params=pltpu.CompilerParams(dimension_semantics=("arbitrary", "arbitrary")),
        name="rwkv_scan",
    )(r, k, v, nkk, b, cum)


def _attn_proj_kernel(x_ref, shift_ref, scale_ref, g_ref, w_ref, cos_ref, sin_ref, *rest):
    outs, buf = rest[:-1], rest[-1]
    SUB_ROWS = ATTN_PROJ_SUB_ROWS
    aw = ATTN_WIDTH
    reps = aw // LANES
    lane = lax.broadcasted_iota(jnp.int32, (SUB_ROWS, aw), 1)
    first_half = _rem(lane, HEAD_DIM) < (HEAD_DIM // 2)
    tiles_per_group = GROUP_WIDTH // LANES

    subs = range(x_ref.shape[1] // SUB_ROWS)
    ps = []
    for sb in subs:
        rows = slice(sb * SUB_ROWS, (sb + 1) * SUB_ROWS)
        h = _norm_modulate(x_ref[0, rows, :], g_ref[...], shift_ref[0], scale_ref[0])
        ps.append(jnp.dot(_bf(h), w_ref[...], preferred_element_type=F32))
    for sb in subs:
        rows = slice(sb * SUB_ROWS, (sb + 1) * SUB_ROWS)
        p = ps[sb]
        cos_t, sin_t = cos_ref[rows, :], sin_ref[rows, :]

        def rope(t, scale):
            cos = jnp.concatenate([cos_t * scale] * reps, axis=-1)
            sin = jnp.concatenate([sin_t * scale] * reps, axis=-1)
            partner = jnp.where(first_half, pltpu.roll(t, aw - HEAD_DIM // 2, axis=1),
                                pltpu.roll(t, HEAD_DIM // 2, axis=1))
            return t * cos + partner * sin

        roped = (rope(p[:, :aw], HEAD_DIM ** -0.5 * LOG2E), rope(p[:, aw:2 * aw], 1.0), p[:, 2 * aw:])
        n = 0
        for which in range(3):
            for g, (_, dil) in enumerate(ATTN_GROUPS):
                o_ref = outs[n]
                n += 1
                out_rows = slice(sb * SUB_ROWS // dil, (sb + 1) * SUB_ROWS // dil)
                for ct in range(tiles_per_group):
                    c0 = g * GROUP_WIDTH + ct * LANES
                    val = roped[which][:, c0:c0 + LANES]
                    if dil == 1:
                        o_ref[0, 0, out_rows, ct * LANES:(ct + 1) * LANES] = _bf(val)
                        continue
                    slot = (which * len(ATTN_GROUPS) + g) * tiles_per_group + ct
                    buf[sb, slot] = val
                    for r in range(dil):
                        o_ref[0, r, out_rows, ct * LANES:(ct + 1) * LANES] = _bf(
                            buf[sb, slot, pl.ds(r, SUB_ROWS // dil, stride=dil), :])


def _attn_proj(x, mod3, g_pre, w_qkv, cos_t, sin_t, tm):
    bsz, seq, d = x.shape
    out_specs, out_shape = [], []
    for _ in range(3):
        for _, dil in ATTN_GROUPS:
            out_specs.append(pl.BlockSpec((1, dil, tm // dil, GROUP_WIDTH), lambda b, i: (b, 0, i, 0)))
            out_shape.append(jax.ShapeDtypeStruct((bsz, dil, seq // dil, GROUP_WIDTH), BF16))
    return pl.pallas_call(
        _attn_proj_kernel,
        grid=(bsz, seq // tm),
        in_specs=[pl.BlockSpec((1, tm, d), lambda b, i: (b, i, 0)),
                  pl.BlockSpec((1, 1, d), lambda b, i: (b, 0, 0)),
                  pl.BlockSpec((1, 1, d), lambda b, i: (b, 0, 1)),
                  pl.BlockSpec((1, d), lambda b, i: (0, 0)),
                  pl.BlockSpec((pl.Element(d), pl.Element(3 * ATTN_WIDTH)), lambda b, i: (0, SHIFT_WIDTH),
                               pipeline_mode=pl.Buffered(1)),
                  pl.BlockSpec((tm, LANES), lambda b, i: (i, 0)),
                  pl.BlockSpec((tm, LANES), lambda b, i: (i, 0))],
        out_specs=out_specs,
        out_shape=out_shape,
        scratch_shapes=[pltpu.VMEM((tm // ATTN_PROJ_SUB_ROWS, 3 * ATTN_WIDTH // LANES,
                                    ATTN_PROJ_SUB_ROWS, LANES), F32)],
        compiler_params=pltpu.CompilerParams(
            dimension_semantics=("arbitrary", "arbitrary"), vmem_limit_bytes=VMEM_LIMIT),
        name="attn_proj",
    )(x, mod3, mod3, g_pre, w_qkv, cos_t, sin_t)


def _attn_kernel(*refs):
    qkv = refs[:9]
    o_ref = refs[9]
    num_ref, m_ref, den_ref = refs[10:]
    blk = ATTN_BLOCK
    qi = lax.broadcasted_iota(jnp.int32, (blk, blk), 0)
    kj = lax.broadcasted_iota(jnp.int32, (blk, blk), 1)
    bias_cur = jnp.where(kj <= qi, 0.0, NEG_BIG)
    bias_prev = jnp.where(kj >= qi, 0.0, NEG_BIG)
    lane = lax.broadcasted_iota(jnp.int32, (blk, LANES), 1)
    head0 = lane < HEAD_DIM

    order = sorted(range(len(ATTN_GROUPS)), key=lambda g: -ATTN_GROUPS[g][1])
    assert ATTN_GROUPS[order[-1]][1] == 1
    for g in order:
        dil = ATTN_GROUPS[g][1]
        q_ref, k_ref, v_ref = qkv[g], qkv[3 + g], qkv[6 + g]
        nb = q_ref.shape[2] // blk

        nper = min(nb, ATTN_BLOCKS_PER_ITER)
        rper = ATTN_BLOCKS_PER_ITER // nper
        whole = nper == nb

        def body(idx, carry, q_ref=q_ref, k_ref=k_ref, v_ref=v_ref, dil=dil, nb=nb, g=g,
                 nper=nper, rper=rper, whole=whole):
            if whole:
                r0, n0 = idx * rper, 0
                base = 0
            else:
                r0 = _div(idx, nb // nper)
                n0 = _rem(idx, nb // nper) * nper
                base = pl.multiple_of(n0 * blk, blk)
            nt = lambda a, b: lax.dot_general(a, b, (((1,), (1,)), ((), ())), preferred_element_type=F32)
            one = jnp.ones((blk, LANES), BF16)
            zero = jnp.zeros((blk, LANES), BF16)
            q_msk, ks, v_aug = {}, {}, {}
            for rr in range(rper):
                r = r0 + rr
                first = 0 if whole else -1
                for i in range(first, nper):
                    if i < 0:
                        st = pl.multiple_of(jnp.maximum(n0 - 1, 0) * blk, blk)
                    else:
                        st = base + i * blk
                    ks[(rr, i)] = k_ref[0, r, pl.ds(st, blk), :]
                    v = v_ref[0, r, pl.ds(st, blk), :]
                    v_aug[(rr, i)] = (jnp.where(head0, v, one), jnp.where(head0, one, v))
                    if i >= 0:
                        q = q_ref[0, r, pl.ds(st, blk), :]
                        q_msk[(rr, i)] = (jnp.where(head0, q, zero), jnp.where(head0, zero, q))
            items = [(rr, i, e) for rr in range(rper) for i in range(nper) for e in range(2)]
            has_prev = lambda it: (it[0], it[1] - 1) in ks
            prev_of = lambda it: (it[0], it[1] - 1)
            k_win = {it[:2]: jnp.concatenate([ks[prev_of(it)], ks[it[:2]]], axis=0)
                     for it in items if has_prev(it)}
            bias_win = jnp.concatenate([bias_prev, bias_cur], axis=1)
            bias_first = bias_win if whole else jnp.where(n0 > 0, bias_win, jnp.concatenate(
                [jnp.full_like(bias_prev, NEG_BIG), bias_cur], axis=1))
            s, ms, ps, pv = {}, {}, {}, {}

            def scores(it):
                s[it] = nt(q_msk[it[:2]][it[2]], k_win[it[:2]] if has_prev(it) else ks[it[:2]])

            def softmax(it):
                if has_prev(it):
                    sb = s.pop(it) + (bias_first if it[1] == 0 else bias_win)
                else:
                    sb = s.pop(it) + bias_cur
                m = jnp.max(sb, axis=-1, keepdims=True)
                ms[it] = m
                ps[it] = _bf(jnp.exp2(sb - m))

            def weighted_values(it):
                if has_prev(it):
                    v_win = jnp.concatenate([v_aug[prev_of(it)][it[2]], v_aug[it[:2]][it[2]]], axis=0)
                else:
                    v_win = v_aug[it[:2]][it[2]]
                pv[it] = jnp.dot(ps.pop(it), v_win, preferred_element_type=F32)

            for it in items:
                scores(it)
            for it in items:
                softmax(it)
            for it in items:
                weighted_values(it)
            for rr in range(rper):
                for i in range(nper):
                    m_b = jnp.where(head0, ms[(rr, i, 0)], ms[(rr, i, 1)])
                    num_b = jnp.where(head0, pv[(rr, i, 0)], pv[(rr, i, 1)])
                    l_b = pltpu.roll(jnp.where(head0, pv[(rr, i, 1)], pv[(rr, i, 0)]), HEAD_DIM, axis=1)
                    if dil == 1:
                        rows = pl.ds(base + i * blk, blk)
                    else:
                        rows = pl.ds((n0 + i) * (blk * dil) + r0 + rr, blk, stride=dil)
                    if g == order[0]:
                        num_ref[rows, :] = num_b
                        m_ref[rows, :] = m_b
                        den_ref[rows, :] = l_b
                        continue
                    m_old = m_ref[rows, :]
                    m_new = jnp.maximum(m_old, m_b)
                    w_old = jnp.exp2(m_old - m_new)
                    w_b = jnp.exp2(m_b - m_new)
                    num = num_ref[rows, :] * w_old + num_b * w_b
                    den = den_ref[rows, :] * w_old + l_b * w_b
                    if g == order[-1]:
                        o_ref[0, rows, :] = _bf(num / den)
                    else:
                        num_ref[rows, :] = num
                        den_ref[rows, :] = den
                        m_ref[rows, :] = m_new
            return carry

        lax.fori_loop(0, dil * nb // (nper * rper), body, 0, unroll=2)


def _attn(qkv, seq):
    bsz = qkv[0].shape[0]
    in_specs = []
    for _ in range(3):
        for _, dil in ATTN_GROUPS:
            in_specs.append(pl.BlockSpec((1, dil, seq // dil, LANES), lambda b, sp: (b, 0, 0, sp)))
    return pl.pallas_call(
        _attn_kernel,
        grid=(bsz, GROUP_WIDTH // LANES),
        in_specs=in_specs,
        out_specs=pl.BlockSpec((1, seq, LANES), lambda b, sp: (b, 0, sp)),
        out_shape=jax.ShapeDtypeStruct((bsz, seq, GROUP_WIDTH), BF16),
        scratch_shapes=[pltpu.VMEM((seq, LANES), F32)] * 3,
        compiler_params=pltpu.CompilerParams(
            dimension_semantics=("arbitrary", "arbitrary"), vmem_limit_bytes=VMEM_LIMIT),
        name="attn",
    )(*qkv)


def _merge_kernel(x_ref, shift_ref, scale_ref, gatem_ref, g_ref, wg_ref, y_ref, bonus_ref, gate_ref,
                  lnw_ref, lnb_ref, wa_ref, o_ref, wb_ref, wout_ref, gpost_ref, out_ref):
    inv_n = 1.0 / HEAD_DIM
    subs = range(x_ref.shape[1] // SUB_ROWS)
    rows = [slice(sb * SUB_ROWS, (sb + 1) * SUB_ROWS) for sb in subs]
    yb = [jnp.dot(o_ref[0, rows[sb], :], wb_ref[...], preferred_element_type=F32) for sb in subs]
    y = [y_ref[0, rows[sb], :] for sb in subs]
    mean = [_head_sum(y[sb]) * inv_n for sb in subs]
    h = [_bf(_norm_modulate(x_ref[0, rows[sb], :], g_ref[...], shift_ref[0], scale_ref[0])) for sb in subs]
    pg = [jnp.dot(h[sb], wg_ref[...], preferred_element_type=F32) for sb in subs]
    yc = [y[sb] - mean[sb] for sb in subs]
    var = [_head_sum(yc[sb] * yc[sb]) * inv_n for sb in subs]
    pre = [_bf((yc[sb] * lax.rsqrt(var[sb] + GN_EPS) * lnw_ref[...] + lnb_ref[...]
                + bonus_ref[0, rows[sb], :]) * gate_ref[0, rows[sb], :]) for sb in subs]
    ya = [jnp.dot(pre[sb], wa_ref[...], preferred_element_type=F32) for sb in subs]
    merged = [_bf(_sigmoid(pg[sb][:, :D_MODEL]) * ya[sb] + _sigmoid(pg[sb][:, D_MODEL:]) * yb[sb])
              for sb in subs]
    zz = [jnp.dot(merged[sb], wout_ref[...], preferred_element_type=F32) for sb in subs]
    for sb in subs:
        out_ref[0, rows[sb], :] = x_ref[0, rows[sb], :] + gatem_ref[0] * _rms(zz[sb], gpost_ref[...])


def _merge(x, mod3, g_pre, w_g, y, bonus, gate, lnx_w, lnx_b, w_a, o_att, w_b, w_out, g_post, tm):
    bsz, seq, d = x.shape
    rw = RWKV_WIDTH
    const = lambda shape: pl.BlockSpec(shape, lambda b, i: (0, 0), pipeline_mode=pl.Buffered(1))
    row = lambda n: pl.BlockSpec((1, tm, n), lambda b, i: (b, i, 0))
    modv = lambda j: pl.BlockSpec((1, 1, d), lambda b, i: (b, 0, j))
    return pl.pallas_call(
        _merge_kernel,
        grid=(bsz, seq // tm),
        in_specs=[row(d), modv(0), modv(1), modv(2), const((1, d)),
                  pl.BlockSpec((pl.Element(d), pl.Element(2 * d)),
                               lambda b, i: (0, SHIFT_WIDTH + 3 * ATTN_WIDTH),
                               pipeline_mode=pl.Buffered(1)),
                  row(rw), row(rw), row(rw), const((1, rw)), const((1, rw)), const((rw, d)),
                  row(GROUP_WIDTH), const((GROUP_WIDTH, d)), const((d, d)), const((1, d))],
        out_specs=row(d),
        out_shape=jax.ShapeDtypeStruct((bsz, seq, d), F32),
        compiler_params=pltpu.CompilerParams(
            dimension_semantics=("arbitrary", "arbitrary"), vmem_limit_bytes=VMEM_LIMIT),
        name="merge",
    )(x, mod3, mod3, mod3, g_pre, w_g, y, bonus, gate, lnx_w, lnx_b, w_a, o_att, w_b, w_out, g_post)


def _ffn_kernel(x_ref, shift_ref, scale_ref, gatef_ref, g_ref, win_ref, wout_ref, gpost_ref, out_ref):
    dff = wout_ref.shape[0]
    subs = range(x_ref.shape[1] // SUB_ROWS)
    rows = [slice(sb * SUB_ROWS, (sb + 1) * SUB_ROWS) for sb in subs]
    h = [_bf(_norm_modulate(x_ref[0, rows[sb], :], g_ref[...], shift_ref[0], scale_ref[0])) for sb in subs]
    u = [jnp.dot(h[sb], win_ref[:, :dff], preferred_element_type=F32) for sb in subs]
    gt = [jnp.dot(h[sb], win_ref[:, dff:], preferred_element_type=F32) for sb in subs]
    act = [_bf(u[sb] * _sigmoid(u[sb]) * gt[sb]) for sb in subs]
    f = [jnp.dot(act[sb], wout_ref[...], preferred_element_type=F32) for sb in subs]
    for sb in subs:
        out_ref[0, rows[sb], :] = x_ref[0, rows[sb], :] + gatef_ref[0] * _rms(f[sb], gpost_ref[...])


def _ffn(x, mod3, g_pre, w_in, w_out, g_post, tm):
    bsz, seq, d = x.shape
    dff = w_out.shape[0]
    const = lambda shape: pl.BlockSpec(shape, lambda b, i: (0, 0), pipeline_mode=pl.Buffered(1))
    row = pl.BlockSpec((1, tm, d), lambda b, i: (b, i, 0))
    modv = lambda j: pl.BlockSpec((1, 1, d), lambda b, i: (b, 0, j))
    return pl.pallas_call(
        _ffn_kernel,
        grid=(bsz, seq // tm),
        in_specs=[row, modv(3), modv(4), modv(5), const((1, d)), const((d, 2 * dff)),
                  const((dff, d)), const((1, d))],
        out_specs=row,
        out_shape=jax.ShapeDtypeStruct((bsz, seq, d), F32),
        compiler_params=pltpu.CompilerParams(
            dimension_semantics=("arbitrary", "arbitrary"), vmem_limit_bytes=VMEM_LIMIT),
        name="ffn",
    )(x, mod3, mod3, mod3, g_pre, w_in, w_out, g_post)


def _rope_tables(seq):
    half = HEAD_DIM // 2
    inv_freq = ROPE_THETA ** (-jnp.arange(half, dtype=F32) / half)
    ang = jnp.arange(seq, dtype=F32)[:, None] * inv_freq[None, :]
    cos, sin = jnp.cos(ang), jnp.sin(ang)
    reps = LANES // HEAD_DIM
    cos_t = jnp.tile(jnp.concatenate([cos, cos], axis=-1), (1, reps))
    sin_t = jnp.tile(jnp.concatenate([-sin, sin], axis=-1), (1, reps))
    return cos_t, sin_t


def _layer(x, c, w_ada, b_ada, g_pre_mix, g_post_mix, w_in, mu_shift, w0, w2, a0, a2, g2,
           k_k, k_a, r_k, lnx_w, lnx_b, w_a, w_b, w_out, g_pre_ffn, g_post_ffn, w_ffn_in, w_ffn_out):
    bsz, seq, d = x.shape
    rw = RWKV_WIDTH
    row1 = lambda t: t.reshape(1, -1)
    mod3 = _mod(c, w_ada, b_ada).reshape(bsz, 1, 6 * d)

    w_rwkv = w_qkv = w_gates = _bf(w_in)
    w_lora = jnp.zeros((LORA_WIDTH, 3 * rw), F32)
    w_lora = w_lora.at[:DECAY_LORA, :rw].set(w2)
    w_lora = w_lora.at[DECAY_LORA:DECAY_LORA + ICLR_LORA, rw:2 * rw].set(a2)
    w_lora = _bf(w_lora.at[DECAY_LORA + ICLR_LORA:, 2 * rw:].set(g2))

    prep = _rwkv_prep(x, mod3, row1(g_pre_mix), w_rwkv, row1(mu_shift), w_lora, row1(w0), row1(a0),
                      row1(k_k), row1(k_a), row1(r_k), tm=512)
    r, k, v, nkk, b, cum, bonus, gate = prep
    y = _rwkv_scan(r, k, v, nkk, b, cum, rows_per_step=8, chunks_per_step=1)

    cos_t, sin_t = _rope_tables(seq)
    qkv = _attn_proj(x, mod3, row1(g_pre_mix), w_qkv, cos_t, sin_t, tm=1024)
    o_att = _attn(qkv, seq)

    x1 = _merge(x, mod3, row1(g_pre_mix), w_gates, y, bonus, gate, row1(lnx_w), row1(lnx_b),
                _bf(w_a), o_att, _bf(w_b), _bf(w_out), row1(g_post_mix), tm=512)
    return _ffn(x1, mod3, row1(g_pre_ffn), _bf(w_ffn_in), _bf(w_ffn_out), row1(g_post_ffn), tm=512)


def kernel(x, c, w_ada, b_ada, g_pre_mix, g_post_mix, w_in, mu_shift, w0, w2, a0, a2, g2, k_k, k_a, r_k,
           lnx_w, lnx_b, w_a, w_b, w_out, g_pre_ffn, g_post_ffn, w_ffn_in, w_ffn_out):
    for i in range(w_in.shape[0]):
        x = _layer(x, c, w_ada[i], b_ada[i], g_pre_mix[i], g_post_mix[i], w_in[i], mu_shift[i], w0[i],
                   w2[i], a0[i], a2[i], g2[i], k_k[i], k_a[i], r_k[i], lnx_w[i], lnx_b[i], w_a[i], w_b[i],
                   w_out[i], g_pre_ffn[i], g_post_ffn[i], w_ffn_in[i], w_ffn_out[i])
    return x
```

```python
import functools
import math

import jax
import jax.numpy as jnp
from jax import lax
from jax.experimental import pallas as pl
from jax.experimental.pallas import tpu as pltpu

D_MODEL = 1024
HEAD_DIM = 64
RWKV_HEADS = 12
RWKV_WIDTH = RWKV_HEADS * HEAD_DIM
DECAY_LORA = 64
ICLR_LORA = 64
GATE_LORA = 128
LORA_WIDTH = DECAY_LORA + ICLR_LORA + GATE_LORA
ATTN_GROUPS = ((128, 1), (512, 4), (2048, 16))
HEADS_PER_GROUP = 4
GROUP_WIDTH = HEADS_PER_GROUP * HEAD_DIM
ATTN_WIDTH = len(ATTN_GROUPS) * GROUP_WIDTH
ATTN_BLOCK = 128
ROPE_THETA = 10000.0
SHIFT_WIDTH = 3 * RWKV_WIDTH + LORA_WIDTH
NORM_EPS = 1e-6
GN_EPS = 64e-5

CHUNK = 64
INV_BASE = 16
LANES = 128
NEG_BIG = -1e30
LOG2E = math.log2(math.e)
SCAN_OPERAND_GROUPS = 4
ATTN_BLOCKS_PER_ITER = 8
SUB_ROWS = 256
ATTN_PROJ_SUB_ROWS = 512
VMEM_LIMIT = 56 * 1024 * 1024

F32 = jnp.float32
BF16 = jnp.bfloat16


def _bf(x):
    return x.astype(BF16)


def _div(x, n):
    assert n & (n - 1) == 0
    return jnp.right_shift(x, n.bit_length() - 1)


def _rem(x, n):
    assert n & (n - 1) == 0
    return jnp.bitwise_and(x, n - 1)


def _sigmoid(x):
    return 1.0 / (1.0 + jnp.exp2(x * (-LOG2E)))


def _split_hi_lo(x):
    hi = _bf(x)
    lo = _bf(x - hi.astype(F32))
    return hi, lo


def _mm_exact_lhs(m_bf, x):
    hi, lo = _split_hi_lo(x)
    return (jnp.dot(m_bf, hi, preferred_element_type=F32)
            + jnp.dot(m_bf, lo, preferred_element_type=F32))


def _head_sum(x):
    tile = 2 * LANES
    ri = _div(lax.broadcasted_iota(jnp.int32, (tile, tile), 0), HEAD_DIM)
    ci = _div(lax.broadcasted_iota(jnp.int32, (tile, tile), 1), HEAD_DIM)
    ones_bd = jnp.where(ri == ci, 1.0, 0.0).astype(BF16)
    xb = _bf(x)
    parts = [jnp.dot(xb[:, j * tile:(j + 1) * tile], ones_bd, preferred_element_type=F32)
             for j in range(x.shape[-1] // tile)]
    return jnp.concatenate(parts, axis=-1)


def _norm_modulate(x, g, shift, scale):
    y = x * lax.rsqrt(jnp.mean(x * x, axis=-1, keepdims=True) + NORM_EPS)
    return y * (g * (1.0 + scale)) + shift


def _rms(x, g):
    return x * lax.rsqrt(jnp.mean(x * x, axis=-1, keepdims=True) + NORM_EPS) * g


def _mod_kernel(c_ref, w_ref, b_ref, o_ref):
    c = c_ref[...]
    hi, lo = _split_hi_lo(c * _sigmoid(c))
    w = _bf(w_ref[...])
    o_ref[...] = (jnp.dot(hi, w, preferred_element_type=F32)
                  + jnp.dot(lo, w, preferred_element_type=F32) + b_ref[...])


def _mod(c, w_ada, b_ada):
    bsz, d = c.shape
    n = w_ada.shape[1]
    tn = d
    return pl.pallas_call(
        _mod_kernel,
        grid=(n // tn,),
        in_specs=[pl.BlockSpec((bsz, d), lambda j: (0, 0)),
                  pl.BlockSpec((d, tn), lambda j: (0, j)),
                  pl.BlockSpec((1, tn), lambda j: (0, j))],
        out_specs=pl.BlockSpec((bsz, tn), lambda j: (0, j)),
        out_shape=jax.ShapeDtypeStruct((bsz, n), F32),
        name="mod",
    )(c, w_ada, b_ada.reshape(1, n))


def _rwkv_prep_kernel(x_ref, shift_ref, scale_ref, g_ref, w_ref, mu_ref, wl_ref, w0_ref, a0_ref,
                      kk_ref, ka_ref, rk_ref,
                      r_o, k_o, v_o, nkk_o, b_o, cum_o, bonus_o, gate_o,
                      carry_ref):
    i = pl.program_id(1)
    rw = RWKV_WIDTH
    sub = SUB_ROWS

    @pl.when(i == 0)
    def _():
        carry_ref[...] = jnp.zeros_like(carry_ref)

    ri = lax.broadcasted_iota(jnp.int32, (sub, sub), 0)
    ci = lax.broadcasted_iota(jnp.int32, (sub, sub), 1)
    same = _div(ri, CHUNK) == _div(ci, CHUNK)
    lower = jnp.where(same & (ci <= ri), 1.0, 0.0).astype(BF16)
    row0 = lax.broadcasted_iota(jnp.int32, (8, SHIFT_WIDTH), 0) == 0
    assert DECAY_LORA + ICLR_LORA == LANES
    lane = lax.broadcasted_iota(jnp.int32, (sub, LANES), 1)

    subs = range(x_ref.shape[1] // sub)
    rows = [slice(sb * sub, (sb + 1) * sub) for sb in subs]
    h = [_bf(_norm_modulate(x_ref[0, rows[sb], :], g_ref[...], shift_ref[0], scale_ref[0])) for sb in subs]
    p = [jnp.dot(h[sb], w_ref[...], preferred_element_type=F32) for sb in subs]

    prev_row = carry_ref[7:8, :]
    ps = []
    for sb in subs:
        rolled = pltpu.roll(p[sb], 1, axis=0)
        shifted = jnp.concatenate([jnp.where(row0, prev_row, rolled[:8]), rolled[8:]], axis=0)
        last8 = p[sb][sub - 8:, :]
        prev_row = last8[7:8, :]
        ps.append(p[sb] + mu_ref[...] * (shifted - p[sb]))
    carry_ref[...] = last8

    z = []
    for sb in subs:
        wa_in = ps[sb][:, 3 * rw:3 * rw + LANES]
        g_in = ps[sb][:, 3 * rw + LANES:]
        z.append(_bf(jnp.concatenate([jnp.where(lane < DECAY_LORA, jnp.tanh(wa_in), wa_in),
                                      _sigmoid(g_in)], axis=1)))
    lo = [jnp.dot(z[sb], wl_ref[...], preferred_element_type=F32) for sb in subs]

    pr = [ps[sb][:, :rw] for sb in subs]
    pk = [ps[sb][:, rw:2 * rw] for sb in subs]
    pv = [ps[sb][:, 2 * rw:3 * rw] for sb in subs]
    kkr = [pk[sb] * kk_ref[...] for sb in subs]
    kk_sq = [_head_sum(kkr[sb] * kkr[sb]) for sb in subs]
    a = [_sigmoid(a0_ref[...] + lo[sb][:, rw:2 * rw]) for sb in subs]
    one_minus_ka = 1.0 - ka_ref[...]
    k = [pk[sb] * (one_minus_ka + a[sb] * ka_ref[...]) for sb in subs]
    rk_sum = [_head_sum(pr[sb] * k[sb] * rk_ref[...]) for sb in subs]
    lw = [(-math.exp(-0.5) * LOG2E) * _sigmoid(w0_ref[...] + lo[sb][:, :rw]) for sb in subs]
    cum = [_mm_exact_lhs(lower, lw[sb]) for sb in subs]

    for sb in subs:
        kkn = kkr[sb] * lax.rsqrt(jnp.maximum(kk_sq[sb], 1e-24))
        gate_o[0, rows[sb], :] = _bf(lo[sb][:, 2 * rw:])
        bonus_o[0, rows[sb], :] = _bf(rk_sum[sb] * pv[sb])
        cum_o[0, rows[sb], :] = cum[sb]
        r_o[0, rows[sb], :] = _bf(pr[sb])
        k_o[0, rows[sb], :] = _bf(k[sb])
        v_o[0, rows[sb], :] = _bf(pv[sb])
        nkk_o[0, rows[sb], :] = _bf(-kkn)
        b_o[0, rows[sb], :] = _bf(kkn * a[sb])


def _rwkv_prep(x, mod3, g_pre, w_rwkv, mu, w_lora, w0, a0, k_k, k_a, r_k, tm):
    bsz, seq, d = x.shape
    rw = RWKV_WIDTH
    row_spec = pl.BlockSpec((1, tm, rw), lambda b, i: (b, i, 0))
    vec = lambda n: pl.BlockSpec((1, n), lambda b, i: (0, 0))
    big = jax.ShapeDtypeStruct((bsz, seq, rw), F32)
    half = jax.ShapeDtypeStruct((bsz, seq, rw), BF16)
    return pl.pallas_call(
        _rwkv_prep_kernel,
        grid=(bsz, seq // tm),
        in_specs=[pl.BlockSpec((1, tm, d), lambda b, i: (b, i, 0)),
                  pl.BlockSpec((1, 1, d), lambda b, i: (b, 0, 0)),
                  pl.BlockSpec((1, 1, d), lambda b, i: (b, 0, 1)),
                  vec(d),
                  pl.BlockSpec((d, SHIFT_WIDTH), lambda b, i: (0, 0), pipeline_mode=pl.Buffered(1)),
                  vec(SHIFT_WIDTH),
                  pl.BlockSpec((LORA_WIDTH, 3 * rw), lambda b, i: (0, 0), pipeline_mode=pl.Buffered(1)),
                  vec(rw), vec(rw), vec(rw), vec(rw), vec(rw)],
        out_specs=[row_spec] * 8,
        out_shape=[half] * 5 + [big] + [half] * 2,
        scratch_shapes=[pltpu.VMEM((8, SHIFT_WIDTH), F32)],
        compiler_params=pltpu.CompilerParams(
            dimension_semantics=("arbitrary", "arbitrary"), vmem_limit_bytes=VMEM_LIMIT),
        name="rwkv_prep",
    )(x, mod3, mod3, g_pre, w_rwkv, mu, w_lora, w0, a0, k_k, k_a, r_k)


def _pair_blockdiag(x):
    first = lax.broadcasted_iota(jnp.int32, x.shape, 1) < HEAD_DIM
    zero = jnp.zeros_like(x)
    return jnp.concatenate([jnp.where(first, x, zero), jnp.where(first, zero, x)], axis=0)


def _pmm(a, b):
    return jnp.dot(_bf(a), _pair_blockdiag(_bf(b)), preferred_element_type=F32)


def _pmm_nt(a, b):
    return lax.dot_general(_bf(a), _pair_blockdiag(_bf(b)), (((1,), (1,)), ((), ())),
                           preferred_element_type=F32)


def _pmm_tn(a, b):
    full = lax.dot_general(_bf(a), _bf(b), (((0,), (0,)), ((), ())), preferred_element_type=F32)
    first = lax.broadcasted_iota(jnp.int32, (HEAD_DIM, LANES), 1) < HEAD_DIM
    return jnp.where(first, full[:HEAD_DIM], full[HEAD_DIM:])


def _unit_lower_inverse(a_list, masks):
    eye, diag_mask, merge_masks = masks
    ps = [jnp.where(diag_mask, a, 0.0) for a in a_list]
    ts = [eye + p for p in ps]
    ps = [_pmm(p, p) for p in ps]
    m = 2
    while 2 * m < INV_BASE:
        both = [_pmm(jnp.concatenate([p, t], axis=0), p) for p, t in zip(ps, ts)]
        ps = [x[:CHUNK] for x in both]
        ts = [t + x[CHUNK:] for t, x in zip(ts, both)]
        m *= 2
    ts = [t + _pmm(t, p) for t, p in zip(ts, ps)]
    for mk in merge_masks:
        ws = [_pmm(jnp.where(mk, a, 0.0), t) for a, t in zip(a_list, ts)]
        ts = [t + _pmm(t, w) for t, w in zip(ts, ws)]
    return ts


def _rwkv_scan_kernel(r_ref, k_ref, v_ref, nkk_ref, b_ref, cum_ref, y_ref, st_ref):
    @pl.when(pl.program_id(1) == 0)
    def _():
        st_ref[...] = jnp.zeros_like(st_ref)

    ri = lax.broadcasted_iota(jnp.int32, (CHUNK, LANES), 0)
    ci = _rem(lax.broadcasted_iota(jnp.int32, (CHUNK, LANES), 1), HEAD_DIM)
    strict = ri > ci
    incl = ri >= ci
    ri2 = lax.broadcasted_iota(jnp.int32, (2 * CHUNK, LANES), 0)
    ci2 = _rem(lax.broadcasted_iota(jnp.int32, (2 * CHUNK, LANES), 1), HEAD_DIM)
    rloc = _rem(ri2, CHUNK)
    stacked = (rloc > ci2) | ((rloc == ci2) & (ri2 >= CHUNK))
    eye = jnp.where(ri == ci, 1.0, 0.0)
    diag_mask = _div(ri, INV_BASE) == _div(ci, INV_BASE)
    merge_masks = []
    bsize = INV_BASE
    while bsize < CHUNK:
        merge_masks.append((_div(ri, 2 * bsize) == _div(ci, 2 * bsize))
                           & (_div(ri, bsize) != _div(ci, bsize)))
        bsize *= 2
    masks = (eye, diag_mask, merge_masks)

    n_rows = r_ref.shape[0]
    n_chunks = r_ref.shape[1] // CHUNK
    n_pairs = RWKV_WIDTH // LANES
    seqs = [(b, j) for b in range(n_rows) for j in range(n_pairs)]
    items = [(c, s) for c in range(n_chunks) for s in seqs]

    def tile(ref, c, s):
        b, j = s
        return ref[b, c * CHUNK:(c + 1) * CHUNK, j * LANES:(j + 1) * LANES]

    first_row = ri == 0
    nt = lambda a, b: lax.dot_general(a, b, (((1,), (1,)), ((), ())), preferred_element_type=F32)
    qs, bkg, gcs, ab, akv = {}, {}, {}, {}, {}
    n_groups = min(SCAN_OPERAND_GROUPS, len(items))
    for grp in range(n_groups):
        group = items[grp::n_groups]
        bk, kd = {}, {}
        for it in group:
            cum = tile(cum_ref, *it)
            cum_last = cum[CHUNK - 1:CHUNK, :]
            e_cum = jnp.exp2(cum)
            e_prev = jnp.exp2(jnp.where(first_row, 0.0, pltpu.roll(cum, 1, axis=0)))
            e_inv = jnp.exp2(-cum)
            e_rest = jnp.exp2(cum_last - cum)
            gcs[it] = jnp.exp2(cum_last)
            r, k = tile(r_ref, *it).astype(F32), tile(k_ref, *it).astype(F32)
            nkk, b = tile(nkk_ref, *it).astype(F32), tile(b_ref, *it).astype(F32)
            qs[it] = _bf(jnp.concatenate([nkk * e_prev, r * e_cum], axis=0))
            bk[it], kd[it] = _bf(b * e_inv), _bf(k * e_inv)
            bkg[it] = _bf(jnp.concatenate([b * e_rest, k * e_rest], axis=0))
        abk = {it: nt(qs[it], jnp.concatenate([_pair_blockdiag(bk[it]), _pair_blockdiag(kd[it])], axis=0))
               for it in group}
        for it in group:
            ab[it] = abk[it][:, :LANES]
        ak = {it: jnp.where(stacked, abk[it][:, LANES:], 0.0) for it in group}
        for it in group:
            akv[it] = _pmm(ak[it], tile(v_ref, *it))

    t_list = _unit_lower_inverse([jnp.where(strict, ab[it][:CHUNK], 0.0) for it in items], masks)
    t = dict(zip(items, t_list))

    state = {s: st_ref[s[0], s[1]] for s in seqs}
    for c in range(n_chunks):
        xs = {s: _pmm_nt(qs[(c, s)], state[s]) for s in seqs}
        u = {s: _pmm(t[(c, s)], xs[s][:CHUNK] + akv[(c, s)][:CHUNK]) for s in seqs}
        for s in seqs:
            b, j = s
            y = xs[s][CHUNK:] + akv[(c, s)][CHUNK:] + _pmm(
                jnp.where(incl, ab[(c, s)][CHUNK:], 0.0), u[s])
            y_ref[b, c * CHUNK:(c + 1) * CHUNK, j * LANES:(j + 1) * LANES] = _bf(y)
        new_state = {}
        for s in seqs:
            uv = jnp.concatenate([_bf(u[s]), tile(v_ref, c, s)], axis=0)
            new_state[s] = state[s] * gcs[(c, s)] + _pmm_tn(uv, bkg[(c, s)])
        state = new_state
    for s in seqs:
        st_ref[s[0], s[1]] = state[s]


def _rwkv_scan(r, k, v, nkk, b, cum, rows_per_step, chunks_per_step):
    bsz, seq, rw = r.shape
    ts = chunks_per_step * CHUNK
    row_spec = pl.BlockSpec((rows_per_step, ts, rw), lambda b, c: (b, c, 0))
    return pl.pallas_call(
        _rwkv_scan_kernel,
        grid=(bsz // rows_per_step, seq // ts),
        in_specs=[row_spec] * 6,
        out_specs=row_spec,
        out_shape=jax.ShapeDtypeStruct((bsz, seq, rw), BF16),
        scratch_shapes=[pltpu.VMEM((rows_per_step, rw // LANES, HEAD_DIM, LANES), F32)],
        compiler_params=pltpu.CompilerParams(dimension_semantics=("arbitrary", "arbitrary")),
        name="rwkv_scan",
    )(r, k, v, nkk, b, cum)


def _attn_proj_kernel(x_ref, shift_ref, scale_ref, g_ref, w_ref, cos_ref, sin_ref, *rest):
    outs, buf = rest[:-1], rest[-1]
    SUB_ROWS = ATTN_PROJ_SUB_ROWS
    aw = ATTN_WIDTH
    reps = aw // LANES
    lane = lax.broadcasted_iota(jnp.int32, (SUB_ROWS, aw), 1)
    first_half = _rem(lane, HEAD_DIM) < (HEAD_DIM // 2)
    tiles_per_group = GROUP_WIDTH // LANES

    subs = range(x_ref.shape[1] // SUB_ROWS)
    ps = []
    for sb in subs:
        rows = slice(sb * SUB_ROWS, (sb + 1) * SUB_ROWS)
        h = _norm_modulate(x_ref[0, rows, :], g_ref[...], shift_ref[0], scale_ref[0])
        ps.append(jnp.dot(_bf(h), w_ref[...], preferred_element_type=F32))
    for sb in subs:
        rows = slice(sb * SUB_ROWS, (sb + 1) * SUB_ROWS)
        p = ps[sb]
        cos_t, sin_t = cos_ref[rows, :], sin_ref[rows, :]

        def rope(t, scale):
            cos = jnp.concatenate([cos_t * scale] * reps, axis=-1)
            sin = jnp.concatenate([sin_t * scale] * reps, axis=-1)
            partner = jnp.where(first_half, pltpu.roll(t, aw - HEAD_DIM // 2, axis=1),
                                pltpu.roll(t, HEAD_DIM // 2, axis=1))
            return t * cos + partner * sin

        roped = (rope(p[:, :aw], HEAD_DIM ** -0.5 * LOG2E), rope(p[:, aw:2 * aw], 1.0), p[:, 2 * aw:])
        n = 0
        for which in range(3):
            for g, (_, dil) in enumerate(ATTN_GROUPS):
                o_ref = outs[n]
                n += 1
                out_rows = slice(sb * SUB_ROWS // dil, (sb + 1) * SUB_ROWS // dil)
                for ct in range(tiles_per_group):
                    c0 = g * GROUP_WIDTH + ct * LANES
                    val = roped[which][:, c0:c0 + LANES]
                    if dil == 1:
                        o_ref[0, 0, out_rows, ct * LANES:(ct + 1) * LANES] = _bf(val)
                        continue
                    slot = (which * len(ATTN_GROUPS) + g) * tiles_per_group + ct
                    buf[sb, slot] = val
                    for r in range(dil):
                        o_ref[0, r, out_rows, ct * LANES:(ct + 1) * LANES] = _bf(
                            buf[sb, slot, pl.ds(r, SUB_ROWS // dil, stride=dil), :])


def _attn_proj(x, mod3, g_pre, w_qkv, cos_t, sin_t, tm):
    bsz, seq, d = x.shape
    out_specs, out_shape = [], []
    for _ in range(3):
        for _, dil in ATTN_GROUPS:
            out_specs.append(pl.BlockSpec((1, dil, tm // dil, GROUP_WIDTH), lambda b, i: (b, 0, i, 0)))
            out_shape.append(jax.ShapeDtypeStruct((bsz, dil, seq // dil, GROUP_WIDTH), BF16))
    return pl.pallas_call(
        _attn_proj_kernel,
        grid=(bsz, seq // tm),
        in_specs=[pl.BlockSpec((1, tm, d), lambda b, i: (b, i, 0)),
                  pl.BlockSpec((1, 1, d), lambda b, i: (b, 0, 0)),
                  pl.BlockSpec((1, 1, d), lambda b, i: (b, 0, 1)),
                  pl.BlockSpec((1, d), lambda b, i: (0, 0)),
                  pl.BlockSpec((pl.Element(d), pl.Element(3 * ATTN_WIDTH)), lambda b, i: (0, SHIFT_WIDTH),
                               pipeline_mode=pl.Buffered(1)),
                  pl.BlockSpec((tm, LANES), lambda b, i: (i, 0)),
                  pl.BlockSpec((tm, LANES), lambda b, i: (i, 0))],
        out_specs=out_specs,
        out_shape=out_shape,
        scratch_shapes=[pltpu.VMEM((tm // ATTN_PROJ_SUB_ROWS, 3 * ATTN_WIDTH // LANES,
                                    ATTN_PROJ_SUB_ROWS, LANES), F32)],
        compiler_params=pltpu.CompilerParams(
            dimension_semantics=("arbitrary", "arbitrary"), vmem_limit_bytes=VMEM_LIMIT),
        name="attn_proj",
    )(x, mod3, mod3, g_pre, w_qkv, cos_t, sin_t)


def _attn_kernel(*refs):
    qkv = refs[:9]
    o_ref = refs[9]
    num_ref, m_ref, den_ref = refs[10:]
    blk = ATTN_BLOCK
    qi = lax.broadcasted_iota(jnp.int32, (blk, blk), 0)
    kj = lax.broadcasted_iota(jnp.int32, (blk, blk), 1)
    bias_cur = jnp.where(kj <= qi, 0.0, NEG_BIG)
    bias_prev = jnp.where(kj >= qi, 0.0, NEG_BIG)
    lane = lax.broadcasted_iota(jnp.int32, (blk, LANES), 1)
    head0 = lane < HEAD_DIM

    order = sorted(range(len(ATTN_GROUPS)), key=lambda g: -ATTN_GROUPS[g][1])
    assert ATTN_GROUPS[order[-1]][1] == 1
    for g in order:
        dil = ATTN_GROUPS[g][1]
        q_ref, k_ref, v_ref = qkv[g], qkv[3 + g], qkv[6 + g]
        nb = q_ref.shape[2] // blk

        nper = min(nb, ATTN_BLOCKS_PER_ITER)
        rper = ATTN_BLOCKS_PER_ITER // nper
        whole = nper == nb

        def body(idx, carry, q_ref=q_ref, k_ref=k_ref, v_ref=v_ref, dil=dil, nb=nb, g=g,
                 nper=nper, rper=rper, whole=whole):
            if whole:
                r0, n0 = idx * rper, 0
                base = 0
            else:
                r0 = _div(idx, nb // nper)
                n0 = _rem(idx, nb // nper) * nper
                base = pl.multiple_of(n0 * blk, blk)
            nt = lambda a, b: lax.dot_general(a, b, (((1,), (1,)), ((), ())), preferred_element_type=F32)
            one = jnp.ones((blk, LANES), BF16)
            zero = jnp.zeros((blk, LANES), BF16)
            q_msk, ks, v_aug = {}, {}, {}
            for rr in range(rper):
                r = r0 + rr
                first = 0 if whole else -1
                for i in range(first, nper):
                    if i < 0:
                        st = pl.multiple_of(jnp.maximum(n0 - 1, 0) * blk, blk)
                    else:
                        st = base + i * blk
                    ks[(rr, i)] = k_ref[0, r, pl.ds(st, blk), :]
                    v = v_ref[0, r, pl.ds(st, blk), :]
                    v_aug[(rr, i)] = (jnp.where(head0, v, one), jnp.where(head0, one, v))
                    if i >= 0:
                        q = q_ref[0, r, pl.ds(st, blk), :]
                        q_msk[(rr, i)] = (jnp.where(head0, q, zero), jnp.where(head0, zero, q))
            items = [(rr, i, e) for rr in range(rper) for i in range(nper) for e in range(2)]
            has_prev = lambda it: (it[0], it[1] - 1) in ks
            prev_of = lambda it: (it[0], it[1] - 1)
            k_win = {it[:2]: jnp.concatenate([ks[prev_of(it)], ks[it[:2]]], axis=0)
                     for it in items if has_prev(it)}
            bias_win = jnp.concatenate([bias_prev, bias_cur], axis=1)
            bias_first = bias_win if whole else jnp.where(n0 > 0, bias_win, jnp.concatenate(
                [jnp.full_like(bias_prev, NEG_BIG), bias_cur], axis=1))
            s, ms, ps, pv = {}, {}, {}, {}

            def scores(it):
                s[it] = nt(q_msk[it[:2]][it[2]], k_win[it[:2]] if has_prev(it) else ks[it[:2]])

            def softmax(it):
                if has_prev(it):
                    sb = s.pop(it) + (bias_first if it[1] == 0 else bias_win)
                else:
                    sb = s.pop(it) + bias_cur
                m = jnp.max(sb, axis=-1, keepdims=True)
                ms[it] = m
                ps[it] = _bf(jnp.exp2(sb - m))

            def weighted_values(it):
                if has_prev(it):
                    v_win = jnp.concatenate([v_aug[prev_of(it)][it[2]], v_aug[it[:2]][it[2]]], axis=0)
                else:
                    v_win = v_aug[it[:2]][it[2]]
                pv[it] = jnp.dot(ps.pop(it), v_win, preferred_element_type=F32)

            for it in items:
                scores(it)
            for it in items:
                softmax(it)
            for it in items:
                weighted_values(it)
            for rr in range(rper):
                for i in range(nper):
                    m_b = jnp.where(head0, ms[(rr, i, 0)], ms[(rr, i, 1)])
                    num_b = jnp.where(head0, pv[(rr, i, 0)], pv[(rr, i, 1)])
                    l_b = pltpu.roll(jnp.where(head0, pv[(rr, i, 1)], pv[(rr, i, 0)]), HEAD_DIM, axis=1)
                    if dil == 1:
                        rows = pl.ds(base + i * blk, blk)
                    else:
                        rows = pl.ds((n0 + i) * (blk * dil) + r0 + rr, blk, stride=dil)
                    if g == order[0]:
                        num_ref[rows, :] = num_b
                        m_ref[rows, :] = m_b
                        den_ref[rows, :] = l_b
                        continue
                    m_old = m_ref[rows, :]
                    m_new = jnp.maximum(m_old, m_b)
                    w_old = jnp.exp2(m_old - m_new)
                    w_b = jnp.exp2(m_b - m_new)
                    num = num_ref[rows, :] * w_old + num_b * w_b
                    den = den_ref[rows, :] * w_old + l_b * w_b
                    if g == order[-1]:
                        o_ref[0, rows, :] = _bf(num / den)
                    else:
                        num_ref[rows, :] = num
                        den_ref[rows, :] = den
                        m_ref[rows, :] = m_new
            return carry

        lax.fori_loop(0, dil * nb // (nper * rper), body, 0, unroll=2)


def _attn(qkv, seq):
    bsz = qkv[0].shape[0]
    in_specs = []
    for _ in range(3):
        for _, dil in ATTN_GROUPS:
            in_specs.append(pl.BlockSpec((1, dil, seq // dil, LANES), lambda b, sp: (b, 0, 0, sp)))
    return pl.pallas_call(
        _attn_kernel,
        grid=(bsz, GROUP_WIDTH // LANES),
        in_specs=in_specs,
        out_specs=pl.BlockSpec((1, seq, LANES), lambda b, sp: (b, 0, sp)),
        out_shape=jax.ShapeDtypeStruct((bsz, seq, GROUP_WIDTH), BF16),
        scratch_shapes=[pltpu.VMEM((seq, LANES), F32)] * 3,
        compiler_params=pltpu.CompilerParams(
            dimension_semantics=("arbitrary", "arbitrary"), vmem_limit_bytes=VMEM_LIMIT),
        name="attn",
    )(*qkv)


def _merge_kernel(x_ref, shift_ref, scale_ref, gatem_ref, g_ref, wg_ref, y_ref, bonus_ref, gate_ref,
                  lnw_ref, lnb_ref, wa_ref, o_ref, wb_ref, wout_ref, gpost_ref, out_ref):
    inv_n = 1.0 / HEAD_DIM
    subs = range(x_ref.shape[1] // SUB_ROWS)
    rows = [slice(sb * SUB_ROWS, (sb + 1) * SUB_ROWS) for sb in subs]
    yb = [jnp.dot(o_ref[0, rows[sb], :], wb_ref[...], preferred_element_type=F32) for sb in subs]
    y = [y_ref[0, rows[sb], :].astype(F32) for sb in subs]
    mean = [_head_sum(y[sb]) * inv_n for sb in subs]
    h = [_bf(_norm_modulate(x_ref[0, rows[sb], :], g_ref[...], shift_ref[0], scale_ref[0])) for sb in subs]
    pg = [jnp.dot(h[sb], wg_ref[...], preferred_element_type=F32) for sb in subs]
    yc = [y[sb] - mean[sb] for sb in subs]
    var = [_head_sum(yc[sb] * yc[sb]) * inv_n for sb in subs]
    pre = [_bf((yc[sb] * lax.rsqrt(var[sb] + GN_EPS) * lnw_ref[...] + lnb_ref[...]
                + bonus_ref[0, rows[sb], :].astype(F32)) * gate_ref[0, rows[sb], :].astype(F32))
           for sb in subs]
    ya = [jnp.dot(pre[sb], wa_ref[...], preferred_element_type=F32) for sb in subs]
    merged = [_bf(_sigmoid(pg[sb][:, :D_MODEL]) * ya[sb] + _sigmoid(pg[sb][:, D_MODEL:]) * yb[sb])
              for sb in subs]
    zz = [jnp.dot(merged[sb], wout_ref[...], preferred_element_type=F32) for sb in subs]
    for sb in subs:
        out_ref[0, rows[sb], :] = x_ref[0, rows[sb], :] + gatem_ref[0] * _rms(zz[sb], gpost_ref[...])


def _merge(x, mod3, g_pre, w_g, y, bonus, gate, lnx_w, lnx_b, w_a, o_att, w_b, w_out, g_post, tm):
    bsz, seq, d = x.shape
    rw = RWKV_WIDTH
    const = lambda shape: pl.BlockSpec(shape, lambda b, i: (0, 0), pipeline_mode=pl.Buffered(1))
    row = lambda n: pl.BlockSpec((1, tm, n), lambda b, i: (b, i, 0))
    modv = lambda j: pl.BlockSpec((1, 1, d), lambda b, i: (b, 0, j))
    return pl.pallas_call(
        _merge_kernel,
        grid=(bsz, seq // tm),
        in_specs=[row(d), modv(0), modv(1), modv(2), const((1, d)),
                  pl.BlockSpec((pl.Element(d), pl.Element(2 * d)),
                               lambda b, i: (0, SHIFT_WIDTH + 3 * ATTN_WIDTH),
                               pipeline_mode=pl.Buffered(1)),
                  row(rw), row(rw), row(rw), const((1, rw)), const((1, rw)), const((rw, d)),
                  row(GROUP_WIDTH), const((GROUP_WIDTH, d)), const((d, d)), const((1, d))],
        out_specs=row(d),
        out_shape=jax.ShapeDtypeStruct((bsz, seq, d), F32),
        compiler_params=pltpu.CompilerParams(
            dimension_semantics=("arbitrary", "arbitrary"), vmem_limit_bytes=VMEM_LIMIT),
        name="merge",
    )(x, mod3, mod3, mod3, g_pre, w_g, y, bonus, gate, lnx_w, lnx_b, w_a, o_att, w_b, w_out, g_post)


def _ffn_kernel(x_ref, shift_ref, scale_ref, gatef_ref, g_ref, win_ref, wout_ref, gpost_ref, out_ref):
    dff = wout_ref.shape[0]
    subs = range(x_ref.shape[1] // SUB_ROWS)
    rows = [slice(sb * SUB_ROWS, (sb + 1) * SUB_ROWS) for sb in subs]
    h = [_bf(_norm_modulate(x_ref[0, rows[sb], :], g_ref[...], shift_ref[0], scale_ref[0])) for sb in subs]
    u = [jnp.dot(h[sb], win_ref[:, :dff], preferred_element_type=F32) for sb in subs]
    gt = [jnp.dot(h[sb], win_ref[:, dff:], preferred_element_type=F32) for sb in subs]
    act = [_bf(u[sb] * _sigmoid(u[sb]) * gt[sb]) for sb in subs]
    f = [jnp.dot(act[sb], wout_ref[...], preferred_element_type=F32) for sb in subs]
    for sb in subs:
        out_ref[0, rows[sb], :] = x_ref[0, rows[sb], :] + gatef_ref[0] * _rms(f[sb], gpost_ref[...])


def _ffn(x, mod3, g_pre, w_in, w_out, g_post, tm):
    bsz, seq, d = x.shape
    dff = w_out.shape[0]
    const = lambda shape: pl.BlockSpec(shape, lambda b, i: (0, 0), pipeline_mode=pl.Buffered(1))
    row = pl.BlockSpec((1, tm, d), lambda b, i: (b, i, 0))
    modv = lambda j: pl.BlockSpec((1, 1, d), lambda b, i: (b, 0, j))
    return pl.pallas_call(
        _ffn_kernel,
        grid=(bsz, seq // tm),
        in_specs=[row, modv(3), modv(4), modv(5), const((1, d)), const((d, 2 * dff)),
                  const((dff, d)), const((1, d))],
        out_specs=row,
        out_shape=jax.ShapeDtypeStruct((bsz, seq, d), F32),
        compiler_params=pltpu.CompilerParams(
            dimension_semantics=("arbitrary", "arbitrary"), vmem_limit_bytes=VMEM_LIMIT),
        name="ffn",
    )(x, mod3, mod3, mod3, g_pre, w_in, w_out, g_post)


def _rope_tables(seq):
    half = HEAD_DIM // 2
    inv_freq = ROPE_THETA ** (-jnp.arange(half, dtype=F32) / half)
    ang = jnp.arange(seq, dtype=F32)[:, None] * inv_freq[None, :]
    cos, sin = jnp.cos(ang), jnp.sin(ang)
    reps = LANES // HEAD_DIM
    cos_t = jnp.tile(jnp.concatenate([cos, cos], axis=-1), (1, reps))
    sin_t = jnp.tile(jnp.concatenate([-sin, sin], axis=-1), (1, reps))
    return cos_t, sin_t


def _layer(x, c, w_ada, b_ada, g_pre_mix, g_post_mix, w_in, mu_shift, w0, w2, a0, a2, g2,
           k_k, k_a, r_k, lnx_w, lnx_b, w_a, w_b, w_out, g_pre_ffn, g_post_ffn, w_ffn_in, w_ffn_out):
    bsz, seq, d = x.shape
    rw = RWKV_WIDTH
    row1 = lambda t: t.reshape(1, -1)
    mod3 = _mod(c, w_ada, b_ada).reshape(bsz, 1, 6 * d)

    w_rwkv = w_qkv = w_gates = _bf(w_in)
    w_lora = jnp.zeros((LORA_WIDTH, 3 * rw), F32)
    w_lora = w_lora.at[:DECAY_LORA, :rw].set(w2)
    w_lora = w_lora.at[DECAY_LORA:DECAY_LORA + ICLR_LORA, rw:2 * rw].set(a2)
    w_lora = _bf(w_lora.at[DECAY_LORA + ICLR_LORA:, 2 * rw:].set(g2))

    prep = _rwkv_prep(x, mod3, row1(g_pre_mix), w_rwkv, row1(mu_shift), w_lora, row1(w0), row1(a0),
                      row1(k_k), row1(k_a), row1(r_k), tm=512)
    r, k, v, nkk, b, cum, bonus, gate = prep
    y = _rwkv_scan(r, k, v, nkk, b, cum, rows_per_step=8, chunks_per_step=1)

    cos_t, sin_t = _rope_tables(seq)
    qkv = _attn_proj(x, mod3, row1(g_pre_mix), w_qkv, cos_t, sin_t, tm=1024)
    o_att = _attn(qkv, seq)

    x1 = _merge(x, mod3, row1(g_pre_mix), w_gates, y, bonus, gate, row1(lnx_w), row1(lnx_b),
                _bf(w_a), o_att, _bf(w_b), _bf(w_out), row1(g_post_mix), tm=512)
    return _ffn(x1, mod3, row1(g_pre_ffn), _bf(w_ffn_in), _bf(w_ffn_out), row1(g_post_ffn), tm=512)


def kernel(x, c, w_ada, b_ada, g_pre_mix, g_post_mix, w_in, mu_shift, w0, w2, a0, a2, g2, k_k, k_a, r_k,
           lnx_w, lnx_b, w_a, w_b, w_out, g_pre_ffn, g_post_ffn, w_ffn_in, w_ffn_out):
    for i in range(w_in.shape[0]):
        x = _layer(x, c, w_ada[i], b_ada[i], g_pre_mix[i], g_post_mix[i], w_in[i], mu_shift[i], w0[i],
                   w2[i], a0[i], a2[i], g2[i], k_k[i], k_a[i], r_k[i], lnx_w[i], lnx_b[i], w_a[i], w_b[i],
                   w_out[i], g_pre_ffn[i], g_post_ffn[i], w_ffn_in[i], w_ffn_out[i])
    return x
```

```python
import functools
import math

import jax
import jax.numpy as jnp
from jax import lax
from jax.experimental import pallas as pl
from jax.experimental.pallas import tpu as pltpu

D_MODEL = 1024
HEAD_DIM = 64
RWKV_HEADS = 12
RWKV_WIDTH = RWKV_HEADS * HEAD_DIM
DECAY_LORA = 64
ICLR_LORA = 64
GATE_LORA = 128
LORA_WIDTH = DECAY_LORA + ICLR_LORA + GATE_LORA
ATTN_GROUPS = ((128, 1), (512, 4), (2048, 16))
HEADS_PER_GROUP = 4
GROUP_WIDTH = HEADS_PER_GROUP * HEAD_DIM
ATTN_WIDTH = len(ATTN_GROUPS) * GROUP_WIDTH
ATTN_BLOCK = 128
ROPE_THETA = 10000.0
SHIFT_WIDTH = 3 * RWKV_WIDTH + LORA_WIDTH
NORM_EPS = 1e-6
GN_EPS = 64e-5

CHUNK = 64
INV_BASE = 16
LANES = 128
NEG_BIG = -1e30
LOG2E = math.log2(math.e)
SCAN_OPERAND_GROUPS = 4
ATTN_BLOCKS_PER_ITER = 8
SUB_ROWS = 256
ATTN_PROJ_SUB_ROWS = 512
VMEM_LIMIT = 56 * 1024 * 1024

F32 = jnp.float32
BF16 = jnp.bfloat16


def _bf(x):
    return x.astype(BF16)


def _div(x, n):
    assert n & (n - 1) == 0
    return jnp.right_shift(x, n.bit_length() - 1)


def _rem(x, n):
    assert n & (n - 1) == 0
    return jnp.bitwise_and(x, n - 1)


def _sigmoid(x):
    return 1.0 / (1.0 + jnp.exp2(x * (-LOG2E)))


def _split_hi_lo(x):
    hi = _bf(x)
    lo = _bf(x - hi.astype(F32))
    return hi, lo


def _mm_exact_lhs(m_bf, x):
    hi, lo = _split_hi_lo(x)
    return (jnp.dot(m_bf, hi, preferred_element_type=F32)
            + jnp.dot(m_bf, lo, preferred_element_type=F32))


def _head_sum(x):
    tile = 2 * LANES
    ri = _div(lax.broadcasted_iota(jnp.int32, (tile, tile), 0), HEAD_DIM)
    ci = _div(lax.broadcasted_iota(jnp.int32, (tile, tile), 1), HEAD_DIM)
    ones_bd = jnp.where(ri == ci, 1.0, 0.0).astype(BF16)
    xb = _bf(x)
    parts = [jnp.dot(xb[:, j * tile:(j + 1) * tile], ones_bd, preferred_element_type=F32)
             for j in range(x.shape[-1] // tile)]
    return jnp.concatenate(parts, axis=-1)


def _norm_modulate(x, g, shift, scale):
    y = x * lax.rsqrt(jnp.mean(x * x, axis=-1, keepdims=True) + NORM_EPS)
    return y * (g * (1.0 + scale)) + shift


def _rms(x, g):
    return x * lax.rsqrt(jnp.mean(x * x, axis=-1, keepdims=True) + NORM_EPS) * g


def _mod_kernel(c_ref, w_ref, b_ref, o_ref):
    c = c_ref[...]
    hi, lo = _split_hi_lo(c * _sigmoid(c))
    w = _bf(w_ref[...])
    o_ref[...] = (jnp.dot(hi, w, preferred_element_type=F32)
                  + jnp.dot(lo, w, preferred_element_type=F32) + b_ref[...])


def _mod(c, w_ada, b_ada):
    bsz, d = c.shape
    n = w_ada.shape[1]
    tn = d
    return pl.pallas_call(
        _mod_kernel,
        grid=(n // tn,),
        in_specs=[pl.BlockSpec((bsz, d), lambda j: (0, 0)),
                  pl.BlockSpec((d, tn), lambda j: (0, j)),
                  pl.BlockSpec((1, tn), lambda j: (0, j))],
        out_specs=pl.BlockSpec((bsz, tn), lambda j: (0, j)),
        out_shape=jax.ShapeDtypeStruct((bsz, n), F32),
        name="mod",
    )(c, w_ada, b_ada.reshape(1, n))


def _rwkv_prep_kernel(x_ref, shift_ref, scale_ref, g_ref, w_ref, mu_ref, wl_ref, w0_ref, a0_ref,
                      kk_ref, ka_ref, rk_ref,
                      r_o, k_o, v_o, nkk_o, b_o, cum_o, bonus_o, gate_o,
                      carry_ref):
    i = pl.program_id(1)
    rw = RWKV_WIDTH
    sub = SUB_ROWS

    @pl.when(i == 0)
    def _():
        carry_ref[...] = jnp.zeros_like(carry_ref)

    ri = lax.broadcasted_iota(jnp.int32, (sub, sub), 0)
    ci = lax.broadcasted_iota(jnp.int32, (sub, sub), 1)
    same = _div(ri, CHUNK) == _div(ci, CHUNK)
    lower = jnp.where(same & (ci <= ri), 1.0, 0.0).astype(BF16)
    row0 = lax.broadcasted_iota(jnp.int32, (8, SHIFT_WIDTH), 0) == 0
    assert DECAY_LORA + ICLR_LORA == LANES
    lane = lax.broadcasted_iota(jnp.int32, (sub, LANES), 1)

    subs = range(x_ref.shape[1] // sub)
    rows = [slice(sb * sub, (sb + 1) * sub) for sb in subs]
    h = [_bf(_norm_modulate(x_ref[0, rows[sb], :], g_ref[...], shift_ref[0], scale_ref[0])) for sb in subs]
    p = [jnp.dot(h[sb], w_ref[...], preferred_element_type=F32) for sb in subs]

    prev_row = carry_ref[7:8, :]
    ps = []
    for sb in subs:
        rolled = pltpu.roll(p[sb], 1, axis=0)
        shifted = jnp.concatenate([jnp.where(row0, prev_row, rolled[:8]), rolled[8:]], axis=0)
        last8 = p[sb][sub - 8:, :]
        prev_row = last8[7:8, :]
        ps.append(p[sb] + mu_ref[...] * (shifted - p[sb]))
    carry_ref[...] = last8

    z = []
    for sb in subs:
        wa_in = ps[sb][:, 3 * rw:3 * rw + LANES]
        g_in = ps[sb][:, 3 * rw + LANES:]
        z.append(_bf(jnp.concatenate([jnp.where(lane < DECAY_LORA, jnp.tanh(wa_in), wa_in),
                                      _sigmoid(g_in)], axis=1)))
    lo = [jnp.dot(z[sb], wl_ref[...], preferred_element_type=F32) for sb in subs]

    pr = [ps[sb][:, :rw] for sb in subs]
    pk = [ps[sb][:, rw:2 * rw] for sb in subs]
    pv = [ps[sb][:, 2 * rw:3 * rw] for sb in subs]
    kkr = [pk[sb] * kk_ref[...] for sb in subs]
    kk_sq = [_head_sum(kkr[sb] * kkr[sb]) for sb in subs]
    a = [_sigmoid(a0_ref[...] + lo[sb][:, rw:2 * rw]) for sb in subs]
    one_minus_ka = 1.0 - ka_ref[...]
    k = [pk[sb] * (one_minus_ka + a[sb] * ka_ref[...]) for sb in subs]
    rk_sum = [_head_sum(pr[sb] * k[sb] * rk_ref[...]) for sb in subs]
    lw = [(-math.exp(-0.5) * LOG2E) * _sigmoid(w0_ref[...] + lo[sb][:, :rw]) for sb in subs]
    cum = [_mm_exact_lhs(lower, lw[sb]) for sb in subs]

    for sb in subs:
        kkn = kkr[sb] * lax.rsqrt(jnp.maximum(kk_sq[sb], 1e-24))
        gate_o[0, rows[sb], :] = lo[sb][:, 2 * rw:]
        bonus_o[0, rows[sb], :] = rk_sum[sb] * pv[sb]
        cum_o[0, rows[sb], :] = cum[sb]
        r_o[0, rows[sb], :] = _bf(pr[sb])
        k_o[0, rows[sb], :] = _bf(k[sb])
        v_o[0, rows[sb], :] = _bf(pv[sb])
        nkk_o[0, rows[sb], :] = _bf(-kkn)
        b_o[0, rows[sb], :] = _bf(kkn * a[sb])


def _rwkv_prep(x, mod3, g_pre, w_rwkv, mu, w_lora, w0, a0, k_k, k_a, r_k, tm):
    bsz, seq, d = x.shape
    rw = RWKV_WIDTH
    row_spec = pl.BlockSpec((1, tm, rw), lambda b, i: (b, i, 0))
    vec = lambda n: pl.BlockSpec((1, n), lambda b, i: (0, 0))
    big = jax.ShapeDtypeStruct((bsz, seq, rw), F32)
    half = jax.ShapeDtypeStruct((bsz, seq, rw), BF16)
    return pl.pallas_call(
        _rwkv_prep_kernel,
        grid=(bsz, seq // tm),
        in_specs=[pl.BlockSpec((1, tm, d), lambda b, i: (b, i, 0)),
                  pl.BlockSpec((1, 1, d), lambda b, i: (b, 0, 0)),
                  pl.BlockSpec((1, 1, d), lambda b, i: (b, 0, 1)),
                  vec(d),
                  pl.BlockSpec((d, SHIFT_WIDTH), lambda b, i: (0, 0), pipeline_mode=pl.Buffered(1)),
                  vec(SHIFT_WIDTH),
                  pl.BlockSpec((LORA_WIDTH, 3 * rw), lambda b, i: (0, 0), pipeline_mode=pl.Buffered(1)),
                  vec(rw), vec(rw), vec(rw), vec(rw), vec(rw)],
        out_specs=[row_spec] * 8,
        out_shape=[half] * 5 + [big] * 3,
        scratch_shapes=[pltpu.VMEM((8, SHIFT_WIDTH), F32)],
        compiler_params=pltpu.CompilerParams(
            dimension_semantics=("arbitrary", "arbitrary"), vmem_limit_bytes=VMEM_LIMIT),
        name="rwkv_prep",
    )(x, mod3, mod3, g_pre, w_rwkv, mu, w_lora, w0, a0, k_k, k_a, r_k)


def _pair_blockdiag(x):
    first = lax.broadcasted_iota(jnp.int32, x.shape, 1) < HEAD_DIM
    zero = jnp.zeros_like(x)
    return jnp.concatenate([jnp.where(first, x, zero), jnp.where(first, zero, x)], axis=0)


def _pmm(a, b):
    return jnp.dot(_bf(a), _pair_blockdiag(_bf(b)), preferred_element_type=F32)


def _pmm_nt(a, b):
    return lax.dot_general(_bf(a), _pair_blockdiag(_bf(b)), (((1,), (1,)), ((), ())),
                           preferred_element_type=F32)


def _pmm_tn(a, b):
    full = lax.dot_general(_bf(a), _bf(b), (((0,), (0,)), ((), ())), preferred_element_type=F32)
    first = lax.broadcasted_iota(jnp.int32, (HEAD_DIM, LANES), 1) < HEAD_DIM
    return jnp.where(first, full[:HEAD_DIM], full[HEAD_DIM:])


def _unit_lower_inverse(a_list, masks):
    eye, diag_mask, merge_masks = masks
    ps = [jnp.where(diag_mask, a, 0.0) for a in a_list]
    ts = [eye + p for p in ps]
    ps = [_pmm(p, p) for p in ps]
    m = 2
    while 2 * m < INV_BASE:
        both = [_pmm(jnp.concatenate([p, t], axis=0), p) for p, t in zip(ps, ts)]
        ps = [x[:CHUNK] for x in both]
        ts = [t + x[CHUNK:] for t, x in zip(ts, both)]
        m *= 2
    ts = [t + _pmm(t, p) for t, p in zip(ts, ps)]
    for mk in merge_masks:
        ws = [_pmm(jnp.where(mk, a, 0.0), t) for a, t in zip(a_list, ts)]
        ts = [t + _pmm(t, w) for t, w in zip(ts, ws)]
    return ts


def _rwkv_scan_kernel(r_ref, k_ref, v_ref, nkk_ref, b_ref, cum_ref, y_ref, st_ref):
    @pl.when(pl.program_id(1) == 0)
    def _():
        st_ref[...] = jnp.zeros_like(st_ref)

    ri = lax.broadcasted_iota(jnp.int32, (CHUNK, LANES), 0)
    ci = _rem(lax.broadcasted_iota(jnp.int32, (CHUNK, LANES), 1), HEAD_DIM)
    strict = ri > ci
    incl = ri >= ci
    ri2 = lax.broadcasted_iota(jnp.int32, (2 * CHUNK, LANES), 0)
    ci2 = _rem(lax.broadcasted_iota(jnp.int32, (2 * CHUNK, LANES), 1), HEAD_DIM)
    rloc = _rem(ri2, CHUNK)
    stacked = (rloc > ci2) | ((rloc == ci2) & (ri2 >= CHUNK))
    eye = jnp.where(ri == ci, 1.0, 0.0)
    diag_mask = _div(ri, INV_BASE) == _div(ci, INV_BASE)
    merge_masks = []
    bsize = INV_BASE
    while bsize < CHUNK:
        merge_masks.append((_div(ri, 2 * bsize) == _div(ci, 2 * bsize))
                           & (_div(ri, bsize) != _div(ci, bsize)))
        bsize *= 2
    masks = (eye, diag_mask, merge_masks)

    n_rows = r_ref.shape[0]
    n_chunks = r_ref.shape[1] // CHUNK
    n_pairs = RWKV_WIDTH // LANES
    seqs = [(b, j) for b in range(n_rows) for j in range(n_pairs)]
    items = [(c, s) for c in range(n_chunks) for s in seqs]

    def tile(ref, c, s):
        b, j = s
        return ref[b, c * CHUNK:(c + 1) * CHUNK, j * LANES:(j + 1) * LANES]

    first_row = ri == 0
    nt = lambda a, b: lax.dot_general(a, b, (((1,), (1,)), ((), ())), preferred_element_type=F32)
    qs, bkg, gcs, ab, akv = {}, {}, {}, {}, {}
    n_groups = min(SCAN_OPERAND_GROUPS, len(items))
    for grp in range(n_groups):
        group = items[grp::n_groups]
        bk, kd = {}, {}
        for it in group:
            cum = tile(cum_ref, *it)
            cum_last = cum[CHUNK - 1:CHUNK, :]
            e_cum = jnp.exp2(cum)
            e_prev = jnp.exp2(jnp.where(first_row, 0.0, pltpu.roll(cum, 1, axis=0)))
            e_inv = jnp.exp2(-cum)
            e_rest = jnp.exp2(cum_last - cum)
            gcs[it] = jnp.exp2(cum_last)
            r, k = tile(r_ref, *it).astype(F32), tile(k_ref, *it).astype(F32)
            nkk, b = tile(nkk_ref, *it).astype(F32), tile(b_ref, *it).astype(F32)
            qs[it] = _bf(jnp.concatenate([nkk * e_prev, r * e_cum], axis=0))
            bk[it], kd[it] = _bf(b * e_inv), _bf(k * e_inv)
            bkg[it] = _bf(jnp.concatenate([b * e_rest, k * e_rest], axis=0))
        abk = {it: nt(qs[it], jnp.concatenate([_pair_blockdiag(bk[it]), _pair_blockdiag(kd[it])], axis=0))
               for it in group}
        for it in group:
            ab[it] = abk[it][:, :LANES]
        ak = {it: jnp.where(stacked, abk[it][:, LANES:], 0.0) for it in group}
        for it in group:
            akv[it] = _pmm(ak[it], tile(v_ref, *it))

    t_list = _unit_lower_inverse([jnp.where(strict, ab[it][:CHUNK], 0.0) for it in items], masks)
    t = dict(zip(items, t_list))

    state = {s: st_ref[s[0], s[1]] for s in seqs}
    for c in range(n_chunks):
        xs = {s: _pmm_nt(qs[(c, s)], state[s]) for s in seqs}
        u = {s: _pmm(t[(c, s)], xs[s][:CHUNK] + akv[(c, s)][:CHUNK]) for s in seqs}
        for s in seqs:
            b, j = s
            y = xs[s][CHUNK:] + akv[(c, s)][CHUNK:] + _pmm(
                jnp.where(incl, ab[(c, s)][CHUNK:], 0.0), u[s])
            y_ref[b, c * CHUNK:(c + 1) * CHUNK, j * LANES:(j + 1) * LANES] = y
        new_state = {}
        for s in seqs:
            uv = jnp.concatenate([_bf(u[s]), tile(v_ref, c, s)], axis=0)
            new_state[s] = state[s] * gcs[(c, s)] + _pmm_tn(uv, bkg[(c, s)])
        state = new_state
    for s in seqs:
        st_ref[s[0], s[1]] = state[s]


def _rwkv_scan(r, k, v, nkk, b, cum, rows_per_step, chunks_per_step):
    bsz, seq, rw = r.shape
    ts = chunks_per_step * CHUNK
    row_spec = pl.BlockSpec((rows_per_step, ts, rw), lambda b, c: (b, c, 0))
    return pl.pallas_call(
        _rwkv_scan_kernel,
        grid=(bsz // rows_per_step, seq // ts),
        in_specs=[row_spec] * 6,
        out_specs=row_spec,
        out_shape=jax.ShapeDtypeStruct((bsz, seq, rw), F32),
        scratch_shapes=[pltpu.VMEM((rows_per_step, rw // LANES, HEAD_DIM, LANES), F32)],
        compiler_params=pltpu.CompilerParams(dimension_semantics=("arbitrary", "arbitrary")),
        name="rwkv_scan",
    )(r, k, v, nkk, b, cum)


def _attn_proj_kernel(x_ref, shift_ref, scale_ref, g_ref, w_ref, cos_ref, sin_ref, *rest):
    outs, buf = rest[:-1], rest[-1]
    SUB_ROWS = ATTN_PROJ_SUB_ROWS
    aw = ATTN_WIDTH
    reps = aw // LANES
    lane = lax.broadcasted_iota(jnp.int32, (SUB_ROWS, aw), 1)
    first_half = _rem(lane, HEAD_DIM) < (HEAD_DIM // 2)
    tiles_per_group = GROUP_WIDTH // LANES

    subs = range(x_ref.shape[1] // SUB_ROWS)
    ps = []
    for sb in subs:
        rows = slice(sb * SUB_ROWS, (sb + 1) * SUB_ROWS)
        h = _norm_modulate(x_ref[0, rows, :], g_ref[...], shift_ref[0], scale_ref[0])
        ps.append(jnp.dot(_bf(h), w_ref[...], preferred_element_type=F32))
    for sb in subs:
        rows = slice(sb * SUB_ROWS, (sb + 1) * SUB_ROWS)
        p = ps[sb]
        cos_t, sin_t = cos_ref[rows, :], sin_ref[rows, :]

        def rope(t, scale):
            cos = jnp.concatenate([cos_t * scale] * reps, axis=-1)
            sin = jnp.concatenate([sin_t * scale] * reps, axis=-1)
            partner = jnp.where(first_half, pltpu.roll(t, aw - HEAD_DIM // 2, axis=1),
                                pltpu.roll(t, HEAD_DIM // 2, axis=1))
            return t * cos + partner * sin

        roped = (rope(p[:, :aw], HEAD_DIM ** -0.5 * LOG2E), rope(p[:, aw:2 * aw], 1.0), p[:, 2 * aw:])
        n = 0
        for which in range(3):
            for g, (_, dil) in enumerate(ATTN_GROUPS):
                o_ref = outs[n]
                n += 1
                out_rows = slice(sb * SUB_ROWS // dil, (sb + 1) * SUB_ROWS // dil)
                for ct in range(tiles_per_group):
                    c0 = g * GROUP_WIDTH + ct * LANES
                    val = roped[which][:, c0:c0 + LANES]
                    if dil == 1:
                        o_ref[0, 0, out_rows, ct * LANES:(ct + 1) * LANES] = _bf(val)
                        continue
                    slot = (which * len(ATTN_GROUPS) + g) * tiles_per_group + ct
                    buf[sb, slot] = val
                    for r in range(dil):
                        o_ref[0, r, out_rows, ct * LANES:(ct + 1) * LANES] = _bf(
                            buf[sb, slot, pl.ds(r, SUB_ROWS // dil, stride=dil), :])


def _attn_proj(x, mod3, g_pre, w_qkv, cos_t, sin_t, tm):
    bsz, seq, d = x.shape
    out_specs, out_shape = [], []
    for _ in range(3):
        for _, dil in ATTN_GROUPS:
            out_specs.append(pl.BlockSpec((1, dil, tm // dil, GROUP_WIDTH), lambda b, i: (b, 0, i, 0)))
            out_shape.append(jax.ShapeDtypeStruct((bsz, dil, seq // dil, GROUP_WIDTH), BF16))
    return pl.pallas_call(
        _attn_proj_kernel,
        grid=(bsz, seq // tm),
        in_specs=[pl.BlockSpec((1, tm, d), lambda b, i: (b, i, 0)),
                  pl.BlockSpec((1, 1, d), lambda b, i: (b, 0, 0)),
                  pl.BlockSpec((1, 1, d), lambda b, i: (b, 0, 1)),
                  pl.BlockSpec((1, d), lambda b, i: (0, 0)),
                  pl.BlockSpec((pl.Element(d), pl.Element(3 * ATTN_WIDTH)), lambda b, i: (0, SHIFT_WIDTH),
                               pipeline_mode=pl.Buffered(1)),
                  pl.BlockSpec((tm, LANES), lambda b, i: (i, 0)),
                  pl.BlockSpec((tm, LANES), lambda b, i: (i, 0))],
        out_specs=out_specs,
        out_shape=out_shape,
        scratch_shapes=[pltpu.VMEM((tm // ATTN_PROJ_SUB_ROWS, 3 * ATTN_WIDTH // LANES,
                                    ATTN_PROJ_SUB_ROWS, LANES), F32)],
        compiler_params=pltpu.CompilerParams(
            dimension_semantics=("arbitrary", "arbitrary"), vmem_limit_bytes=VMEM_LIMIT),
        name="attn_proj",
    )(x, mod3, mod3, g_pre, w_qkv, cos_t, sin_t)


def _attn_kernel(*refs):
    qkv = refs[:9]
    o_ref = refs[9]
    num_ref, m_ref, den_ref = refs[10:]
    blk = ATTN_BLOCK
    qi = lax.broadcasted_iota(jnp.int32, (blk, blk), 0)
    kj = lax.broadcasted_iota(jnp.int32, (blk, blk), 1)
    bias_cur = jnp.where(kj <= qi, 0.0, NEG_BIG)
    bias_prev = jnp.where(kj >= qi, 0.0, NEG_BIG)
    lane = lax.broadcasted_iota(jnp.int32, (blk, LANES), 1)
    head0 = lane < HEAD_DIM

    order = sorted(range(len(ATTN_GROUPS)), key=lambda g: -ATTN_GROUPS[g][1])
    assert ATTN_GROUPS[order[-1]][1] == 1
    for g in order:
        dil = ATTN_GROUPS[g][1]
        q_ref, k_ref, v_ref = qkv[g], qkv[3 + g], qkv[6 + g]
        nb = q_ref.shape[2] // blk

        nper = min(nb, ATTN_BLOCKS_PER_ITER)
        rper = ATTN_BLOCKS_PER_ITER // nper
        whole = nper == nb

        def body(idx, carry, q_ref=q_ref, k_ref=k_ref, v_ref=v_ref, dil=dil, nb=nb, g=g,
                 nper=nper, rper=rper, whole=whole):
            if whole:
                r0, n0 = idx * rper, 0
                base = 0
            else:
                r0 = _div(idx, nb // nper)
                n0 = _rem(idx, nb // nper) * nper
                base = pl.multiple_of(n0 * blk, blk)
            nt = lambda a, b: lax.dot_general(a, b, (((1,), (1,)), ((), ())), preferred_element_type=F32)
            one = jnp.ones((blk, LANES), BF16)
            zero = jnp.zeros((blk, LANES), BF16)
            q_msk, ks, v_aug = {}, {}, {}
            for rr in range(rper):
                r = r0 + rr
                first = 0 if whole else -1
                for i in range(first, nper):
                    if i < 0:
                        st = pl.multiple_of(jnp.maximum(n0 - 1, 0) * blk, blk)
                    else:
                        st = base + i * blk
                    ks[(rr, i)] = k_ref[0, r, pl.ds(st, blk), :]
                    v = v_ref[0, r, pl.ds(st, blk), :]
                    v_aug[(rr, i)] = (jnp.where(head0, v, one), jnp.where(head0, one, v))
                    if i >= 0:
                        q = q_ref[0, r, pl.ds(st, blk), :]
                        q_msk[(rr, i)] = (jnp.where(head0, q, zero), jnp.where(head0, zero, q))
            items = [(rr, i, e) for rr in range(rper) for i in range(nper) for e in range(2)]
            has_prev = lambda it: (it[0], it[1] - 1) in ks
            prev_of = lambda it: (it[0], it[1] - 1)
            k_win = {it[:2]: jnp.concatenate([ks[prev_of(it)], ks[it[:2]]], axis=0)
                     for it in items if has_prev(it)}
            bias_win = jnp.concatenate([bias_prev, bias_cur], axis=1)
            bias_first = bias_win if whole else jnp.where(n0 > 0, bias_win, jnp.concatenate(
                [jnp.full_like(bias_prev, NEG_BIG), bias_cur], axis=1))
            s, ms, ps, pv = {}, {}, {}, {}

            def scores(it):
                s[it] = nt(q_msk[it[:2]][it[2]], k_win[it[:2]] if has_prev(it) else ks[it[:2]])

            def softmax(it):
                if has_prev(it):
                    sb = s.pop(it) + (bias_first if it[1] == 0 else bias_win)
                else:
                    sb = s.pop(it) + bias_cur
                m = jnp.max(sb, axis=-1, keepdims=True)
                ms[it] = m
                ps[it] = _bf(jnp.exp2(sb - m))

            def weighted_values(it):
                if has_prev(it):
                    v_win = jnp.concatenate([v_aug[prev_of(it)][it[2]], v_aug[it[:2]][it[2]]], axis=0)
                else:
                    v_win = v_aug[it[:2]][it[2]]
                pv[it] = jnp.dot(ps.pop(it), v_win, preferred_element_type=F32)

            for it in items:
                scores(it)
            for it in items:
                softmax(it)
            for it in items:
                weighted_values(it)
            for rr in range(rper):
                for i in range(nper):
                    m_b = jnp.where(head0, ms[(rr, i, 0)], ms[(rr, i, 1)])
                    num_b = jnp.where(head0, pv[(rr, i, 0)], pv[(rr, i, 1)])
                    l_b = pltpu.roll(jnp.where(head0, pv[(rr, i, 1)], pv[(rr, i, 0)]), HEAD_DIM, axis=1)
                    if dil == 1:
                        rows = pl.ds(base + i * blk, blk)
                    else:
                        rows = pl.ds((n0 + i) * (blk * dil) + r0 + rr, blk, stride=dil)
                    if g == order[0]:
                        num_ref[rows, :] = num_b
                        m_ref[rows, :] = m_b
                        den_ref[rows, :] = l_b
                        continue
                    m_old = m_ref[rows, :]
                    m_new = jnp.maximum(m_old, m_b)
                    w_old = jnp.exp2(m_old - m_new)
                    w_b = jnp.exp2(m_b - m_new)
                    num = num_ref[rows, :] * w_old + num_b * w_b
                    den = den_ref[rows, :] * w_old + l_b * w_b
                    if g == order[-1]:
                        o_ref[0, rows, :] = _bf(num / den)
                    else:
                        num_ref[rows, :] = num
                        den_ref[rows, :] = den
                        m_ref[rows, :] = m_new
            return carry

        lax.fori_loop(0, dil * nb // (nper * rper), body, 0, unroll=2)


def _attn(qkv, seq):
    bsz = qkv[0].shape[0]
    in_specs = []
    for _ in range(3):
        for _, dil in ATTN_GROUPS:
            in_specs.append(pl.BlockSpec((1, dil, seq // dil, LANES), lambda b, sp: (b, 0, 0, sp)))
    return pl.pallas_call(
        _attn_kernel,
        grid=(bsz, GROUP_WIDTH // LANES),
        in_specs=in_specs,
        out_specs=pl.BlockSpec((1, seq, LANES), lambda b, sp: (b, 0, sp)),
        out_shape=jax.ShapeDtypeStruct((bsz, seq, GROUP_WIDTH), BF16),
        scratch_shapes=[pltpu.VMEM((seq, LANES), F32)] * 3,
        compiler_params=pltpu.CompilerParams(
            dimension_semantics=("arbitrary", "arbitrary"), vmem_limit_bytes=VMEM_LIMIT),
        name="attn",
    )(*qkv)


def _merge_kernel(x_ref, shift_ref, scale_ref, gatem_ref, g_ref, wg_ref, y_ref, bonus_ref, gate_ref,
                  lnw_ref, lnb_ref, wa_ref, o_ref, wb_ref, wout_ref, gpost_ref, out_ref):
    inv_n = 1.0 / HEAD_DIM
    subs = range(x_ref.shape[1] // SUB_ROWS)
    rows = [slice(sb * SUB_ROWS, (sb + 1) * SUB_ROWS) for sb in subs]
    yb = [jnp.dot(o_ref[0, rows[sb], :], wb_ref[...], preferred_element_type=F32) for sb in subs]
    y = [y_ref[0, rows[sb], :] for sb in subs]
    mean = [_head_sum(y[sb]) * inv_n for sb in subs]
    h = [_bf(_norm_modulate(x_ref[0, rows[sb], :], g_ref[...], shift_ref[0], scale_ref[0])) for sb in subs]
    pg = [jnp.dot(h[sb], wg_ref[...], preferred_element_type=F32) for sb in subs]
    yc = [y[sb] - mean[sb] for sb in subs]
    var = [_head_sum(yc[sb] * yc[sb]) * inv_n for sb in subs]
    pre = [_bf((yc[sb] * lax.rsqrt(var[sb] + GN_EPS) * lnw_ref[...] + lnb_ref[...]
                + bonus_ref[0, rows[sb], :]) * gate_ref[0, rows[sb], :]) for sb in subs]
    ya = [jnp.dot(pre[sb], wa_ref[...], preferred_element_type=F32) for sb in subs]
    merged = [_bf(_sigmoid(pg[sb][:, :D_MODEL]) * ya[sb] + _sigmoid(pg[sb][:, D_MODEL:]) * yb[sb])
              for sb in subs]
    zz = [jnp.dot(merged[sb], wout_ref[...], preferred_element_type=F32) for sb in subs]
    for sb in subs:
        out_ref[0, rows[sb], :] = x_ref[0, rows[sb], :] + gatem_ref[0] * _rms(zz[sb], gpost_ref[...])


def _merge(x, mod3, g_pre, w_g, y, bonus, gate, lnx_w, lnx_b, w_a, o_att, w_b, w_out, g_post, tm):
    bsz, seq, d = x.shape
    rw = RWKV_WIDTH
    const = lambda shape: pl.BlockSpec(shape, lambda b, i: (0, 0), pipeline_mode=pl.Buffered(1))
    row = lambda n: pl.BlockSpec((1, tm, n), lambda b, i: (b, i, 0))
    modv = lambda j: pl.BlockSpec((1, 1, d), lambda b, i: (b, 0, j))
    return pl.pallas_call(
        _merge_kernel,
        grid=(bsz, seq // tm),
        in_specs=[row(d), modv(0), modv(1), modv(2), const((1, d)),
                  pl.BlockSpec((pl.Element(d), pl.Element(2 * d)),
                               lambda b, i: (0, SHIFT_WIDTH + 3 * ATTN_WIDTH),
                               pipeline_mode=pl.Buffered(1)),
                  row(rw), row(rw), row(rw), const((1, rw)), const((1, rw)), const((rw, d)),
                  row(GROUP_WIDTH), const((GROUP_WIDTH, d)), const((d, d)), const((1, d))],
        out_specs=row(d),
        out_shape=jax.ShapeDtypeStruct((bsz, seq, d), F32),
        compiler_params=pltpu.CompilerParams(
            dimension_semantics=("arbitrary", "arbitrary"), vmem_limit_bytes=VMEM_LIMIT),
        name="merge",
    )(x, mod3, mod3, mod3, g_pre, w_g, y, bonus, gate, lnx_w, lnx_b, w_a, o_att, w_b, w_out, g_post)


def _ffn_kernel(x_ref, shift_ref, scale_ref, gatef_ref, g_ref, win_ref, wout_ref, gpost_ref, out_ref):
    dff = wout_ref.shape[0]
    n_sub = x_ref.shape[1] // SUB_ROWS
    rows = [slice(sb * SUB_ROWS, (sb + 1) * SUB_ROWS) for sb in range(n_sub)]
    u, gt = {}, {}
    for t in range(n_sub + 1):
        if t < n_sub:
            h = _bf(_norm_modulate(x_ref[0, rows[t], :], g_ref[...], shift_ref[0], scale_ref[0]))
            u[t] = jnp.dot(h, win_ref[:, :dff], preferred_element_type=F32)
            gt[t] = jnp.dot(h, win_ref[:, dff:], preferred_element_type=F32)
        if t >= 1:
            sb = t - 1
            ub = u.pop(sb)
            act = _bf(ub * _sigmoid(ub) * gt.pop(sb))
            f = jnp.dot(act, wout_ref[...], preferred_element_type=F32)
            out_ref[0, rows[sb], :] = x_ref[0, rows[sb], :] + gatef_ref[0] * _rms(f, gpost_ref[...])


def _ffn(x, mod3, g_pre, w_in, w_out, g_post, tm):
    bsz, seq, d = x.shape
    dff = w_out.shape[0]
    const = lambda shape: pl.BlockSpec(shape, lambda b, i: (0, 0), pipeline_mode=pl.Buffered(1))
    row = pl.BlockSpec((1, tm, d), lambda b, i: (b, i, 0))
    modv = lambda j: pl.BlockSpec((1, 1, d), lambda b, i: (b, 0, j))
    return pl.pallas_call(
        _ffn_kernel,
        grid=(bsz, seq // tm),
        in_specs=[row, modv(3), modv(4), modv(5), const((1, d)), const((d, 2 * dff)),
                  const((dff, d)), const((1, d))],
        out_specs=row,
        out_shape=jax.ShapeDtypeStruct((bsz, seq, d), F32),
        compiler_params=pltpu.CompilerParams(
            dimension_semantics=("arbitrary", "arbitrary"), vmem_limit_bytes=VMEM_LIMIT),
        name="ffn",
    )(x, mod3, mod3, mod3, g_pre, w_in, w_out, g_post)


def _rope_tables(seq):
    half = HEAD_DIM // 2
    inv_freq = ROPE_THETA ** (-jnp.arange(half, dtype=F32) / half)
    ang = jnp.arange(seq, dtype=F32)[:, None] * inv_freq[None, :]
    cos, sin = jnp.cos(ang), jnp.sin(ang)
    reps = LANES // HEAD_DIM
    cos_t = jnp.tile(jnp.concatenate([cos, cos], axis=-1), (1, reps))
    sin_t = jnp.tile(jnp.concatenate([-sin, sin], axis=-1), (1, reps))
    return cos_t, sin_t


def _layer(x, c, w_ada, b_ada, g_pre_mix, g_post_mix, w_in, mu_shift, w0, w2, a0, a2, g2,
           k_k, k_a, r_k, lnx_w, lnx_b, w_a, w_b, w_out, g_pre_ffn, g_post_ffn, w_ffn_in, w_ffn_out):
    bsz, seq, d = x.shape
    rw = RWKV_WIDTH
    row1 = lambda t: t.reshape(1, -1)
    mod3 = _mod(c, w_ada, b_ada).reshape(bsz, 1, 6 * d)

    w_rwkv = w_qkv = w_gates = _bf(w_in)
    w_lora = jnp.zeros((LORA_WIDTH, 3 * rw), F32)
    w_lora = w_lora.at[:DECAY_LORA, :rw].set(w2)
    w_lora = w_lora.at[DECAY_LORA:DECAY_LORA + ICLR_LORA, rw:2 * rw].set(a2)
    w_lora = _bf(w_lora.at[DECAY_LORA + ICLR_LORA:, 2 * rw:].set(g2))

    prep = _rwkv_prep(x, mod3, row1(g_pre_mix), w_rwkv, row1(mu_shift), w_lora, row1(w0), row1(a0),
                      row1(k_k), row1(k_a), row1(r_k), tm=512)
    r, k, v, nkk, b, cum, bonus, gate = prep
    y = _rwkv_scan(r, k, v, nkk, b, cum, rows_per_step=8, chunks_per_step=1)

    cos_t, sin_t = _rope_tables(seq)
    qkv = _attn_proj(x, mod3, row1(g_pre_mix), w_qkv, cos_t, sin_t, tm=1024)
    o_att = _attn(qkv, seq)

    x1 = _merge(x, mod3, row1(g_pre_mix), w_gates, y, bonus, gate, row1(lnx_w), row1(lnx_b),
                _bf(w_a), o_att, _bf(w_b), _bf(w_out), row1(g_post_mix), tm=512)
    return _ffn(x1, mod3, row1(g_pre_ffn), _bf(w_ffn_in), _bf(w_ffn_out), row1(g_post_ffn), tm=1024)


def kernel(x, c, w_ada, b_ada, g_pre_mix, g_post_mix, w_in, mu_shift, w0, w2, a0, a2, g2, k_k, k_a, r_k,
           lnx_w, lnx_b, w_a, w_b, w_out, g_pre_ffn, g_post_ffn, w_ffn_in, w_ffn_out):
    for i in range(w_in.shape[0]):
        x = _layer(x, c, w_ada[i], b_ada[i], g_pre_mix[i], g_post_mix[i], w_in[i], mu_shift[i], w0[i],
                   w2[i], a0[i], a2[i], g2[i], k_k[i], k_a[i], r_k[i], lnx_w[i], lnx_b[i], w_a[i], w_b[i],
                   w_out[i], g_pre_ffn[i], g_post_ffn[i], w_ffn_in[i], w_ffn_out[i])
    return x
```

```python
import functools
import math

import jax
import jax.numpy as jnp
from jax import lax
from jax.experimental import pallas as pl
from jax.experimental.pallas import tpu as pltpu

D_MODEL = 1024
HEAD_DIM = 64
RWKV_HEADS = 12
RWKV_WIDTH = RWKV_HEADS * HEAD_DIM
DECAY_LORA = 64
ICLR_LORA = 64
GATE_LORA = 128
LORA_WIDTH = DECAY_LORA + ICLR_LORA + GATE_LORA
ATTN_GROUPS = ((128, 1), (512, 4), (2048, 16))
HEADS_PER_GROUP = 4
GROUP_WIDTH = HEADS_PER_GROUP * HEAD_DIM
ATTN_WIDTH = len(ATTN_GROUPS) * GROUP_WIDTH
ATTN_BLOCK = 128
ROPE_THETA = 10000.0
SHIFT_WIDTH = 3 * RWKV_WIDTH + LORA_WIDTH
NORM_EPS = 1e-6
GN_EPS = 64e-5

CHUNK = 64
INV_BASE = 8
LANES = 128
NEG_BIG = -1e30
LOG2E = math.log2(math.e)
SCAN_OPERAND_GROUPS = 4
ATTN_BLOCKS_PER_ITER = 8
SUB_ROWS = 256
ATTN_PROJ_SUB_ROWS = 512
VMEM_LIMIT = 56 * 1024 * 1024

F32 = jnp.float32
BF16 = jnp.bfloat16


def _bf(x):
    return x.astype(BF16)


def _div(x, n):
    assert n & (n - 1) == 0
    return jnp.right_shift(x, n.bit_length() - 1)


def _rem(x, n):
    assert n & (n - 1) == 0
    return jnp.bitwise_and(x, n - 1)


def _sigmoid(x):
    return 1.0 / (1.0 + jnp.exp2(x * (-LOG2E)))


def _split_hi_lo(x):
    hi = _bf(x)
    lo = _bf(x - hi.astype(F32))
    return hi, lo


def _mm_exact_lhs(m_bf, x):
    hi, lo = _split_hi_lo(x)
    return (jnp.dot(m_bf, hi, preferred_element_type=F32)
            + jnp.dot(m_bf, lo, preferred_element_type=F32))


def _head_sum(x):
    tile = 2 * LANES
    ri = _div(lax.broadcasted_iota(jnp.int32, (tile, tile), 0), HEAD_DIM)
    ci = _div(lax.broadcasted_iota(jnp.int32, (tile, tile), 1), HEAD_DIM)
    ones_bd = jnp.where(ri == ci, 1.0, 0.0).astype(BF16)
    xb = _bf(x)
    parts = [jnp.dot(xb[:, j * tile:(j + 1) * tile], ones_bd, preferred_element_type=F32)
             for j in range(x.shape[-1] // tile)]
    return jnp.concatenate(parts, axis=-1)


def _norm_modulate(x, g, shift, scale):
    y = x * lax.rsqrt(jnp.mean(x * x, axis=-1, keepdims=True) + NORM_EPS)
    return y * (g * (1.0 + scale)) + shift


def _rms(x, g):
    return x * lax.rsqrt(jnp.mean(x * x, axis=-1, keepdims=True) + NORM_EPS) * g


def _mod_kernel(c_ref, w_ref, b_ref, o_ref):
    c = c_ref[...]
    hi, lo = _split_hi_lo(c * _sigmoid(c))
    w = _bf(w_ref[...])
    o_ref[...] = (jnp.dot(hi, w, preferred_element_type=F32)
                  + jnp.dot(lo, w, preferred_element_type=F32) + b_ref[...])


def _mod(c, w_ada, b_ada):
    bsz, d = c.shape
    n = w_ada.shape[1]
    tn = d
    return pl.pallas_call(
        _mod_kernel,
        grid=(n // tn,),
        in_specs=[pl.BlockSpec((bsz, d), lambda j: (0, 0)),
                  pl.BlockSpec((d, tn), lambda j: (0, j)),
                  pl.BlockSpec((1, tn), lambda j: (0, j))],
        out_specs=pl.BlockSpec((bsz, tn), lambda j: (0, j)),
        out_shape=jax.ShapeDtypeStruct((bsz, n), F32),
        name="mod",
    )(c, w_ada, b_ada.reshape(1, n))


def _rwkv_prep_kernel(x_ref, shift_ref, scale_ref, g_ref, w_ref, mu_ref, wl_ref, w0_ref, a0_ref,
                      kk_ref, ka_ref, rk_ref,
                      r_o, k_o, v_o, nkk_o, b_o, cum_o, bonus_o, gate_o,
                      carry_ref):
    i = pl.program_id(1)
    rw = RWKV_WIDTH
    sub = SUB_ROWS

    @pl.when(i == 0)
    def _():
        carry_ref[...] = jnp.zeros_like(carry_ref)

    ri = lax.broadcasted_iota(jnp.int32, (sub, sub), 0)
    ci = lax.broadcasted_iota(jnp.int32, (sub, sub), 1)
    same = _div(ri, CHUNK) == _div(ci, CHUNK)
    lower = jnp.where(same & (ci <= ri), 1.0, 0.0).astype(BF16)
    row0 = lax.broadcasted_iota(jnp.int32, (8, SHIFT_WIDTH), 0) == 0
    assert DECAY_LORA + ICLR_LORA == LANES
    lane = lax.broadcasted_iota(jnp.int32, (sub, LANES), 1)

    subs = range(x_ref.shape[1] // sub)
    rows = [slice(sb * sub, (sb + 1) * sub) for sb in subs]
    h = [_bf(_norm_modulate(x_ref[0, rows[sb], :], g_ref[...], shift_ref[0], scale_ref[0])) for sb in subs]
    p = [jnp.dot(h[sb], w_ref[...], preferred_element_type=F32) for sb in subs]

    prev_row = carry_ref[7:8, :]
    ps = []
    for sb in subs:
        rolled = pltpu.roll(p[sb], 1, axis=0)
        shifted = jnp.concatenate([jnp.where(row0, prev_row, rolled[:8]), rolled[8:]], axis=0)
        last8 = p[sb][sub - 8:, :]
        prev_row = last8[7:8, :]
        ps.append(p[sb] + mu_ref[...] * (shifted - p[sb]))
    carry_ref[...] = last8

    z = []
    for sb in subs:
        wa_in = ps[sb][:, 3 * rw:3 * rw + LANES]
        g_in = ps[sb][:, 3 * rw + LANES:]
        z.append(_bf(jnp.concatenate([jnp.where(lane < DECAY_LORA, jnp.tanh(wa_in), wa_in),
                                      _sigmoid(g_in)], axis=1)))
    lo = [jnp.dot(z[sb], wl_ref[...], preferred_element_type=F32) for sb in subs]

    pr = [ps[sb][:, :rw] for sb in subs]
    pk = [ps[sb][:, rw:2 * rw] for sb in subs]
    pv = [ps[sb][:, 2 * rw:3 * rw] for sb in subs]
    kkr = [pk[sb] * kk_ref[...] for sb in subs]
    kk_sq = [_head_sum(kkr[sb] * kkr[sb]) for sb in subs]
    a = [_sigmoid(a0_ref[...] + lo[sb][:, rw:2 * rw]) for sb in subs]
    one_minus_ka = 1.0 - ka_ref[...]
    k = [pk[sb] * (one_minus_ka + a[sb] * ka_ref[...]) for sb in subs]
    rk_sum = [_head_sum(pr[sb] * k[sb] * rk_ref[...]) for sb in subs]
    lw = [(-math.exp(-0.5) * LOG2E) * _sigmoid(w0_ref[...] + lo[sb][:, :rw]) for sb in subs]
    cum = [_mm_exact_lhs(lower, lw[sb]) for sb in subs]

    for sb in subs:
        kkn = kkr[sb] * lax.rsqrt(jnp.maximum(kk_sq[sb], 1e-24))
        gate_o[0, rows[sb], :] = lo[sb][:, 2 * rw:]
        bonus_o[0, rows[sb], :] = rk_sum[sb] * pv[sb]
        cum_o[0, rows[sb], :] = cum[sb]
        r_o[0, rows[sb], :] = _bf(pr[sb])
        k_o[0, rows[sb], :] = _bf(k[sb])
        v_o[0, rows[sb], :] = _bf(pv[sb])
        nkk_o[0, rows[sb], :] = _bf(-kkn)
        b_o[0, rows[sb], :] = _bf(kkn * a[sb])


def _rwkv_prep(x, mod3, g_pre, w_rwkv, mu, w_lora, w0, a0, k_k, k_a, r_k, tm):
    bsz, seq, d = x.shape
    rw = RWKV_WIDTH
    row_spec = pl.BlockSpec((1, tm, rw), lambda b, i: (b, i, 0))
    vec = lambda n: pl.BlockSpec((1, n), lambda b, i: (0, 0))
    big = jax.ShapeDtypeStruct((bsz, seq, rw), F32)
    half = jax.ShapeDtypeStruct((bsz, seq, rw), BF16)
    return pl.pallas_call(
        _rwkv_prep_kernel,
        grid=(bsz, seq // tm),
        in_specs=[pl.BlockSpec((1, tm, d), lambda b, i: (b, i, 0)),
                  pl.BlockSpec((1, 1, d), lambda b, i: (b, 0, 0)),
                  pl.BlockSpec((1, 1, d), lambda b, i: (b, 0, 1)),
                  vec(d),
                  pl.BlockSpec((d, SHIFT_WIDTH), lambda b, i: (0, 0), pipeline_mode=pl.Buffered(1)),
                  vec(SHIFT_WIDTH),
                  pl.BlockSpec((LORA_WIDTH, 3 * rw), lambda b, i: (0, 0), pipeline_mode=pl.Buffered(1)),
                  vec(rw), vec(rw), vec(rw), vec(rw), vec(rw)],
        out_specs=[row_spec] * 8,
        out_shape=[half] * 5 + [big] * 3,
        scratch_shapes=[pltpu.VMEM((8, SHIFT_WIDTH), F32)],
        compiler_params=pltpu.CompilerParams(
            dimension_semantics=("arbitrary", "arbitrary"), vmem_limit_bytes=VMEM_LIMIT),
        name="rwkv_prep",
    )(x, mod3, mod3, g_pre, w_rwkv, mu, w_lora, w0, a0, k_k, k_a, r_k)


def _pair_blockdiag(x):
    first = lax.broadcasted_iota(jnp.int32, x.shape, 1) < HEAD_DIM
    zero = jnp.zeros_like(x)
    return jnp.concatenate([jnp.where(first, x, zero), jnp.where(first, zero, x)], axis=0)


def _pmm(a, b):
    return jnp.dot(_bf(a), _pair_blockdiag(_bf(b)), preferred_element_type=F32)


def _pmm_nt(a, b):
    return lax.dot_general(_bf(a), _pair_blockdiag(_bf(b)), (((1,), (1,)), ((), ())),
                           preferred_element_type=F32)


def _pmm_tn(a, b):
    full = lax.dot_general(_bf(a), _bf(b), (((0,), (0,)), ((), ())), preferred_element_type=F32)
    first = lax.broadcasted_iota(jnp.int32, (HEAD_DIM, LANES), 1) < HEAD_DIM
    return jnp.where(first, full[:HEAD_DIM], full[HEAD_DIM:])


def _unit_lower_inverse(a_list, masks):
    eye, diag_mask, merge_masks = masks
    ps = [jnp.where(diag_mask, a, 0.0) for a in a_list]
    ts = [eye + p for p in ps]
    ps = [_pmm(p, p) for p in ps]
    m = 2
    while 2 * m < INV_BASE:
        both = [_pmm(jnp.concatenate([p, t], axis=0), p) for p, t in zip(ps, ts)]
        ps = [x[:CHUNK] for x in both]
        ts = [t + x[CHUNK:] for t, x in zip(ts, both)]
        m *= 2
    ts = [t + _pmm(t, p) for t, p in zip(ts, ps)]
    for mk in merge_masks:
        ws = [_pmm(jnp.where(mk, a, 0.0), t) for a, t in zip(a_list, ts)]
        ts = [t + _pmm(t, w) for t, w in zip(ts, ws)]
    return ts


def _rwkv_scan_kernel(r_ref, k_ref, v_ref, nkk_ref, b_ref, cum_ref, y_ref, st_ref):
    @pl.when(pl.program_id(1) == 0)
    def _():
        st_ref[...] = jnp.zeros_like(st_ref)

    ri = lax.broadcasted_iota(jnp.int32, (CHUNK, LANES), 0)
    ci = _rem(lax.broadcasted_iota(jnp.int32, (CHUNK, LANES), 1), HEAD_DIM)
    strict = ri > ci
    incl = ri >= ci
    ri2 = lax.broadcasted_iota(jnp.int32, (2 * CHUNK, LANES), 0)
    ci2 = _rem(lax.broadcasted_iota(jnp.int32, (2 * CHUNK, LANES), 1), HEAD_DIM)
    rloc = _rem(ri2, CHUNK)
    stacked = (rloc > ci2) | ((rloc == ci2) & (ri2 >= CHUNK))
    eye = jnp.where(ri == ci, 1.0, 0.0)
    diag_mask = _div(ri, INV_BASE) == _div(ci, INV_BASE)
    merge_masks = []
    bsize = INV_BASE
    while bsize < CHUNK:
        merge_masks.append((_div(ri, 2 * bsize) == _div(ci, 2 * bsize))
                           & (_div(ri, bsize) != _div(ci, bsize)))
        bsize *= 2
    masks = (eye, diag_mask, merge_masks)

    n_rows = r_ref.shape[0]
    n_chunks = r_ref.shape[1] // CHUNK
    n_pairs = RWKV_WIDTH // LANES
    seqs = [(b, j) for b in range(n_rows) for j in range(n_pairs)]
    items = [(c, s) for c in range(n_chunks) for s in seqs]

    def tile(ref, c, s):
        b, j = s
        return ref[b, c * CHUNK:(c + 1) * CHUNK, j * LANES:(j + 1) * LANES]

    first_row = ri == 0
    nt = lambda a, b: lax.dot_general(a, b, (((1,), (1,)), ((), ())), preferred_element_type=F32)
    qs, bkg, gcs, ab, akv = {}, {}, {}, {}, {}
    n_groups = min(SCAN_OPERAND_GROUPS, len(items))
    for grp in range(n_groups):
        group = items[grp::n_groups]
        bk, kd = {}, {}
        for it in group:
            cum = tile(cum_ref, *it)
            cum_last = cum[CHUNK - 1:CHUNK, :]
            e_cum = jnp.exp2(cum)
            e_prev = jnp.exp2(jnp.where(first_row, 0.0, pltpu.roll(cum, 1, axis=0)))
            e_inv = jnp.exp2(-cum)
            e_rest = jnp.exp2(cum_last - cum)
            gcs[it] = jnp.exp2(cum_last)
            r, k = tile(r_ref, *it).astype(F32), tile(k_ref, *it).astype(F32)
            nkk, b = tile(nkk_ref, *it).astype(F32), tile(b_ref, *it).astype(F32)
            qs[it] = _bf(jnp.concatenate([nkk * e_prev, r * e_cum], axis=0))
            bk[it], kd[it] = _bf(b * e_inv), _bf(k * e_inv)
            bkg[it] = _bf(jnp.concatenate([b * e_rest, k * e_rest], axis=0))
        abk = {it: nt(qs[it], jnp.concatenate([_pair_blockdiag(bk[it]), _pair_blockdiag(kd[it])], axis=0))
               for it in group}
        for it in group:
            ab[it] = abk[it][:, :LANES]
        ak = {it: jnp.where(stacked, abk[it][:, LANES:], 0.0) for it in group}
        for it in group:
            akv[it] = _pmm(ak[it], tile(v_ref, *it))

    t_list = _unit_lower_inverse([jnp.where(strict, ab[it][:CHUNK], 0.0) for it in items], masks)
    t = dict(zip(items, t_list))

    state = {s: st_ref[s[0], s[1]] for s in seqs}
    for c in range(n_chunks):
        xs = {s: _pmm_nt(qs[(c, s)], state[s]) for s in seqs}
        u = {s: _pmm(t[(c, s)], xs[s][:CHUNK] + akv[(c, s)][:CHUNK]) for s in seqs}
        for s in seqs:
            b, j = s
            y = xs[s][CHUNK:] + akv[(c, s)][CHUNK:] + _pmm(
                jnp.where(incl, ab[(c, s)][CHUNK:], 0.0), u[s])
            y_ref[b, c * CHUNK:(c + 1) * CHUNK, j * LANES:(j + 1) * LANES] = y
        new_state = {}
        for s in seqs:
            uv = jnp.concatenate([_bf(u[s]), tile(v_ref, c, s)], axis=0)
            new_state[s] = state[s] * gcs[(c, s)] + _pmm_tn(uv, bkg[(c, s)])
        state = new_state
    for s in seqs:
        st_ref[s[0], s[1]] = state[s]


def _rwkv_scan(r, k, v, nkk, b, cum, rows_per_step, chunks_per_step):
    bsz, seq, rw = r.shape
    ts = chunks_per_step * CHUNK
    row_spec = pl.BlockSpec((rows_per_step, ts, rw), lambda b, c: (b, c, 0))
    return pl.pallas_call(
        _rwkv_scan_kernel,
        grid=(bsz // rows_per_step, seq // ts),
        in_specs=[row_spec] * 6,
        out_specs=row_spec,
        out_shape=jax.ShapeDtypeStruct((bsz, seq, rw), F32),
        scratch_shapes=[pltpu.VMEM((rows_per_step, rw // LANES, HEAD_DIM, LANES), F32)],
        compiler_params=pltpu.CompilerParams(dimension_semantics=("arbitrary", "arbitrary")),
        name="rwkv_scan",
    )(r, k, v, nkk, b, cum)


def _attn_proj_kernel(x_ref, shift_ref, scale_ref, g_ref, w_ref, cos_ref, sin_ref, *rest):
    outs, buf = rest[:-1], rest[-1]
    SUB_ROWS = ATTN_PROJ_SUB_ROWS
    aw = ATTN_WIDTH
    reps = aw // LANES
    lane = lax.broadcasted_iota(jnp.int32, (SUB_ROWS, aw), 1)
    first_half = _rem(lane, HEAD_DIM) < (HEAD_DIM // 2)
    tiles_per_group = GROUP_WIDTH // LANES

    subs = range(x_ref.shape[1] // SUB_ROWS)
    ps = []
    for sb in subs:
        rows = slice(sb * SUB_ROWS, (sb + 1) * SUB_ROWS)
        h = _norm_modulate(x_ref[0, rows, :], g_ref[...], shift_ref[0], scale_ref[0])
        ps.append(jnp.dot(_bf(h), w_ref[...], preferred_element_type=F32))
    for sb in subs:
        rows = slice(sb * SUB_ROWS, (sb + 1) * SUB_ROWS)
        p = ps[sb]
        cos_t, sin_t = cos_ref[rows, :], sin_ref[rows, :]

        def rope(t, scale):
            cos = jnp.concatenate([cos_t * scale] * reps, axis=-1)
            sin = jnp.concatenate([sin_t * scale] * reps, axis=-1)
            partner = jnp.where(first_half, pltpu.roll(t, aw - HEAD_DIM // 2, axis=1),
                                pltpu.roll(t, HEAD_DIM // 2, axis=1))
            return t * cos + partner * sin

        roped = (rope(p[:, :aw], HEAD_DIM ** -0.5 * LOG2E), rope(p[:, aw:2 * aw], 1.0), p[:, 2 * aw:])
        n = 0
        for which in range(3):
            for g, (_, dil) in enumerate(ATTN_GROUPS):
                o_ref = outs[n]
                n += 1
                out_rows = slice(sb * SUB_ROWS // dil, (sb + 1) * SUB_ROWS // dil)
                for ct in range(tiles_per_group):
                    c0 = g * GROUP_WIDTH + ct * LANES
                    val = roped[which][:, c0:c0 + LANES]
                    if dil == 1:
                        o_ref[0, 0, out_rows, ct * LANES:(ct + 1) * LANES] = _bf(val)
                        continue
                    slot = (which * len(ATTN_GROUPS) + g) * tiles_per_group + ct
                    buf[sb, slot] = val
                    for r in range(dil):
                        o_ref[0, r, out_rows, ct * LANES:(ct + 1) * LANES] = _bf(
                            buf[sb, slot, pl.ds(r, SUB_ROWS // dil, stride=dil), :])


def _attn_proj(x, mod3, g_pre, w_qkv, cos_t, sin_t, tm):
    bsz, seq, d = x.shape
    out_specs, out_shape = [], []
    for _ in range(3):
        for _, dil in ATTN_GROUPS:
            out_specs.append(pl.BlockSpec((1, dil, tm // dil, GROUP_WIDTH), lambda b, i: (b, 0, i, 0)))
            out_shape.append(jax.ShapeDtypeStruct((bsz, dil, seq // dil, GROUP_WIDTH), BF16))
    return pl.pallas_call(
        _attn_proj_kernel,
        grid=(bsz, seq // tm),
        in_specs=[pl.BlockSpec((1, tm, d), lambda b, i: (b, i, 0)),
                  pl.BlockSpec((1, 1, d), lambda b, i: (b, 0, 0)),
                  pl.BlockSpec((1, 1, d), lambda b, i: (b, 0, 1)),
                  pl.BlockSpec((1, d), lambda b, i: (0, 0)),
                  pl.BlockSpec((pl.Element(d), pl.Element(3 * ATTN_WIDTH)), lambda b, i: (0, SHIFT_WIDTH),
                               pipeline_mode=pl.Buffered(1)),
                  pl.BlockSpec((tm, LANES), lambda b, i: (i, 0)),
                  pl.BlockSpec((tm, LANES), lambda b, i: (i, 0))],
        out_specs=out_specs,
        out_shape=out_shape,
        scratch_shapes=[pltpu.VMEM((tm // ATTN_PROJ_SUB_ROWS, 3 * ATTN_WIDTH // LANES,
                                    ATTN_PROJ_SUB_ROWS, LANES), F32)],
        compiler_params=pltpu.CompilerParams(
            dimension_semantics=("arbitrary", "arbitrary"), vmem_limit_bytes=VMEM_LIMIT),
        name="attn_proj",
    )(x, mod3, mod3, g_pre, w_qkv, cos_t, sin_t)


def _attn_kernel(*refs):
    qkv = refs[:9]
    o_ref = refs[9]
    num_ref, m_ref, den_ref = refs[10:]
    blk = ATTN_BLOCK
    qi = lax.broadcasted_iota(jnp.int32, (blk, blk), 0)
    kj = lax.broadcasted_iota(jnp.int32, (blk, blk), 1)
    bias_cur = jnp.where(kj <= qi, 0.0, NEG_BIG)
    bias_prev = jnp.where(kj >= qi, 0.0, NEG_BIG)
    lane = lax.broadcasted_iota(jnp.int32, (blk, LANES), 1)
    head0 = lane < HEAD_DIM

    order = sorted(range(len(ATTN_GROUPS)), key=lambda g: -ATTN_GROUPS[g][1])
    assert ATTN_GROUPS[order[-1]][1] == 1
    for g in order:
        dil = ATTN_GROUPS[g][1]
        q_ref, k_ref, v_ref = qkv[g], qkv[3 + g], qkv[6 + g]
        nb = q_ref.shape[2] // blk

        nper = min(nb, ATTN_BLOCKS_PER_ITER)
        rper = ATTN_BLOCKS_PER_ITER // nper
        whole = nper == nb

        def body(idx, carry, q_ref=q_ref, k_ref=k_ref, v_ref=v_ref, dil=dil, nb=nb, g=g,
                 nper=nper, rper=rper, whole=whole):
            if whole:
                r0, n0 = idx * rper, 0
                base = 0
            else:
                r0 = _div(idx, nb // nper)
                n0 = _rem(idx, nb // nper) * nper
                base = pl.multiple_of(n0 * blk, blk)
            nt = lambda a, b: lax.dot_general(a, b, (((1,), (1,)), ((), ())), preferred_element_type=F32)
            one = jnp.ones((blk, LANES), BF16)
            zero = jnp.zeros((blk, LANES), BF16)
            q_msk, ks, v_aug = {}, {}, {}
            for rr in range(rper):
                r = r0 + rr
                first = 0 if whole else -1
                for i in range(first, nper):
                    if i < 0:
                        st = pl.multiple_of(jnp.maximum(n0 - 1, 0) * blk, blk)
                    else:
                        st = base + i * blk
                    ks[(rr, i)] = k_ref[0, r, pl.ds(st, blk), :]
                    v = v_ref[0, r, pl.ds(st, blk), :]
                    v_aug[(rr, i)] = (jnp.where(head0, v, one), jnp.where(head0, one, v))
                    if i >= 0:
                        q = q_ref[0, r, pl.ds(st, blk), :]
                        q_msk[(rr, i)] = (jnp.where(head0, q, zero), jnp.where(head0, zero, q))
            items = [(rr, i, e) for rr in range(rper) for i in range(nper) for e in range(2)]
            has_prev = lambda it: (it[0], it[1] - 1) in ks
            prev_of = lambda it: (it[0], it[1] - 1)
            k_win = {it[:2]: jnp.concatenate([ks[prev_of(it)], ks[it[:2]]], axis=0)
                     for it in items if has_prev(it)}
            bias_win = jnp.concatenate([bias_prev, bias_cur], axis=1)
            bias_first = bias_win if whole else jnp.where(n0 > 0, bias_win, jnp.concatenate(
                [jnp.full_like(bias_prev, NEG_BIG), bias_cur], axis=1))
            s, ms, ps, pv = {}, {}, {}, {}

            def scores(it):
                s[it] = nt(q_msk[it[:2]][it[2]], k_win[it[:2]] if has_prev(it) else ks[it[:2]])

            def softmax(it):
                if has_prev(it):
                    sb = s.pop(it) + (bias_first if it[1] == 0 else bias_win)
                else:
                    sb = s.pop(it) + bias_cur
                m = jnp.max(sb, axis=-1, keepdims=True)
                ms[it] = m
                ps[it] = _bf(jnp.exp2(sb - m))

            def weighted_values(it):
                if has_prev(it):
                    v_win = jnp.concatenate([v_aug[prev_of(it)][it[2]], v_aug[it[:2]][it[2]]], axis=0)
                else:
                    v_win = v_aug[it[:2]][it[2]]
                pv[it] = jnp.dot(ps.pop(it), v_win, preferred_element_type=F32)

            for it in items:
                scores(it)
            for it in items:
                softmax(it)
            for it in items:
                weighted_values(it)
            for rr in range(rper):
                for i in range(nper):
                    m_b = jnp.where(head0, ms[(rr, i, 0)], ms[(rr, i, 1)])
                    num_b = jnp.where(head0, pv[(rr, i, 0)], pv[(rr, i, 1)])
                    l_b = pltpu.roll(jnp.where(head0, pv[(rr, i, 1)], pv[(rr, i, 0)]), HEAD_DIM, axis=1)
                    if dil == 1:
                        rows = pl.ds(base + i * blk, blk)
                    else:
                        rows = pl.ds((n0 + i) * (blk * dil) + r0 + rr, blk, stride=dil)
                    if g == order[0]:
                        num_ref[rows, :] = num_b
                        m_ref[rows, :] = m_b
                        den_ref[rows, :] = l_b
                        continue
                    m_old = m_ref[rows, :]
                    m_new = jnp.maximum(m_old, m_b)
                    w_old = jnp.exp2(m_old - m_new)
                    w_b = jnp.exp2(m_b - m_new)
                    num = num_ref[rows, :] * w_old + num_b * w_b
                    den = den_ref[rows, :] * w_old + l_b * w_b
                    if g == order[-1]:
                        o_ref[0, rows, :] = _bf(num / den)
                    else:
                        num_ref[rows, :] = num
                        den_ref[rows, :] = den
                        m_ref[rows, :] = m_new
            return carry

        lax.fori_loop(0, dil * nb // (nper * rper), body, 0, unroll=2)


def _attn(qkv, seq):
    bsz = qkv[0].shape[0]
    in_specs = []
    for _ in range(3):
        for _, dil in ATTN_GROUPS:
            in_specs.append(pl.BlockSpec((1, dil, seq // dil, LANES), lambda b, sp: (b, 0, 0, sp)))
    return pl.pallas_call(
        _attn_kernel,
        grid=(bsz, GROUP_WIDTH // LANES),
        in_specs=in_specs,
        out_specs=pl.BlockSpec((1, seq, LANES), lambda b, sp: (b, 0, sp)),
        out_shape=jax.ShapeDtypeStruct((bsz, seq, GROUP_WIDTH), BF16),
        scratch_shapes=[pltpu.VMEM((seq, LANES), F32)] * 3,
        compiler_params=pltpu.CompilerParams(
            dimension_semantics=("arbitrary", "arbitrary"), vmem_limit_bytes=VMEM_LIMIT),
        name="attn",
    )(*qkv)


def _merge_kernel(x_ref, shift_ref, scale_ref, gatem_ref, g_ref, wg_ref, y_ref, bonus_ref, gate_ref,
                  lnw_ref, lnb_ref, wa_ref, o_ref, wb_ref, wout_ref, gpost_ref, out_ref):
    inv_n = 1.0 / HEAD_DIM
    subs = range(x_ref.shape[1] // SUB_ROWS)
    rows = [slice(sb * SUB_ROWS, (sb + 1) * SUB_ROWS) for sb in subs]
    yb = [jnp.dot(o_ref[0, rows[sb], :], wb_ref[...], preferred_element_type=F32) for sb in subs]
    y = [y_ref[0, rows[sb], :] for sb in subs]
    mean = [_head_sum(y[sb]) * inv_n for sb in subs]
    h = [_bf(_norm_modulate(x_ref[0, rows[sb], :], g_ref[...], shift_ref[0], scale_ref[0])) for sb in subs]
    pg = [jnp.dot(h[sb], wg_ref[...], preferred_element_type=F32) for sb in subs]
    yc = [y[sb] - mean[sb] for sb in subs]
    var = [_head_sum(yc[sb] * yc[sb]) * inv_n for sb in subs]
    pre = [_bf((yc[sb] * lax.rsqrt(var[sb] + GN_EPS) * lnw_ref[...] + lnb_ref[...]
                + bonus_ref[0, rows[sb], :]) * gate_ref[0, rows[sb], :]) for sb in subs]
    ya = [jnp.dot(pre[sb], wa_ref[...], preferred_element_type=F32) for sb in subs]
    merged = [_bf(_sigmoid(pg[sb][:, :D_MODEL]) * ya[sb] + _sigmoid(pg[sb][:, D_MODEL:]) * yb[sb])
              for sb in subs]
    zz = [jnp.dot(merged[sb], wout_ref[...], preferred_element_type=F32) for sb in subs]
    for sb in subs:
        out_ref[0, rows[sb], :] = x_ref[0, rows[sb], :] + gatem_ref[0] * _rms(zz[sb], gpost_ref[...])


def _merge(x, mod3, g_pre, w_g, y, bonus, gate, lnx_w, lnx_b, w_a, o_att, w_b, w_out, g_post, tm):
    bsz, seq, d = x.shape
    rw = RWKV_WIDTH
    const = lambda shape: pl.BlockSpec(shape, lambda b, i: (0, 0), pipeline_mode=pl.Buffered(1))
    row = lambda n: pl.BlockSpec((1, tm, n), lambda b, i: (b, i, 0))
    modv = lambda j: pl.BlockSpec((1, 1, d), lambda b, i: (b, 0, j))
    return pl.pallas_call(
        _merge_kernel,
        grid=(bsz, seq // tm),
        in_specs=[row(d), modv(0), modv(1), modv(2), const((1, d)),
                  pl.BlockSpec((pl.Element(d), pl.Element(2 * d)),
                               lambda b, i: (0, SHIFT_WIDTH + 3 * ATTN_WIDTH),
                               pipeline_mode=pl.Buffered(1)),
                  row(rw), row(rw), row(rw), const((1, rw)), const((1, rw)), const((rw, d)),
                  row(GROUP_WIDTH), const((GROUP_WIDTH, d)), const((d, d)), const((1, d))],
        out_specs=row(d),
        out_shape=jax.ShapeDtypeStruct((bsz, seq, d), F32),
        compiler_params=pltpu.CompilerParams(
            dimension_semantics=("arbitrary", "arbitrary"), vmem_limit_bytes=VMEM_LIMIT),
        name="merge",
    )(x, mod3, mod3, mod3, g_pre, w_g, y, bonus, gate, lnx_w, lnx_b, w_a, o_att, w_b, w_out, g_post)


def _ffn_kernel(x_ref, shift_ref, scale_ref, gatef_ref, g_ref, win_ref, wout_ref, gpost_ref, out_ref):
    dff = wout_ref.shape[0]
    subs = range(x_ref.shape[1] // SUB_ROWS)
    rows = [slice(sb * SUB_ROWS, (sb + 1) * SUB_ROWS) for sb in subs]
    h = [_bf(_norm_modulate(x_ref[0, rows[sb], :], g_ref[...], shift_ref[0], scale_ref[0])) for sb in subs]
    u = [jnp.dot(h[sb], win_ref[:, :dff], preferred_element_type=F32) for sb in subs]
    gt = [jnp.dot(h[sb], win_ref[:, dff:], preferred_element_type=F32) for sb in subs]
    act = [_bf(u[sb] * _sigmoid(u[sb]) * gt[sb]) for sb in subs]
    f = [jnp.dot(act[sb], wout_ref[...], preferred_element_type=F32) for sb in subs]
    for sb in subs:
        out_ref[0, rows[sb], :] = x_ref[0, rows[sb], :] + gatef_ref[0] * _rms(f[sb], gpost_ref[...])


def _ffn(x, mod3, g_pre, w_in, w_out, g_post, tm):
    bsz, seq, d = x.shape
    dff = w_out.shape[0]
    const = lambda shape: pl.BlockSpec(shape, lambda b, i: (0, 0), pipeline_mode=pl.Buffered(1))
    row = pl.BlockSpec((1, tm, d), lambda b, i: (b, i, 0))
    modv = lambda j: pl.BlockSpec((1, 1, d), lambda b, i: (b, 0, j))
    return pl.pallas_call(
        _ffn_kernel,
        grid=(bsz, seq // tm),
        in_specs=[row, modv(3), modv(4), modv(5), const((1, d)), const((d, 2 * dff)),
                  const((dff, d)), const((1, d))],
        out_specs=row,
        out_shape=jax.ShapeDtypeStruct((bsz, seq, d), F32),
        compiler_params=pltpu.CompilerParams(
            dimension_semantics=("arbitrary", "arbitrary"), vmem_limit_bytes=VMEM_LIMIT),
        name="ffn",
    )(x, mod3, mod3, mod3, g_pre, w_in, w_out, g_post)


def _rope_tables(seq):
    half = HEAD_DIM // 2
    inv_freq = ROPE_THETA ** (-jnp.arange(half, dtype=F32) / half)
    ang = jnp.arange(seq, dtype=F32)[:, None] * inv_freq[None, :]
    cos, sin = jnp.cos(ang), jnp.sin(ang)
    reps = LANES // HEAD_DIM
    cos_t = jnp.tile(jnp.concatenate([cos, cos], axis=-1), (1, reps))
    sin_t = jnp.tile(jnp.concatenate([-sin, sin], axis=-1), (1, reps))
    return cos_t, sin_t


def _layer(x, c, w_ada, b_ada, g_pre_mix, g_post_mix, w_in, mu_shift, w0, w2, a0, a2, g2,
           k_k, k_a, r_k, lnx_w, lnx_b, w_a, w_b, w_out, g_pre_ffn, g_post_ffn, w_ffn_in, w_ffn_out):
    bsz, seq, d = x.shape
    rw = RWKV_WIDTH
    row1 = lambda t: t.reshape(1, -1)
    mod3 = _mod(c, w_ada, b_ada).reshape(bsz, 1, 6 * d)

    w_rwkv = w_qkv = w_gates = _bf(w_in)
    w_lora = jnp.zeros((LORA_WIDTH, 3 * rw), F32)
    w_lora = w_lora.at[:DECAY_LORA, :rw].set(w2)
    w_lora = w_lora.at[DECAY_LORA:DECAY_LORA + ICLR_LORA, rw:2 * rw].set(a2)
    w_lora = _bf(w_lora.at[DECAY_LORA + ICLR_LORA:, 2 * rw:].set(g2))

    prep = _rwkv_prep(x, mod3, row1(g_pre_mix), w_rwkv, row1(mu_shift), w_lora, row1(w0), row1(a0),
                      row1(k_k), row1(k_a), row1(r_k), tm=512)
    r, k, v, nkk, b, cum, bonus, gate = prep
    y = _rwkv_scan(r, k, v, nkk, b, cum, rows_per_step=8, chunks_per_step=1)

    cos_t, sin_t = _rope_tables(seq)
    qkv = _attn_proj(x, mod3, row1(g_pre_mix), w_qkv, cos_t, sin_t, tm=1024)
    o_att = _attn(qkv, seq)

    x1 = _merge(x, mod3, row1(g_pre_mix), w_gates, y, bonus, gate, row1(lnx_w), row1(lnx_b),
                _bf(w_a), o_att, _bf(w_b), _bf(w_out), row1(g_post_mix), tm=512)
    return _ffn(x1, mod3, row1(g_pre_ffn), _bf(w_ffn_in), _bf(w_ffn_out), row1(g_post_ffn), tm=512)


def kernel(x, c, w_ada, b_ada, g_pre_mix, g_post_mix, w_in, mu_shift, w0, w2, a0, a2, g2, k_k, k_a, r_k,
           lnx_w, lnx_b, w_a, w_b, w_out, g_pre_ffn, g_post_ffn, w_ffn_in, w_ffn_out):
    for i in range(w_in.shape[0]):
        x = _layer(x, c, w_ada[i], b_ada[i], g_pre_mix[i], g_post_mix[i], w_in[i], mu_shift[i], w0[i],
                   w2[i], a0[i], a2[i], g2[i], k_k[i], k_a[i], r_k[i], lnx_w[i], lnx_b[i], w_a[i], w_b[i],
                   w_out[i], g_pre_ffn[i], g_post_ffn[i], w_ffn_in[i], w_ffn_out[i])
    return x
```
